```python
import math
import jax
import jax.numpy as jnp
from jax import lax
import numpy as np

D_MODEL = 2048
BATCH = 16
SEQ = 256
DEPTH = 4
DEC_BATCH = 8
DEC_SEQ = 2048
PAST_LEN = 512

GRID_W = 64
CHUNK = 64
SHORT_CONV = 3
H_A = 8
DK_A = 128
DV_A = 128
QK_A = H_A * DK_A
V_A = H_A * DV_A
H_B = 4
DQK_B = 128
DV_B = 256
QK_B = H_B * DQK_B
V_B = H_B * DV_B
HY_CH = 1024
HY_ORDER = 2
HY_BANDS = 16
HY_FEAT = 1 + 2 * HY_BANDS
HY_HID = 64
N_EXPERTS = 16
D_EXPERT = 1024
CAP_FACTOR = 2
DN_ALPHA = (2 * DEPTH) ** 0.25
DN_BETA = (8 * DEPTH) ** -0.25
LN_EPS = 1e-6
IN_SIZES = (QK_A, QK_A, V_A, V_A, 2 * H_A, 2 * H_A,
            QK_B, QK_B, V_B, V_B, 2 * H_B, 2 * H_B) + (HY_CH,) * (HY_ORDER + 1) + (D_MODEL,) * 3
N_IN = sum(IN_SIZES)
F32 = jnp.float32

kernel_name = 'hybrid_gdn_mlstm_hyena_ec_diffusion_step'


def _ln_stats(x):
    xf = x.astype(F32)
    mu = jnp.mean(xf, -1, keepdims=True)
    var = jnp.mean(jnp.square(xf - mu), -1, keepdims=True)
    return (xf - mu) * lax.rsqrt(var + LN_EPS)


def layer_norm_plain(x):
    return _ln_stats(x).astype(x.dtype)


def layer_norm_affine(x, g, b):
    return (_ln_stats(x) * g.astype(F32) + b.astype(F32)).astype(x.dtype)


def rms_norm(x, w):
    xf = x.astype(F32)
    return (xf * lax.rsqrt(jnp.mean(xf * xf, -1, keepdims=True) + LN_EPS) * w.astype(F32)).astype(x.dtype)


def l2norm(x):
    xf = x.astype(F32)
    return (xf * lax.rsqrt(jnp.sum(xf * xf, -1, keepdims=True) + LN_EPS)).astype(x.dtype)


def _rev(t):
    return jnp.flip(t, axis=1)


def centred_conv(x, w):
    k = w.shape[0]
    p = k // 2
    L = x.shape[1]
    xp = jnp.pad(x, ((0, 0), (p, p), (0, 0)))
    y = xp[:, 0:L] * w[0]
    for j in range(1, k):
        y = y + xp[:, j:j + L] * w[j]
    return y


def _chunk(t):
    b, L, h = t.shape[:3]
    t = t.reshape((b, L // CHUNK, CHUNK, h) + t.shape[3:])
    return jnp.moveaxis(t, 3, 1)


def _unchunk(t):
    b, h, n, c = t.shape[:4]
    return jnp.moveaxis(t, 1, 3).reshape((b, n * c, h) + t.shape[4:])


def gdn_chunked(q, k, v, beta, loga, s0):
    out_dtype = v.dtype
    q, k, v, beta, loga, s0 = (t.astype(F32) for t in (q, k, v, beta, loga, s0))
    dk = q.shape[-1]
    qc, kc, vc = _chunk(q), _chunk(k), _chunk(v)
    bc, g = _chunk(beta), jnp.cumsum(_chunk(loga), -1)
    incl = jnp.tril(jnp.ones((CHUNK, CHUNK), bool))
    strict = jnp.tril(jnp.ones((CHUNK, CHUNK), bool), k=-1)
    dec_incl = jnp.exp(jnp.where(incl, g[..., :, None] - g[..., None, :], -jnp.inf))
    dec_strict = jnp.where(strict, dec_incl, 0.0)
    lmat = bc[..., :, None] * jnp.einsum('bhncd,bhnsd->bhncs', kc, kc) * dec_strict
    rhs = jnp.concatenate([(bc * jnp.exp(g))[..., None] * kc, bc[..., None] * vc], -1)
    sol = lax.linalg.triangular_solve(jnp.eye(CHUNK, dtype=F32) + lmat, rhs,
                                      left_side=True, lower=True, unit_diagonal=True)
    w, u0 = sol[..., :dk], sol[..., dk:]
    pmat = jnp.einsum('bhncd,bhnsd->bhncs', qc, kc) * dec_incl
    glast = g[..., -1]
    kd = jnp.exp(glast[..., None] - g)[..., None] * kc

    def step(s, xs):
        w_, u_, kd_, gl = xs
        uc = u_ - jnp.einsum('bhck,bhkv->bhcv', w_, s)
        s_new = jnp.exp(gl)[..., None, None] * s + jnp.einsum('bhck,bhcv->bhkv', kd_, uc)
        return s_new, (s, uc)

    xs = tuple(jnp.moveaxis(t, 2, 0) for t in (w, u0, kd, glast))
    s_fin, (s_in, uc) = lax.scan(step, s0, xs)
    s_in, uc = jnp.moveaxis(s_in, 0, 2), jnp.moveaxis(uc, 0, 2)
    o = (jnp.einsum('bhncd,bhndv->bhncv', jnp.exp(g)[..., None] * qc, s_in)
         + jnp.einsum('bhncs,bhnsv->bhncv', pmat, uc))
    return _unchunk(o).astype(out_dtype), s_fin


def mlstm_chunked(q, k, v, ig, lf, c0, n0, m0):
    out_dtype = v.dtype
    q, k, v, ig, lf, c0, n0, m0 = (t.astype(F32) for t in (q, k, v, ig, lf, c0, n0, m0))
    qc, kc, vc = _chunk(q), _chunk(k), _chunk(v)
    igc, bcum = _chunk(ig), jnp.cumsum(_chunk(lf), -1)
    incl = jnp.tril(jnp.ones((CHUNK, CHUNK), bool))
    dlog = jnp.where(incl, bcum[..., :, None] - bcum[..., None, :] + igc[..., None, :], -jnp.inf)
    dmax = jnp.max(dlog, -1)
    pw = jnp.exp(dlog - dmax[..., None]) * jnp.einsum('bhncd,bhnsd->bhncs', qc, kc)
    intra_num = jnp.einsum('bhncs,bhnsv->bhncv', pw, vc)
    intra_den = jnp.sum(pw, -1)
    blast = bcum[..., -1]
    wlog = blast[..., None] - bcum + igc
    wmax = jnp.max(wlog, -1)
    ewk = jnp.exp(wlog - wmax[..., None])[..., None] * kc
    dc = jnp.einsum('bhnck,bhncv->bhnkv', ewk, vc)
    dn = jnp.sum(ewk, -2)

    def step(carry, xs):
        cm, nm, mm = carry
        dc_, dn_, bl, wm = xs
        m_new = jnp.maximum(bl + mm, wm)
        a = jnp.exp(bl + mm - m_new)
        e = jnp.exp(wm - m_new)
        c_new = a[..., None, None] * cm + e[..., None, None] * dc_
        n_new = a[..., None] * nm + e[..., None] * dn_
        return (c_new, n_new, m_new), (cm, nm, mm)

    xs = tuple(jnp.moveaxis(t, 2, 0) for t in (dc, dn, blast, wmax))
    (cf, nf, mf), (cs, ns, ms) = lax.scan(step, (c0, n0, m0), xs)
    cs, ns, ms = jnp.moveaxis(cs, 0, 2), jnp.moveaxis(ns, 0, 2), jnp.moveaxis(ms, 0, 2)
    alog = bcum + ms[..., None]
    mt = jnp.maximum(alog, dmax)
    wi, wa = jnp.exp(alog - mt), jnp.exp(dmax - mt)
    num = wi[..., None] * jnp.einsum('bhncd,bhndv->bhncv', qc, cs) + wa[..., None] * intra_num
    den = wi * jnp.einsum('bhncd,bhnd->bhnc', qc, ns) + wa * intra_den
    h = num / jnp.maximum(jnp.abs(den), jnp.exp(-mt))[..., None]
    return _unchunk(h).astype(out_dtype), (cf, nf, mf)


def hyena_filters(L, w1, b1, w2, b2, w3, b3, freq, rate):
    pos = jnp.arange(L, dtype=F32)
    t = pos / L
    ang = (2.0 * math.pi) * t[:, None] * jnp.arange(1, HY_BANDS + 1, dtype=F32)
    feats = jnp.concatenate([t[:, None], jnp.sin(ang), jnp.cos(ang)], -1)
    z = jnp.sin(freq[0].astype(F32) * (feats @ w1.astype(F32) + b1.astype(F32)))
    z = jnp.sin(freq[1].astype(F32) * (z @ w2.astype(F32) + b2.astype(F32)))
    filt = z @ w3.astype(F32) + b3.astype(F32)
    lag = jnp.abs(pos - L // 2) / L
    filt = filt * jnp.exp(-lag[:, None] * rate.astype(F32))
    return filt.reshape(L, HY_ORDER, HY_CH)


def long_conv(u, filt):
    L = u.shape[1]
    uf = jnp.fft.rfft(u.astype(F32), n=2 * L, axis=1)
    ff = jnp.fft.rfft(filt.astype(F32), n=2 * L, axis=0)
    y = jnp.fft.irfft(uf * ff[None], n=2 * L, axis=1)
    return y[:, L // 2: L // 2 + L].astype(u.dtype)


def grid_pos_embed(n_tokens):
    rows = n_tokens // GRID_W
    quarter = D_MODEL // 4
    omega = 1.0 / (10000.0 ** (jnp.arange(quarter, dtype=F32) / quarter))
    r = jnp.broadcast_to(jnp.arange(rows, dtype=F32)[:, None, None] * omega, (rows, GRID_W, quarter))
    cl = jnp.broadcast_to(jnp.arange(GRID_W, dtype=F32)[None, :, None] * omega, (rows, GRID_W, quarter))
    pe = jnp.concatenate([jnp.sin(r), jnp.cos(r), jnp.sin(cl), jnp.cos(cl)], -1)
    return pe.reshape(rows * GRID_W, D_MODEL)


def parallel_mixer(h, lp, s_gdn, s_c, s_n, s_m):
    b, L, _ = h.shape
    proj = h @ lp['w_in']
    parts = jnp.split(proj, [int(i) for i in np.cumsum(IN_SIZES)[:-1]], axis=-1)
    qa, ka, va, za, beta_p, alpha_p, qb, kb, vb, ob, ig_p, fg_p = parts[:12]
    hy_in = parts[12:13 + HY_ORDER]
    g_a, g_b, g_c = parts[13 + HY_ORDER:]

    qkv = jax.nn.silu(centred_conv(jnp.concatenate([qa, ka, va], -1), lp['gdn_conv']))
    q = l2norm(qkv[..., :QK_A].reshape(b, L, H_A, DK_A)) * (DK_A ** -0.5)
    k = l2norm(qkv[..., QK_A:2 * QK_A].reshape(b, L, H_A, DK_A))
    v = qkv[..., 2 * QK_A:].reshape(b, L, H_A, DV_A)
    beta = jax.nn.sigmoid(beta_p.astype(F32)).reshape(b, L, 2, H_A)
    loga = -jnp.exp(lp['gdn_a_log'].astype(F32)) * jax.nn.softplus(
        alpha_p.astype(F32).reshape(b, L, 2, H_A) + lp['gdn_dt_bias'].astype(F32))
    o_f, sf = gdn_chunked(q, k, v, beta[:, :, 0], loga[:, :, 0], s_gdn[:, 0])
    o_b, sb = gdn_chunked(_rev(q), _rev(k), _rev(v), _rev(beta[:, :, 1]), _rev(loga[:, :, 1]), s_gdn[:, 1])
    o = rms_norm(o_f + _rev(o_b), lp['gdn_norm']) * jax.nn.silu(za.reshape(b, L, H_A, DV_A))
    y_a = o.reshape(b, L, V_A)

    q = qb.reshape(b, L, H_B, DQK_B)
    k = kb.reshape(b, L, H_B, DQK_B) * (DQK_B ** -0.5)
    v = vb.reshape(b, L, H_B, DV_B)
    ig = ig_p.astype(F32).reshape(b, L, 2, H_B) + lp['ml_i_bias'].astype(F32)
    lf = jax.nn.log_sigmoid(fg_p.astype(F32).reshape(b, L, 2, H_B) + lp['ml_f_bias'].astype(F32))
    h_f, (cf, nf, mf) = mlstm_chunked(q, k, v, ig[:, :, 0], lf[:, :, 0], s_c[:, 0], s_n[:, 0], s_m[:, 0])
    h_b, (cb, nb, mb) = mlstm_chunked(_rev(q), _rev(k), _rev(v), _rev(ig[:, :, 1]), _rev(lf[:, :, 1]),
                                      s_c[:, 1], s_n[:, 1], s_m[:, 1])
    y_b = (rms_norm(h_f + _rev(h_b), lp['ml_norm'])
           * jax.nn.sigmoid(ob.reshape(b, L, H_B, DV_B))).reshape(b, L, V_B)

    u = jnp.split(centred_conv(jnp.concatenate(hy_in, -1), lp['hy_conv']), HY_ORDER + 1, axis=-1)
    filt = hyena_filters(L, lp['hy_w1'], lp['hy_b1'], lp['hy_w2'], lp['hy_b2'], lp['hy_w3'], lp['hy_b3'],
                         lp['hy_freq'], lp['hy_rate'])
    z = u[0]
    for order in range(HY_ORDER):
        z = u[order + 1] * (long_conv(z, filt[:, order]) + lp['hy_skip'][order] * z)
    y_c = z

    merged = (jax.nn.sigmoid(g_a) * (y_a @ lp['w_br_a'])
              + jax.nn.sigmoid(g_b) * (y_b @ lp['w_br_b'])
              + jax.nn.sigmoid(g_c) * (y_c @ lp['w_br_c']))
    out = merged @ lp['w_out']
    states = (jnp.stack([sf, sb], 1), jnp.stack([cf, cb], 1), jnp.stack([nf, nb], 1), jnp.stack([mf, mb], 1))
    return out, states


def expert_choice_ffn(h, w_router, b_router, w_gate, w_up, w_down):
    b, L, d = h.shape
    n = b * L
    cap = CAP_FACTOR * n // N_EXPERTS
    hf = h.reshape(n, d)
    aff = jax.nn.softmax(hf.astype(F32) @ w_router.astype(F32) + b_router.astype(F32), axis=-1)
    g, idx = lax.top_k(aff.T, cap)
    xe = hf[idx]
    hid = jax.nn.silu(jnp.einsum('ecd,edf->ecf', xe, w_gate)) * jnp.einsum('ecd,edf->ecf', xe, w_up)
    ye = jnp.einsum('ecf,efd->ecd', hid, w_down) * g[..., None].astype(h.dtype)
    y = jnp.zeros_like(hf).at[idx.reshape(-1)].add(ye.reshape(-1, d))
    return y.reshape(b, L, d)


def trunk_layer(x, mod, lp, s_gdn, s_c, s_n, s_m):
    shift1, scale1, gate1, shift2, scale2, gate2 = jnp.split(mod, 6, axis=-1)
    h = layer_norm_plain(x) * (1.0 + scale1) + shift1
    y, states = parallel_mixer(h, lp, s_gdn, s_c, s_n, s_m)
    x = layer_norm_affine(DN_ALPHA * x + (1.0 + gate1) * y, lp['ln1_g'], lp['ln1_b'])
    h = layer_norm_plain(x) * (1.0 + scale2) + shift2
    y = expert_choice_ffn(h, lp['w_router'], lp['b_router'], lp['w_gate'], lp['w_up'], lp['w_down'])
    x = layer_norm_affine(DN_ALPHA * x + (1.0 + gate2) * y, lp['ln2_g'], lp['ln2_b'])
    return x, states


def setup_inputs(seed: int = 0) -> dict:
    key = jax.random.key(seed)
    keys = iter(jax.random.split(key, 64))

    def nrm(shape, scale):
        return jax.random.normal(next(keys), shape, F32) * scale

    def unif(shape, lo, hi):
        return jax.random.uniform(next(keys), shape, F32, lo, hi)

    dt = jnp.exp(unif((DEPTH, 2, H_A), math.log(1e-3), math.log(1e-1)))
    return {
        'x_prompt': nrm((BATCH, SEQ, D_MODEL), 1.0),
        'x_sample': nrm((DEC_BATCH, DEC_SEQ, D_MODEL), 1.0),
        'state_gdn': nrm((DEC_BATCH, DEPTH, 2, H_A, DK_A, DV_A), 0.1),
        'state_mlstm_c': nrm((DEC_BATCH, DEPTH, 2, H_B, DQK_B, DV_B), 0.1),
        'state_mlstm_n': nrm((DEC_BATCH, DEPTH, 2, H_B, DQK_B), 0.1),
        'state_mlstm_m': nrm((DEC_BATCH, DEPTH, 2, H_B), 0.5),
        'c': nrm((DEC_BATCH, D_MODEL), 1.0),
        'c_ctx': nrm((D_MODEL,), 1.0),
        'w_ada': nrm((DEPTH, D_MODEL, 6 * D_MODEL), 0.1 * D_MODEL ** -0.5),
        'b_ada': nrm((DEPTH, 6 * D_MODEL), 0.01),
        'w_in': nrm((DEPTH, D_MODEL, N_IN), D_MODEL ** -0.5),
        'gdn_conv': nrm((DEPTH, SHORT_CONV, 2 * QK_A + V_A), SHORT_CONV ** -0.5),
        'gdn_a_log': jnp.log(unif((DEPTH, 2, H_A), 1.0, 16.0)),
        'gdn_dt_bias': dt + jnp.log(-jnp.expm1(-dt)),
        'gdn_norm': 1.0 + nrm((DEPTH, DV_A), 0.02),
        'ml_i_bias': nrm((DEPTH, 2, H_B), 0.1),
        'ml_f_bias': unif((DEPTH, 2, H_B), 3.0, 6.0),
        'ml_norm': 1.0 + nrm((DEPTH, DV_B), 0.02),
        'hy_conv': nrm((DEPTH, SHORT_CONV, (HY_ORDER + 1) * HY_CH), SHORT_CONV ** -0.5),
        'hy_w1': nrm((DEPTH, HY_FEAT, HY_HID), HY_FEAT ** -0.5),
        'hy_b1': nrm((DEPTH, HY_HID), 0.1),
        'hy_w2': nrm((DEPTH, HY_HID, HY_HID), HY_HID ** -0.5),
        'hy_b2': nrm((DEPTH, HY_HID), 0.1),
        'hy_w3': nrm((DEPTH, HY_HID, HY_ORDER * HY_CH), 0.1 * HY_HID ** -0.5),
        'hy_b3': nrm((DEPTH, HY_ORDER * HY_CH), 0.01),
        'hy_freq': 1.0 + nrm((DEPTH, 2, HY_HID), 0.1),
        'hy_rate': jnp.exp(unif((DEPTH, HY_ORDER * HY_CH), math.log(3.0), math.log(15.0))),
        'hy_skip': 1.0 + nrm((DEPTH, HY_ORDER, HY_CH), 0.1),
        'w_br_a': nrm((DEPTH, V_A, D_MODEL), V_A ** -0.5),
        'w_br_b': nrm((DEPTH, V_B, D_MODEL), V_B ** -0.5),
        'w_br_c': nrm((DEPTH, HY_CH, D_MODEL), HY_CH ** -0.5),
        'w_out': nrm((DEPTH, D_MODEL, D_MODEL), DN_BETA * D_MODEL ** -0.5),
        'ln1_g': 1.0 + nrm((DEPTH, D_MODEL), 0.02),
        'ln1_b': nrm((DEPTH, D_MODEL), 0.02),
        'ln2_g': 1.0 + nrm((DEPTH, D_MODEL), 0.02),
        'ln2_b': nrm((DEPTH, D_MODEL), 0.02),
        'w_router': nrm((DEPTH, D_MODEL, N_EXPERTS), D_MODEL ** -0.5),
        'b_router': nrm((DEPTH, N_EXPERTS), 0.01),
        'w_gate': nrm((DEPTH, N_EXPERTS, D_MODEL, D_EXPERT), D_MODEL ** -0.5),
        'w_up': nrm((DEPTH, N_EXPERTS, D_MODEL, D_EXPERT), D_MODEL ** -0.5),
        'w_down': nrm((DEPTH, N_EXPERTS, D_EXPERT, D_MODEL), DN_BETA * D_EXPERT ** -0.5),
    }


def reference(x_prompt, x_sample, state_gdn, state_mlstm_c, state_mlstm_n, state_mlstm_m, c, c_ctx,
              w_ada, b_ada, w_in, gdn_conv, gdn_a_log, gdn_dt_bias, gdn_norm, ml_i_bias, ml_f_bias, ml_norm,
              hy_conv, hy_w1, hy_b1, hy_w2, hy_b2, hy_w3, hy_b3, hy_freq, hy_rate, hy_skip,
              w_br_a, w_br_b, w_br_c, w_out, ln1_g, ln1_b, ln2_g, ln2_b,
              w_router, b_router, w_gate, w_up, w_down):
    bp = x_prompt.shape[0]
    ls = x_sample.shape[1]
    xp = x_prompt
    xs = x_sample + grid_pos_embed(ls).astype(x_sample.dtype)[None]
    zero_gdn = jnp.zeros((bp, 2, H_A, DK_A, DV_A), F32)
    zero_c = jnp.zeros((bp, 2, H_B, DQK_B, DV_B), F32)
    zero_n = jnp.zeros((bp, 2, H_B, DQK_B), F32)
    zero_m = jnp.zeros((bp, 2, H_B), F32)
    ctx_gdn, ctx_c, ctx_n, ctx_m = [], [], [], []
    for l in range(DEPTH):
        lp = {
            'w_in': w_in[l], 'gdn_conv': gdn_conv[l], 'gdn_a_log': gdn_a_log[l], 'gdn_dt_bias': gdn_dt_bias[l],
            'gdn_norm': gdn_norm[l], 'ml_i_bias': ml_i_bias[l], 'ml_f_bias': ml_f_bias[l], 'ml_norm': ml_norm[l],
            'hy_conv': hy_conv[l], 'hy_w1': hy_w1[l], 'hy_b1': hy_b1[l], 'hy_w2': hy_w2[l], 'hy_b2': hy_b2[l],
            'hy_w3': hy_w3[l], 'hy_b3': hy_b3[l], 'hy_freq': hy_freq[l], 'hy_rate': hy_rate[l],
            'hy_skip': hy_skip[l], 'w_br_a': w_br_a[l], 'w_br_b': w_br_b[l], 'w_br_c': w_br_c[l],
            'w_out': w_out[l], 'ln1_g': ln1_g[l], 'ln1_b': ln1_b[l], 'ln2_g': ln2_g[l], 'ln2_b': ln2_b[l],
            'w_router': w_router[l], 'b_router': b_router[l], 'w_gate': w_gate[l], 'w_up': w_up[l],
            'w_down': w_down[l],
        }
        mod_ctx = (jax.nn.silu(c_ctx) @ w_ada[l] + b_ada[l]).reshape(1, 1, 6 * D_MODEL)
        mod_lat = (jax.nn.silu(c) @ w_ada[l] + b_ada[l])[:, None, :]
        xp, (sg, sc, sn, sm) = trunk_layer(xp, mod_ctx, lp, zero_gdn, zero_c, zero_n, zero_m)
        ctx_gdn.append(sg)
        ctx_c.append(sc)
        ctx_n.append(sn)
        ctx_m.append(sm)
        xs, _ = trunk_layer(xs, mod_lat, lp, state_gdn[:, l], state_mlstm_c[:, l],
                            state_mlstm_n[:, l], state_mlstm_m[:, l])
    new_state_gdn = jnp.stack(ctx_gdn, 1).astype(x_prompt.dtype)
    new_state_mlstm_c = jnp.stack(ctx_c, 1).astype(x_prompt.dtype)
    new_state_mlstm_n = jnp.stack(ctx_n, 1).astype(x_prompt.dtype)
    new_state_mlstm_m = jnp.stack(ctx_m, 1).astype(x_prompt.dtype)
    return (xp, xs, new_state_gdn, new_state_mlstm_c, new_state_mlstm_n, new_state_mlstm_m)
```

```python
import functools
import math

import jax
import jax.numpy as jnp
import numpy as np
from jax import lax
from jax.experimental import pallas as pl
from jax.experimental.pallas import tpu as pltpu

D_MODEL = 2048
DEPTH = 4
GRID_W = 64
CHUNK = 64
H_A = 8
DK_A = 128
DV_A = 128
QK_A = H_A * DK_A
V_A = H_A * DV_A
H_B = 4
DQK_B = 128
DV_B = 256
QK_B = H_B * DQK_B
V_B = H_B * DV_B
HY_CH = 1024
HY_ORDER = 2
HY_BANDS = 16
N_EXPERTS = 16
D_EXPERT = 1024
CAP_FACTOR = 2
DN_ALPHA = (2 * DEPTH) ** 0.25
LN_EPS = 1e-6
F32 = jnp.float32
BF16 = jnp.bfloat16

VMEM_LIMIT_BYTES = 48 * 1024 * 1024


def _mm_kernel(x_ref, w_ref, o_ref):
    o_ref[...] = jnp.dot(x_ref[...], w_ref[...], preferred_element_type=F32).astype(o_ref.dtype)


def _pick(n, pref):
    for t in pref:
        if n % t == 0:
            return t
    return n


def pmm(x, w, out_dtype=F32):
    m, k = x.shape
    _, n = w.shape
    tm = _pick(m, (1024, 512, 256, 128, 16))
    tn = _pick(n, (512, 256, 128))
    return pl.pallas_call(
        _mm_kernel,
        grid=(m // tm, n // tn),
        in_specs=[pl.BlockSpec((tm, k), lambda i, j: (i, 0)),
                  pl.BlockSpec((k, tn), lambda i, j: (0, j))],
        out_specs=pl.BlockSpec((tm, tn), lambda i, j: (i, j)),
        out_shape=jax.ShapeDtypeStruct((m, n), out_dtype),
        compiler_params=pltpu.CompilerParams(
            dimension_semantics=("parallel", "parallel"), vmem_limit_bytes=VMEM_LIMIT_BYTES),
        name="pmm",
    )(x.astype(BF16), w.astype(BF16))


def pbmm(x, w, out_dtype=F32):
    e, m, k = x.shape
    _, _, n = w.shape
    tm = _pick(m, (1024, 512, 256, 128, 8))
    tn = _pick(n, (512, 256, 128))
    return pl.pallas_call(
        _mm_kernel,
        grid=(e, m // tm, n // tn),
        in_specs=[pl.BlockSpec((None, tm, k), lambda b, i, j: (b, i, 0)),
                  pl.BlockSpec((None, k, tn), lambda b, i, j: (b, 0, j))],
        out_specs=pl.BlockSpec((None, tm, tn), lambda b, i, j: (b, i, j)),
        out_shape=jax.ShapeDtypeStruct((e, m, n), out_dtype),
        compiler_params=pltpu.CompilerParams(
            dimension_semantics=("parallel", "parallel", "parallel"), vmem_limit_bytes=VMEM_LIMIT_BYTES),
        name="pbmm",
    )(x.astype(BF16), w.astype(BF16))


def _ln_stats(x):
    mu = jnp.mean(x, -1, keepdims=True)
    var = jnp.mean(jnp.square(x - mu), -1, keepdims=True)
    return (x - mu) * lax.rsqrt(var + LN_EPS)


def rms_norm(x, w):
    return x * lax.rsqrt(jnp.mean(x * x, -1, keepdims=True) + LN_EPS) * w


def l2norm(x):
    return x * lax.rsqrt(jnp.sum(x * x, -1, keepdims=True) + LN_EPS)


def _rev(t):
    return jnp.flip(t, axis=1)


def centred_conv(x, w):
    k = w.shape[0]
    p = k // 2
    L = x.shape[1]
    xp = jnp.pad(x, ((0, 0), (p, p), (0, 0)))
    y = xp[:, 0:L] * w[0]
    for j in range(1, k):
        y = y + xp[:, j:j + L] * w[j]
    return y


def _chunk(t):
    b, L, h = t.shape[:3]
    t = t.reshape((b, L // CHUNK, CHUNK, h) + t.shape[3:])
    return jnp.moveaxis(t, 3, 1)


def _unchunk(t):
    b, h, n, c = t.shape[:4]
    return jnp.moveaxis(t, 1, 3).reshape((b, n * c, h) + t.shape[4:])


def gdn_chunked(q, k, v, beta, loga, s0):
    dk = q.shape[-1]
    qc, kc, vc = _chunk(q), _chunk(k), _chunk(v)
    bc, g = _chunk(beta), jnp.cumsum(_chunk(loga), -1)
    incl = jnp.tril(jnp.ones((CHUNK, CHUNK), bool))
    strict = jnp.tril(jnp.ones((CHUNK, CHUNK), bool), k=-1)
    dec_incl = jnp.exp(jnp.where(incl, g[..., :, None] - g[..., None, :], -jnp.inf))
    dec_strict = jnp.where(strict, dec_incl, 0.0)
    lmat = bc[..., :, None] * jnp.einsum('bhncd,bhnsd->bhncs', kc, kc) * dec_strict
    rhs = jnp.concatenate([(bc * jnp.exp(g))[..., None] * kc, bc[..., None] * vc], -1)
    sol = lax.linalg.triangular_solve(jnp.eye(CHUNK, dtype=F32) + lmat, rhs,
                                      left_side=True, lower=True, unit_diagonal=True)
    w, u0 = sol[..., :dk], sol[..., dk:]
    pmat = jnp.einsum('bhncd,bhnsd->bhncs', qc, kc) * dec_incl
    glast = g[..., -1]
    kd = jnp.exp(glast[..., None] - g)[..., None] * kc

    def step(s, xs):
        w_, u_, kd_, gl = xs
        uc = u_ - jnp.einsum('bhck,bhkv->bhcv', w_, s)
        s_new = jnp.exp(gl)[..., None, None] * s + jnp.einsum('bhck,bhcv->bhkv', kd_, uc)
        return s_new, (s, uc)

    xs = tuple(jnp.moveaxis(t, 2, 0) for t in (w, u0, kd, glast))
    s_fin, (s_in, uc) = lax.scan(step, s0, xs)
    s_in, uc = jnp.moveaxis(s_in, 0, 2), jnp.moveaxis(uc, 0, 2)
    o = (jnp.einsum('bhncd,bhndv->bhncv', jnp.exp(g)[..., None] * qc, s_in)
         + jnp.einsum('bhncs,bhnsv->bhncv', pmat, uc))
    return _unchunk(o), s_fin


def mlstm_chunked(q, k, v, ig, lf, c0, n0, m0):
    qc, kc, vc = _chunk(q), _chunk(k), _chunk(v)
    igc, bcum = _chunk(ig), jnp.cumsum(_chunk(lf), -1)
    incl = jnp.tril(jnp.ones((CHUNK, CHUNK), bool))
    dlog = jnp.where(incl, bcum[..., :, None] - bcum[..., None, :] + igc[..., None, :], -jnp.inf)
    dmax = jnp.max(dlog, -1)
    pw = jnp.exp(dlog - dmax[..., None]) * jnp.einsum('bhncd,bhnsd->bhncs', qc, kc)
    intra_num = jnp.einsum('bhncs,bhnsv->bhncv', pw, vc)
    intra_den = jnp.sum(pw, -1)
    blast = bcum[..., -1]
    wlog = blast[..., None] - bcum + igc
    wmax = jnp.max(wlog, -1)
    ewk = jnp.exp(wlog - wmax[..., None])[..., None] * kc
    dc = jnp.einsum('bhnck,bhncv->bhnkv', ewk, vc)
    dn = jnp.sum(ewk, -2)

    def step(carry, xs):
        cm, nm, mm = carry
        dc_, dn_, bl, wm = xs
        m_new = jnp.maximum(bl + mm, wm)
        a = jnp.exp(bl + mm - m_new)
        e = jnp.exp(wm - m_new)
        c_new = a[..., None, None] * cm + e[..., None, None] * dc_
        n_new = a[..., None] * nm + e[..., None] * dn_
        return (c_new, n_new, m_new), (cm, nm, mm)

    xs = tuple(jnp.moveaxis(t, 2, 0) for t in (dc, dn, blast, wmax))
    (cf, nf, mf), (cs, ns, ms) = lax.scan(step, (c0, n0, m0), xs)
    cs, ns, ms = jnp.moveaxis(cs, 0, 2), jnp.moveaxis(ns, 0, 2), jnp.moveaxis(ms, 0, 2)
    alog = bcum + ms[..., None]
    mt = jnp.maximum(alog, dmax)
    wi, wa = jnp.exp(alog - mt), jnp.exp(dmax - mt)
    num = wi[..., None] * jnp.einsum('bhncd,bhndv->bhncv', qc, cs) + wa[..., None] * intra_num
    den = wi * jnp.einsum('bhncd,bhnd->bhnc', qc, ns) + wa * intra_den
    h = num / jnp.maximum(jnp.abs(den), jnp.exp(-mt))[..., None]
    return _unchunk(h), (cf, nf, mf)


def hyena_filters(L, w1, b1, w2, b2, w3, b3, freq, rate):
    pos = jnp.arange(L, dtype=F32)
    t = pos / L
    ang = (2.0 * math.pi) * t[:, None] * jnp.arange(1, HY_BANDS + 1, dtype=F32)
    feats = jnp.concatenate([t[:, None], jnp.sin(ang), jnp.cos(ang)], -1)
    hp = lax.Precision.HIGHEST
    z = jnp.sin(freq[0] * (jnp.dot(feats, w1, precision=hp) + b1))
    z = jnp.sin(freq[1] * (jnp.dot(z, w2, precision=hp) + b2))
    filt = jnp.dot(z, w3, precision=hp) + b3
    lag = jnp.abs(pos - L // 2) / L
    filt = filt * jnp.exp(-lag[:, None] * rate)
    return filt.reshape(L, HY_ORDER, HY_CH)


def long_conv(u, filt):
    L = u.shape[1]
    uf = jnp.fft.rfft(u, n=2 * L, axis=1)
    ff = jnp.fft.rfft(filt, n=2 * L, axis=0)
    y = jnp.fft.irfft(uf * ff[None], n=2 * L, axis=1)
    return y[:, L // 2: L // 2 + L]


def grid_pos_embed(n_tokens):
    rows = n_tokens // GRID_W
    quarter = D_MODEL // 4
    omega = 1.0 / (10000.0 ** (jnp.arange(quarter, dtype=F32) / quarter))
    r = jnp.broadcast_to(jnp.arange(rows, dtype=F32)[:, None, None] * omega, (rows, GRID_W, quarter))
    cl = jnp.broadcast_to(jnp.arange(GRID_W, dtype=F32)[None, :, None] * omega, (rows, GRID_W, quarter))
    pe = jnp.concatenate([jnp.sin(r), jnp.cos(r), jnp.sin(cl), jnp.cos(cl)], -1)
    return pe.reshape(rows * GRID_W, D_MODEL)


def seq_mixers(parts, lp, s_gdn, s_c, s_n, s_m):
    (qa, ka, va, za, beta_p, alpha_p, qb, kb, vb, ob, ig_p, fg_p, hy, g_a, g_b, g_c) = parts
    b, L, _ = qa.shape

    qkv = jax.nn.silu(centred_conv(jnp.concatenate([qa, ka, va], -1), lp['gdn_conv']))
    q = l2norm(qkv[..., :QK_A].reshape(b, L, H_A, DK_A)) * (DK_A ** -0.5)
    k = l2norm(qkv[..., QK_A:2 * QK_A].reshape(b, L, H_A, DK_A))
    v = qkv[..., 2 * QK_A:].reshape(b, L, H_A, DV_A)
    beta = jax.nn.sigmoid(beta_p).reshape(b, L, 2, H_A)
    loga = -jnp.exp(lp['gdn_a_log']) * jax.nn.softplus(alpha_p.reshape(b, L, 2, H_A) + lp['gdn_dt_bias'])
    o_f, sf = gdn_chunked(q, k, v, beta[:, :, 0], loga[:, :, 0], s_gdn[:, 0])
    o_b, sb = gdn_chunked(_rev(q), _rev(k), _rev(v), _rev(beta[:, :, 1]), _rev(loga[:, :, 1]), s_gdn[:, 1])
    o = rms_norm(o_f + _rev(o_b), lp['gdn_norm']) * jax.nn.silu(za.reshape(b, L, H_A, DV_A))
    y_a = o.reshape(b, L, V_A)

    q = qb.reshape(b, L, H_B, DQK_B)
    k = kb.reshape(b, L, H_B, DQK_B) * (DQK_B ** -0.5)
    v = vb.reshape(b, L, H_B, DV_B)
    ig = ig_p.reshape(b, L, 2, H_B) + lp['ml_i_bias']
    lf = jax.nn.log_sigmoid(fg_p.reshape(b, L, 2, H_B) + lp['ml_f_bias'])
    h_f, (cf, nf, mf) = mlstm_chunked(q, k, v, ig[:, :, 0], lf[:, :, 0], s_c[:, 0], s_n[:, 0], s_m[:, 0])
    h_b, (cb, nb, mb) = mlstm_chunked(_rev(q), _rev(k), _rev(v), _rev(ig[:, :, 1]), _rev(lf[:, :, 1]),
                                      s_c[:, 1], s_n[:, 1], s_m[:, 1])
    y_b = (rms_norm(h_f + _rev(h_b), lp['ml_norm'])
           * jax.nn.sigmoid(ob.reshape(b, L, H_B, DV_B))).reshape(b, L, V_B)

    u = jnp.split(centred_conv(hy, lp['hy_conv']), HY_ORDER + 1, axis=-1)
    filt = hyena_filters(L, lp['hy_w1'], lp['hy_b1'], lp['hy_w2'], lp['hy_b2'], lp['hy_w3'], lp['hy_b3'],
                         lp['hy_freq'], lp['hy_rate'])
    z = u[0]
    for order in range(HY_ORDER):
        z = u[order + 1] * (long_conv(z, filt[:, order]) + lp['hy_skip'][order] * z)
    y_c = z
    states = (jnp.stack([sf, sb], 1), jnp.stack([cf, cb], 1), jnp.stack([nf, nb], 1), jnp.stack([mf, mb], 1))
    return (y_a, y_b, y_c, g_a, g_b, g_c), states


def expert_choice_ffn(hf, w_router, b_router, w_gate, w_up, w_down):
    n, d = hf.shape
    cap = CAP_FACTOR * n // N_EXPERTS
    aff = jax.nn.softmax(jnp.dot(hf, w_router, precision=lax.Precision.HIGHEST) + b_router, axis=-1)
    g, idx = lax.top_k(aff.T, cap)
    xe = hf.astype(BF16)[idx]
    hid = jax.nn.silu(pbmm(xe, w_gate)) * pbmm(xe, w_up)
    ye = pbmm(hid, w_down) * g[..., None]
    return jnp.zeros_like(hf).at[idx.reshape(-1)].add(ye.reshape(-1, d))


_IN_SIZES = (QK_A, QK_A, V_A, V_A, 2 * H_A, 2 * H_A, QK_B, QK_B, V_B, V_B, 2 * H_B, 2 * H_B,
             3 * HY_CH, D_MODEL, D_MODEL, D_MODEL)
_IN_OFFS = tuple(int(v) for v in np.concatenate([[0], np.cumsum(_IN_SIZES)]))
_SMALL_PARTS = (4, 5, 10, 11)


def trunk_layer(x_sets, mods, lp, states):
    hs = []
    for x, mod in zip(x_sets, mods):
        shift1, scale1 = mod[..., :D_MODEL], mod[..., D_MODEL:2 * D_MODEL]
        hs.append((_ln_stats(x) * (1.0 + scale1) + shift1).reshape(-1, D_MODEL))
    h_all = jnp.concatenate(hs, 0)

    big_cols = [p for p in range(len(_IN_SIZES)) if p not in _SMALL_PARTS]
    w_in = lp['w_in']
    w_big = jnp.concatenate([w_in[:, _IN_OFFS[p]:_IN_OFFS[p + 1]] for p in big_cols], -1)
    w_small = jnp.concatenate([w_in[:, _IN_OFFS[p]:_IN_OFFS[p + 1]] for p in _SMALL_PARTS], -1)
    w_small = jnp.pad(w_small, ((0, 0), (0, 128 - w_small.shape[1])))
    proj_big = pmm(h_all, w_big)
    proj_small = pmm(h_all, w_small)

    big_offs = np.concatenate([[0], np.cumsum([_IN_SIZES[p] for p in big_cols])])
    small_offs = np.concatenate([[0], np.cumsum([_IN_SIZES[p] for p in _SMALL_PARTS])])

    mixed, new_states, row = [], [], 0
    for x, st in zip(x_sets, states):
        b, L, _ = x.shape
        rows = slice(row, row + b * L)
        row += b * L
        parts = [None] * len(_IN_SIZES)
        for i, p in enumerate(big_cols):
            parts[p] = proj_big[rows, int(big_offs[i]):int(big_offs[i + 1])].reshape(b, L, -1)
        for i, p in enumerate(_SMALL_PARTS):
            parts[p] = proj_small[rows, int(small_offs[i]):int(small_offs[i + 1])].reshape(b, L, -1)
        outs, st_new = seq_mixers(parts, lp, *st)
        mixed.append(tuple(t.reshape(b * L, -1) for t in outs))
        new_states.append(st_new)

    y_a, y_b, y_c, g_a, g_b, g_c = (jnp.concatenate(ts, 0) for ts in zip(*mixed))
    merged = (jax.nn.sigmoid(g_a) * pmm(y_a, lp['w_br_a'])
              + jax.nn.sigmoid(g_b) * pmm(y_b, lp['w_br_b'])
              + jax.nn.sigmoid(g_c) * pmm(y_c, lp['w_br_c']))
    y_all = pmm(merged, lp['w_out'])

    out_sets, row = [], 0
    for x, mod in zip(x_sets, mods):
        b, L, _ = x.shape
        gate1 = mod[..., 2 * D_MODEL:3 * D_MODEL]
        shift2, scale2, gate2 = (mod[..., (3 + i) * D_MODEL:(4 + i) * D_MODEL] for i in range(3))
        y = y_all[row:row + b * L].reshape(b, L, D_MODEL)
        row += b * L
        x1 = _ln_stats(DN_ALPHA * x + (1.0 + gate1) * y) * lp['ln1_g'] + lp['ln1_b']
        h2 = _ln_stats(x1) * (1.0 + scale2) + shift2
        y2 = expert_choice_ffn(h2.reshape(b * L, D_MODEL), lp['w_router'], lp['b_router'],
                               lp['w_gate'], lp['w_up'], lp['w_down']).reshape(b, L, D_MODEL)
        out_sets.append(_ln_stats(DN_ALPHA * x1 + (1.0 + gate2) * y2) * lp['ln2_g'] + lp['ln2_b'])
    return out_sets, new_states


def kernel(x_prompt, x_sample, state_gdn, state_mlstm_c, state_mlstm_n, state_mlstm_m, c, c_ctx, w_ada, b_ada, w_in, gdn_conv, gdn_a_log, gdn_dt_bias, gdn_norm, ml_i_bias, ml_f_bias, ml_norm, hy_conv, hy_w1, hy_b1, hy_w2, hy_b2, hy_w3, hy_b3, hy_freq, hy_rate, hy_skip, w_br_a, w_br_b, w_br_c, w_out, ln1_g, ln1_b, ln2_g, ln2_b, w_router, b_router, w_gate, w_up, w_down):
    bp = x_prompt.shape[0]
    ls = x_sample.shape[1]
    xp = x_prompt
    xs = x_sample + grid_pos_embed(ls)[None]
    zero_states = (jnp.zeros((bp, 2, H_A, DK_A, DV_A), F32), jnp.zeros((bp, 2, H_B, DQK_B, DV_B), F32),
                   jnp.zeros((bp, 2, H_B, DQK_B), F32), jnp.zeros((bp, 2, H_B), F32))
    cond = jax.nn.silu(jnp.concatenate([c_ctx[None], c], 0))
    cond = jnp.pad(cond, ((0, 16 - cond.shape[0]), (0, 0)))
    ctx_states = []
    stacked = dict(w_in=w_in, gdn_conv=gdn_conv, gdn_a_log=gdn_a_log, gdn_dt_bias=gdn_dt_bias, gdn_norm=gdn_norm,
                   ml_i_bias=ml_i_bias, ml_f_bias=ml_f_bias, ml_norm=ml_norm, hy_conv=hy_conv, hy_w1=hy_w1,
                   hy_b1=hy_b1, hy_w2=hy_w2, hy_b2=hy_b2, hy_w3=hy_w3, hy_b3=hy_b3, hy_freq=hy_freq,
                   hy_rate=hy_rate, hy_skip=hy_skip, w_br_a=w_br_a, w_br_b=w_br_b, w_br_c=w_br_c, w_out=w_out,
                   ln1_g=ln1_g, ln1_b=ln1_b, ln2_g=ln2_g, ln2_b=ln2_b, w_router=w_router, b_router=b_router,
                   w_gate=w_gate, w_up=w_up, w_down=w_down)
    for l in range(DEPTH):
        lp = {name: t[l] for name, t in stacked.items()}
        mod = pmm(cond, w_ada[l]) + b_ada[l]
        mod_ctx = mod[0].reshape(1, 1, 6 * D_MODEL)
        mod_lat = mod[1:1 + c.shape[0]][:, None, :]
        lat_states = (state_gdn[:, l], state_mlstm_c[:, l], state_mlstm_n[:, l], state_mlstm_m[:, l])
        (xp, xs), (st_ctx, _) = trunk_layer([xp, xs], [mod_ctx, mod_lat], lp, [zero_states, lat_states])
        ctx_states.append(st_ctx)
    outs = tuple(jnp.stack([st[i] for st in ctx_states], 1) for i in range(4))
    return (xp, xs) + outs
```

```python
import functools
import math

import jax
import jax.numpy as jnp
import numpy as np
from jax import lax
from jax.experimental import pallas as pl
from jax.experimental.pallas import tpu as pltpu

D_MODEL = 2048
DEPTH = 4
GRID_W = 64
CHUNK = 64
H_A = 8
DK_A = 128
DV_A = 128
QK_A = H_A * DK_A
V_A = H_A * DV_A
H_B = 4
DQK_B = 128
DV_B = 256
QK_B = H_B * DQK_B
V_B = H_B * DV_B
HY_CH = 1024
HY_ORDER = 2
HY_BANDS = 16
N_EXPERTS = 16
D_EXPERT = 1024
CAP_FACTOR = 2
DN_ALPHA = (2 * DEPTH) ** 0.25
LN_EPS = 1e-6
F32 = jnp.float32
BF16 = jnp.bfloat16

LANES = 128
VMEM_LIMIT_BYTES = 48 * 1024 * 1024

COL_QA, COL_KA, COL_VA, COL_ZA = 0, QK_A, 2 * QK_A, 2 * QK_A + V_A
COL_QB = 2 * QK_A + 2 * V_A
COL_KB = COL_QB + QK_B
COL_VB = COL_KB + QK_B
COL_OB = COL_VB + V_B
COL_HY = COL_OB + V_B
COL_GA = COL_HY + 3 * HY_CH
COL_GB = COL_GA + D_MODEL
COL_GC = COL_GB + D_MODEL
N_BIG = COL_GC + D_MODEL


def _mm_kernel(x_ref, w_ref, o_ref):
    o_ref[...] = jnp.dot(x_ref[...], w_ref[...], preferred_element_type=F32).astype(o_ref.dtype)


def _pick(n, pref):
    for t in pref:
        if n % t == 0:
            return t
    return n


def pmm(x, w, out_dtype=F32):
    m, k = x.shape
    _, n = w.shape
    tm = _pick(m, (1024, 512, 256, 128, 16))
    tn = _pick(n, (512, 256, 128))
    return pl.pallas_call(
        _mm_kernel,
        grid=(m // tm, n // tn),
        in_specs=[pl.BlockSpec((tm, k), lambda i, j: (i, 0)),
                  pl.BlockSpec((k, tn), lambda i, j: (0, j))],
        out_specs=pl.BlockSpec((tm, tn), lambda i, j: (i, j)),
        out_shape=jax.ShapeDtypeStruct((m, n), out_dtype),
        compiler_params=pltpu.CompilerParams(
            dimension_semantics=("parallel", "parallel"), vmem_limit_bytes=VMEM_LIMIT_BYTES),
        name="pmm",
    )(x.astype(BF16), w.astype(BF16))


def pbmm(x, w, out_dtype=F32):
    e, m, k = x.shape
    _, _, n = w.shape
    tm = _pick(m, (1024, 512, 256, 128, 8))
    tn = _pick(n, (512, 256, 128))
    return pl.pallas_call(
        _mm_kernel,
        grid=(e, m // tm, n // tn),
        in_specs=[pl.BlockSpec((None, tm, k), lambda b, i, j: (b, i, 0)),
                  pl.BlockSpec((None, k, tn), lambda b, i, j: (b, 0, j))],
        out_specs=pl.BlockSpec((None, tm, tn), lambda b, i, j: (b, i, j)),
        out_shape=jax.ShapeDtypeStruct((e, m, n), out_dtype),
        compiler_params=pltpu.CompilerParams(
            dimension_semantics=("parallel", "parallel", "parallel"), vmem_limit_bytes=VMEM_LIMIT_BYTES),
        name="pbmm",
    )(x.astype(BF16), w.astype(BF16))


def _nn(a, b):
    return jnp.dot(a, b, preferred_element_type=F32)


def _nt(a, b):
    return lax.dot_general(a, b, (((1,), (1,)), ((), ())), preferred_element_type=F32)


def _tn(a, b):
    return lax.dot_general(a, b, (((0,), (0,)), ((), ())), preferred_element_type=F32)


def _split(a):
    hi = a.astype(BF16)
    return hi, (a - hi.astype(F32)).astype(BF16)


def _mm1(f, a, b):
    return f(a.astype(BF16), b.astype(BF16))


def _mm3(f, a, b):
    ah, al = _split(a)
    bh, bl = _split(b)
    return f(ah, bh) + (f(ah, bl) + f(al, bh))


def _chunk_masks():
    ri = lax.broadcasted_iota(jnp.int32, (CHUNK, CHUNK), 0)
    ci = lax.broadcasted_iota(jnp.int32, (CHUNK, CHUNK), 1)
    eye = ri == ci
    incl = (ri >= ci, ri <= ci)
    strict = (ri > ci, ri < ci)
    return eye, incl, strict


def _to_col(eye, row):
    return jnp.sum(jnp.where(eye, row, 0.0), axis=1, keepdims=True)


def _gdn_kernel(q_ref, k_ref, v_ref, z_ref, cw_ref, g_ref, s0_ref, nw_ref, y_ref, sf_ref,
                qs, ks, vs, w_s, u_s, kd_s, qg_s, p_s, o_s, st_s, *, L):
    n = L // CHUNK
    row = lax.broadcasted_iota(jnp.int32, (L, LANES), 0)

    def conv_silu(x_ref, part):
        x = x_ref[...]
        w = cw_ref[part]
        x_prev = jnp.where(row == 0, 0.0, pltpu.roll(x, 1, 0))
        x_next = jnp.where(row == L - 1, 0.0, pltpu.roll(x, L - 1, 0))
        y = x_prev * w[0:1] + x * w[1:2] + x_next * w[2:3]
        return y * jax.nn.sigmoid(y)

    def l2n(x):
        return x * lax.rsqrt(jnp.sum(x * x, axis=-1, keepdims=True) + LN_EPS)

    qs[...] = l2n(conv_silu(q_ref, 0)) * (DK_A ** -0.5)
    ks[...] = l2n(conv_silu(k_ref, 1))
    vs[...] = conv_silu(v_ref, 2)

    eye, incl, strict = _chunk_masks()
    eye_f = jnp.where(eye, 1.0, 0.0)

    def intra(c, carry):
        r0 = pl.multiple_of(c * CHUNK, CHUNK)
        qc = qs[pl.ds(r0, CHUNK), :]
        kc = ks[pl.ds(r0, CHUNK), :]
        vc = vs[pl.ds(r0, CHUNK), :]
        gates = g_ref[c]
        kk = _mm3(_nt, kc, kc)
        qk = _mm1(_nt, qc, kc)
        for d in range(2):
            g_row = gates[2 * d:2 * d + 1, :CHUNK]
            b_row = gates[2 * d + 1:2 * d + 2, :CHUNK]
            gl_row = gates[4 + d:5 + d, :]
            g_col = _to_col(eye, g_row)
            b_col = _to_col(eye, b_row)
            dec = jnp.exp(jnp.where(incl[d], g_col - g_row, -jnp.inf))
            lmat = b_col * kk * jnp.where(strict[d], dec, 0.0)
            pw = lmat
            tinv = eye_f - lmat
            for _ in range(int(math.log2(CHUNK)) - 1):
                pw = _mm3(_nn, pw, pw)
                tinv = tinv + _mm3(_nn, tinv, pw)
            th, tl = _split(tinv)

            def apply_t(rhs):
                rh, rl = _split(rhs)
                return _nn(th, rh) + (_nn(th, rl) + _nn(tl, rh))

            w_s[d, pl.ds(r0, CHUNK), :] = apply_t((b_col * jnp.exp(g_col)) * kc)
            u_s[d, pl.ds(r0, CHUNK), :] = apply_t(b_col * vc)
            kd_s[d, pl.ds(r0, CHUNK), :] = jnp.exp(gl_row - g_col) * kc
            qg_s[d, pl.ds(r0, CHUNK), :] = jnp.exp(g_col) * qc
            p_s[d, c] = qk * dec
        return carry

    lax.fori_loop(0, n, intra, 0)

    st_s[...] = s0_ref[...]

    def scan(i, carry):
        for d in range(2):
            c = i if d == 0 else n - 1 - i
            r0 = pl.multiple_of(c * CHUNK, CHUNK)
            s = st_s[d]
            gl_row = g_ref[c][4 + d:5 + d, :]
            uc = u_s[d, pl.ds(r0, CHUNK), :] - _mm1(_nn, w_s[d, pl.ds(r0, CHUNK), :], s)
            o_s[d, pl.ds(r0, CHUNK), :] = _mm1(_nn, qg_s[d, pl.ds(r0, CHUNK), :], s) + _mm1(_nn, p_s[d, c], uc)
            st_s[d] = jnp.exp(gl_row) * s + _mm1(_tn, kd_s[d, pl.ds(r0, CHUNK), :], uc)
        return carry

    lax.fori_loop(0, n, scan, 0)

    o = o_s[0] + o_s[1]
    z = z_ref[...]
    o = o * lax.rsqrt(jnp.mean(o * o, axis=-1, keepdims=True) + LN_EPS) * nw_ref[...]
    y_ref[...] = o * (z * jax.nn.sigmoid(z))
    sf_ref[...] = st_s[...]


def gdn_branch(proj, row_blk0, nb, L, gates, conv_w, s0, norm_w):
    n = L // CHUNK
    seq = lambda col: pl.BlockSpec((L, LANES), lambda b, h, col=col: (row_blk0 + b, col // LANES + h))
    scr = lambda *s: pltpu.VMEM(s, F32)
    return pl.pallas_call(
        functools.partial(_gdn_kernel, L=L),
        grid=(nb, H_A),
        in_specs=[seq(COL_QA), seq(COL_KA), seq(COL_VA), seq(COL_ZA),
                  pl.BlockSpec((3, None, 3, LANES), lambda b, h: (0, h, 0, 0)),
                  pl.BlockSpec((None, None, n, 8, LANES), lambda b, h: (b, h, 0, 0, 0)),
                  pl.BlockSpec((None, 2, None, DK_A, DV_A), lambda b, h: (b, 0, h, 0, 0)),
                  pl.BlockSpec((1, LANES), lambda b, h: (0, 0))],
        out_specs=[pl.BlockSpec((L, LANES), lambda b, h: (b, h)),
                   pl.BlockSpec((None, 2, None, DK_A, DV_A), lambda b, h: (b, 0, h, 0, 0))],
        out_shape=[jax.ShapeDtypeStruct((nb * L, V_A), F32),
                   jax.ShapeDtypeStruct((nb, 2, H_A, DK_A, DV_A), F32)],
        scratch_shapes=[scr(L, LANES), scr(L, LANES), scr(L, LANES),
                        scr(2, L, LANES), scr(2, L, LANES), scr(2, L, LANES), scr(2, L, LANES),
                        scr(2, n, CHUNK, CHUNK), scr(2, L, LANES), scr(2, DK_A, DV_A)],
        compiler_params=pltpu.CompilerParams(
            dimension_semantics=("parallel", "parallel"), vmem_limit_bytes=VMEM_LIMIT_BYTES),
        name="gdn_branch",
    )(proj, proj, proj, proj, conv_w, gates, s0, norm_w)


def _mlstm_kernel(q_ref, k_ref, v_ref, ob_ref, g_ref, c0_ref, n0_ref, m0_ref, nw_ref,
                  y_ref, cf_ref, nf_ref, mf_ref, h_s, c_s, n_s, m_s, *, L):
    n = L // CHUNK
    eye, incl, _ = _chunk_masks()
    c_s[...] = c0_ref[...]
    n_s[...] = n0_ref[...]
    m_s[...] = m0_ref[...]

    def body(i, carry):
        for d in range(2):
            c = i if d == 0 else n - 1 - i
            r0 = pl.multiple_of(c * CHUNK, CHUNK)
            qc = q_ref[pl.ds(r0, CHUNK), :]
            kc = k_ref[pl.ds(r0, CHUNK), :] * (DQK_B ** -0.5)
            vc = v_ref[pl.ds(r0, CHUNK), :]
            gates = g_ref[c]
            b_row = gates[2 * d:2 * d + 1, :CHUNK]
            i_row = gates[2 * d + 1:2 * d + 2, :CHUNK]
            bl_row = gates[4 + d:5 + d, :]
            wm_row = gates[6 + d:7 + d, :]
            b_col = _to_col(eye, b_row)
            i_col = _to_col(eye, i_row)
            dlog = jnp.where(incl[d], b_col - b_row + i_row, -jnp.inf)
            dmax = jnp.max(dlog, axis=1, keepdims=True)
            pw = jnp.exp(dlog - dmax) * _mm1(_nt, qc, kc)
            intra_num = _mm1(_nn, pw, vc)
            intra_den = jnp.sum(pw, axis=1, keepdims=True)
            ewk = jnp.exp(bl_row[:, :1] - b_col + i_col - wm_row[:, :1]) * kc
            dc = _mm1(_tn, ewk, vc)
            dn = jnp.sum(ewk, axis=0, keepdims=True)

            cm, nm, mm = c_s[d], n_s[d], m_s[d]
            alog = b_col + mm
            mt = jnp.maximum(alog, dmax)
            wi = jnp.exp(alog - mt)[:, :1]
            wa = jnp.exp(dmax - mt)[:, :1]
            num = wi * _mm1(_nn, qc, cm) + wa * intra_num
            den = wi * jnp.sum(qc * nm, axis=1, keepdims=True) + wa * intra_den
            h_s[d, pl.ds(r0, CHUNK), :] = num / jnp.maximum(jnp.abs(den), jnp.exp(-mt[:, :1]))

            m_new = jnp.maximum(bl_row + mm, wm_row)
            a = jnp.exp(bl_row + mm - m_new)
            e = jnp.exp(wm_row - m_new)
            c_s[d] = a[:, :1] * cm + e[:, :1] * dc
            n_s[d] = a * nm + e * dn
            m_s[d] = m_new
        return carry

    lax.fori_loop(0, n, body, 0)

    h = h_s[0] + h_s[1]
    h = h * lax.rsqrt(jnp.mean(h * h, axis=-1, keepdims=True) + LN_EPS) * nw_ref[...]
    y_ref[...] = h * jax.nn.sigmoid(ob_ref[...])
    cf_ref[...] = c_s[...]
    nf_ref[...] = n_s[...]
    mf_ref[...] = m_s[...]


def mlstm_branch(proj, row_blk0, nb, L, gates, c0, n0, m0, norm_w):
    n = L // CHUNK
    seq = lambda col, w: pl.BlockSpec((L, w), lambda b, h, col=col, w=w: (row_blk0 + b, col // w + h))
    st = lambda *s: pl.BlockSpec((None, 2, None) + s, lambda b, h: (b, 0, h, 0, 0))
    scr = lambda *s: pltpu.VMEM(s, F32)
    return pl.pallas_call(
        functools.partial(_mlstm_kernel, L=L),
        grid=(nb, H_B),
        in_specs=[seq(COL_QB, DQK_B), seq(COL_KB, DQK_B), seq(COL_VB, DV_B), seq(COL_OB, DV_B),
                  pl.BlockSpec((None, None, n, 8, LANES), lambda b, h: (b, h, 0, 0, 0)),
                  st(DQK_B, DV_B), st(1, DQK_B), st(1, LANES),
                  pl.BlockSpec((1, DV_B), lambda b, h: (0, 0))],
        out_specs=[pl.BlockSpec((L, DV_B), lambda b, h: (b, h)),
                   st(DQK_B, DV_B), st(1, DQK_B), st(1, LANES)],
        out_shape=[jax.ShapeDtypeStruct((nb * L, V_B), F32),
                   jax.ShapeDtypeStruct((nb, 2, H_B, DQK_B, DV_B), F32),
                   jax.ShapeDtypeStruct((nb, 2, H_B, 1, DQK_B), F32),
                   jax.ShapeDtypeStruct((nb, 2, H_B, 1, LANES), F32)],
        scratch_shapes=[scr(2, L, DV_B), scr(2, DQK_B, DV_B), scr(2, 1, DQK_B), scr(2, 1, LANES)],
        compiler_params=pltpu.CompilerParams(
            dimension_semantics=("parallel", "parallel"), vmem_limit_bytes=VMEM_LIMIT_BYTES),
        name="mlstm_branch",
    )(proj, proj, proj, proj, gates, c0, n0, m0, norm_w)


def _ln_stats(x):
    mu = jnp.mean(x, -1, keepdims=True)
    var = jnp.mean(jnp.square(x - mu), -1, keepdims=True)
    return (x - mu) * lax.rsqrt(var + LN_EPS)


def centred_conv(x, w):
    k = w.shape[0]
    p = k // 2
    L = x.shape[1]
    xp = jnp.pad(x, ((0, 0), (p, p), (0, 0)))
    y = xp[:, 0:L] * w[0]
    for j in range(1, k):
        y = y + xp[:, j:j + L] * w[j]
    return y


def hyena_filters(L, w1, b1, w2, b2, w3, b3, freq, rate):
    pos = jnp.arange(L, dtype=F32)
    t = pos / L
    ang = (2.0 * math.pi) * t[:, None] * jnp.arange(1, HY_BANDS + 1, dtype=F32)
    feats = jnp.concatenate([t[:, None], jnp.sin(ang), jnp.cos(ang)], -1)
    hp = lax.Precision.HIGHEST
    z = jnp.sin(freq[0] * (jnp.dot(feats, w1, precision=hp) + b1))
    z = jnp.sin(freq[1] * (jnp.dot(z, w2, precision=hp) + b2))
    filt = jnp.dot(z, w3, precision=hp) + b3
    lag = jnp.abs(pos - L // 2) / L
    filt = filt * jnp.exp(-lag[:, None] * rate)
    return filt.reshape(L, HY_ORDER, HY_CH)


def long_conv(u, filt):
    L = u.shape[1]
    uf = jnp.fft.rfft(u, n=2 * L, axis=1)
    ff = jnp.fft.rfft(filt, n=2 * L, axis=0)
    y = jnp.fft.irfft(uf * ff[None], n=2 * L, axis=1)
    return y[:, L // 2: L // 2 + L]


def grid_pos_embed(n_tokens):
    rows = n_tokens // GRID_W
    quarter = D_MODEL // 4
    omega = 1.0 / (10000.0 ** (jnp.arange(quarter, dtype=F32) / quarter))
    r = jnp.broadcast_to(jnp.arange(rows, dtype=F32)[:, None, None] * omega, (rows, GRID_W, quarter))
    cl = jnp.broadcast_to(jnp.arange(GRID_W, dtype=F32)[None, :, None] * omega, (rows, GRID_W, quarter))
    pe = jnp.concatenate([jnp.sin(r), jnp.cos(r), jnp.sin(cl), jnp.cos(cl)], -1)
    return pe.reshape(rows * GRID_W, D_MODEL)


def _chunk_rows(t):
    b, L, h = t.shape
    return t.reshape(b, L // CHUNK, CHUNK, h).transpose(0, 3, 1, 2)


def _suffix_cumsum(t):
    return jnp.flip(jnp.cumsum(jnp.flip(t, -1), -1), -1)


def _pack_gate_rows(per_token, per_chunk):
    rows = [jnp.pad(t, ((0, 0), (0, 0), (0, 0), (0, LANES - CHUNK))) for t in per_token]
    rows += [jnp.broadcast_to(t[..., None], t.shape + (LANES,)) for t in per_chunk]
    return jnp.stack(rows, axis=3)


def gdn_gate_rows(beta_p, alpha_p, a_log, dt_bias):
    b, L, _ = beta_p.shape
    beta = jax.nn.sigmoid(beta_p).reshape(b, L, 2, H_A)
    loga = -jnp.exp(a_log) * jax.nn.softplus(alpha_p.reshape(b, L, 2, H_A) + dt_bias)
    g_f = jnp.cumsum(_chunk_rows(loga[:, :, 0]), -1)
    g_b = _suffix_cumsum(_chunk_rows(loga[:, :, 1]))
    zero = jnp.zeros_like(g_f[..., 0])
    return _pack_gate_rows([g_f, _chunk_rows(beta[:, :, 0]), g_b, _chunk_rows(beta[:, :, 1])],
                           [g_f[..., -1], g_b[..., 0], zero, zero])


def mlstm_gate_rows(ig_p, fg_p, i_bias, f_bias):
    b, L, _ = ig_p.shape
    ig = ig_p.reshape(b, L, 2, H_B) + i_bias
    lf = jax.nn.log_sigmoid(fg_p.reshape(b, L, 2, H_B) + f_bias)
    ig_f, ig_b = _chunk_rows(ig[:, :, 0]), _chunk_rows(ig[:, :, 1])
    bc_f = jnp.cumsum(_chunk_rows(lf[:, :, 0]), -1)
    bc_b = _suffix_cumsum(_chunk_rows(lf[:, :, 1]))
    bl_f, bl_b = bc_f[..., -1], bc_b[..., 0]
    wm_f = jnp.max(bl_f[..., None] - bc_f + ig_f, -1)
    wm_b = jnp.max(bl_b[..., None] - bc_b + ig_b, -1)
    return _pack_gate_rows([bc_f, ig_f, bc_b, ig_b], [bl_f, bl_b, wm_f, wm_b])


def seq_mixers(proj_big, proj_small, row0, nb, L, lp, s_gdn, s_c, s_n, s_m):
    rows = slice(row0, row0 + nb * L)
    small = proj_small[rows].reshape(nb, L, LANES)
    blk0 = row0 // L

    conv_w = lp['gdn_conv'].reshape(3, 3, H_A, DK_A).transpose(1, 2, 0, 3)
    gates = gdn_gate_rows(small[..., 0:2 * H_A], small[..., 2 * H_A:4 * H_A], lp['gdn_a_log'], lp['gdn_dt_bias'])
    y_a, s_gdn_new = gdn_branch(proj_big, blk0, nb, L, gates, conv_w, s_gdn, lp['gdn_norm'].reshape(1, DV_A))

    o = 4 * H_A
    gates = mlstm_gate_rows(small[..., o:o + 2 * H_B], small[..., o + 2 * H_B:o + 4 * H_B],
                            lp['ml_i_bias'], lp['ml_f_bias'])
    m0 = jnp.broadcast_to(s_m[..., None, None], s_m.shape + (1, LANES))
    y_b, c_new, n_new, m_new = mlstm_branch(proj_big, blk0, nb, L, gates, s_c, s_n[..., None, :], m0,
                                            lp['ml_norm'].reshape(1, DV_B))

    hy = proj_big[rows, COL_HY:COL_GA].reshape(nb, L, 3 * HY_CH)
    u = jnp.split(centred_conv(hy, lp['hy_conv']), HY_ORDER + 1, axis=-1)
    filt = hyena_filters(L, lp['hy_w1'], lp['hy_b1'], lp['hy_w2'], lp['hy_b2'], lp['hy_w3'], lp['hy_b3'],
                         lp['hy_freq'], lp['hy_rate'])
    z = u[0]
    for order in range(HY_ORDER):
        z = u[order + 1] * (long_conv(z, filt[:, order]) + lp['hy_skip'][order] * z)
    y_c = z.reshape(nb * L, HY_CH)
    return (y_a, y_b, y_c), (s_gdn_new, c_new, n_new[..., 0, :], m_new[..., 0, 0])


def expert_choice_ffn(hf, w_router, b_router, w_gate, w_up, w_down):
    n, d = hf.shape
    cap = CAP_FACTOR * n // N_EXPERTS
    aff = jax.nn.softmax(jnp.dot(hf, w_router, precision=lax.Precision.HIGHEST) + b_router, axis=-1)
    g, idx = lax.top_k(aff.T, cap)
    xe = hf.astype(BF16)[idx]
    hid = jax.nn.silu(pbmm(xe, w_gate)) * pbmm(xe, w_up)
    ye = pbmm(hid, w_down) * g[..., None]
    return jnp.zeros_like(hf).at[idx.reshape(-1)].add(ye.reshape(-1, d))


_IN_SIZES = (QK_A, QK_A, V_A, V_A, 2 * H_A, 2 * H_A, QK_B, QK_B, V_B, V_B, 2 * H_B, 2 * H_B,
             3 * HY_CH, D_MODEL, D_MODEL, D_MODEL)
_IN_OFFS = tuple(int(v) for v in np.concatenate([[0], np.cumsum(_IN_SIZES)]))
_SMALL_PARTS = (4, 5, 10, 11)


def trunk_layer(x_sets, mods, lp, states):
    hs = []
    for x, mod in zip(x_sets, mods):
        shift1, scale1 = mod[..., :D_MODEL], mod[..., D_MODEL:2 * D_MODEL]
        hs.append((_ln_stats(x) * (1.0 + scale1) + shift1).reshape(-1, D_MODEL))
    h_all = jnp.concatenate(hs, 0)

    big_cols = [p for p in range(len(_IN_SIZES)) if p not in _SMALL_PARTS]
    w_in = lp['w_in']
    w_big = jnp.concatenate([w_in[:, _IN_OFFS[p]:_IN_OFFS[p + 1]] for p in big_cols], -1)
    w_small = jnp.concatenate([w_in[:, _IN_OFFS[p]:_IN_OFFS[p + 1]] for p in _SMALL_PARTS], -1)
    w_small = jnp.pad(w_small, ((0, 0), (0, LANES - w_small.shape[1])))
    proj_big = pmm(h_all, w_big)
    proj_small = pmm(h_all, w_small)

    mixed, new_states, row = [], [], 0
    for x, st in zip(x_sets, states):
        b, L, _ = x.shape
        outs, st_new = seq_mixers(proj_big, proj_small, row, b, L, lp, *st)
        row += b * L
        mixed.append(outs)
        new_states.append(st_new)

    y_a, y_b, y_c = (jnp.concatenate(ts, 0) for ts in zip(*mixed))
    merged = (jax.nn.sigmoid(proj_big[:, COL_GA:COL_GB]) * pmm(y_a, lp['w_br_a'])
              + jax.nn.sigmoid(proj_big[:, COL_GB:COL_GC]) * pmm(y_b, lp['w_br_b'])
              + jax.nn.sigmoid(proj_big[:, COL_GC:N_BIG]) * pmm(y_c, lp['w_br_c']))
    y_all = pmm(merged, lp['w_out'])

    out_sets, row = [], 0
    for x, mod in zip(x_sets, mods):
        b, L, _ = x.shape
        gate1 = mod[..., 2 * D_MODEL:3 * D_MODEL]
        shift2, scale2, gate2 = (mod[..., (3 + i) * D_MODEL:(4 + i) * D_MODEL] for i in range(3))
        y = y_all[row:row + b * L].reshape(b, L, D_MODEL)
        row += b * L
        x1 = _ln_stats(DN_ALPHA * x + (1.0 + gate1) * y) * lp['ln1_g'] + lp['ln1_b']
        h2 = _ln_stats(x1) * (1.0 + scale2) + shift2
        y2 = expert_choice_ffn(h2.reshape(b * L, D_MODEL), lp['w_router'], lp['b_router'],
                               lp['w_gate'], lp['w_up'], lp['w_down']).reshape(b, L, D_MODEL)
        out_sets.append(_ln_stats(DN_ALPHA * x1 + (1.0 + gate2) * y2) * lp['ln2_g'] + lp['ln2_b'])
    return out_sets, new_states


def kernel(x_prompt, x_sample, state_gdn, state_mlstm_c, state_mlstm_n, state_mlstm_m, c, c_ctx, w_ada, b_ada, w_in, gdn_conv, gdn_a_log, gdn_dt_bias, gdn_norm, ml_i_bias, ml_f_bias, ml_norm, hy_conv, hy_w1, hy_b1, hy_w2, hy_b2, hy_w3, hy_b3, hy_freq, hy_rate, hy_skip, w_br_a, w_br_b, w_br_c, w_out, ln1_g, ln1_b, ln2_g, ln2_b, w_router, b_router, w_gate, w_up, w_down):
    bp = x_prompt.shape[0]
    ls = x_sample.shape[1]
    xp = x_prompt
    xs = x_sample + grid_pos_embed(ls)[None]
    zero_states = (jnp.zeros((bp, 2, H_A, DK_A, DV_A), F32), jnp.zeros((bp, 2, H_B, DQK_B, DV_B), F32),
                   jnp.zeros((bp, 2, H_B, DQK_B), F32), jnp.zeros((bp, 2, H_B), F32))
    cond = jax.nn.silu(jnp.concatenate([c_ctx[None], c], 0))
    cond = jnp.pad(cond, ((0, 16 - cond.shape[0]), (0, 0)))
    ctx_states = []
    stacked = dict(w_in=w_in, gdn_conv=gdn_conv, gdn_a_log=gdn_a_log, gdn_dt_bias=gdn_dt_bias, gdn_norm=gdn_norm,
                   ml_i_bias=ml_i_bias, ml_f_bias=ml_f_bias, ml_norm=ml_norm, hy_conv=hy_conv, hy_w1=hy_w1,
                   hy_b1=hy_b1, hy_w2=hy_w2, hy_b2=hy_b2, hy_w3=hy_w3, hy_b3=hy_b3, hy_freq=hy_freq,
                   hy_rate=hy_rate, hy_skip=hy_skip, w_br_a=w_br_a, w_br_b=w_br_b, w_br_c=w_br_c, w_out=w_out,
                   ln1_g=ln1_g, ln1_b=ln1_b, ln2_g=ln2_g, ln2_b=ln2_b, w_router=w_router, b_router=b_router,
                   w_gate=w_gate, w_up=w_up, w_down=w_down)
    for l in range(DEPTH):
        lp = {name: t[l] for name, t in stacked.items()}
        mod = pmm(cond, w_ada[l]) + b_ada[l]
        mod_ctx = mod[0].reshape(1, 1, 6 * D_MODEL)
        mod_lat = mod[1:1 + c.shape[0]][:, None, :]
        lat_states = (state_gdn[:, l], state_mlstm_c[:, l], state_mlstm_n[:, l], state_mlstm_m[:, l])
        (xp, xs), (st_ctx, _) = trunk_layer([xp, xs], [mod_ctx, mod_lat], lp, [zero_states, lat_states])
        ctx_states.append(st_ctx)
    outs = tuple(jnp.stack([st[i] for st in ctx_states], 1) for i in range(4))
    return (xp, xs) + outs
```

```python
import functools
import math

import jax
import jax.numpy as jnp
import numpy as np
from jax import lax
from jax.experimental import pallas as pl
from jax.experimental.pallas import tpu as pltpu

D_MODEL = 2048
DEPTH = 4
GRID_W = 64
CHUNK = 64
H_A = 8
DK_A = 128
DV_A = 128
QK_A = H_A * DK_A
V_A = H_A * DV_A
H_B = 4
DQK_B = 128
DV_B = 256
QK_B = H_B * DQK_B
V_B = H_B * DV_B
HY_CH = 1024
HY_ORDER = 2
HY_BANDS = 16
N_EXPERTS = 16
D_EXPERT = 1024
CAP_FACTOR = 2
DN_ALPHA = (2 * DEPTH) ** 0.25
LN_EPS = 1e-6
F32 = jnp.float32
BF16 = jnp.bfloat16

LANES = 128
INTRA_GROUP = 4
VMEM_LIMIT_BYTES = 48 * 1024 * 1024

COL_QA, COL_KA, COL_VA, COL_ZA = 0, QK_A, 2 * QK_A, 2 * QK_A + V_A
COL_QB = 2 * QK_A + 2 * V_A
COL_KB = COL_QB + QK_B
COL_VB = COL_KB + QK_B
COL_OB = COL_VB + V_B
COL_HY = COL_OB + V_B
COL_GA = COL_HY + 3 * HY_CH
COL_GB = COL_GA + D_MODEL
COL_GC = COL_GB + D_MODEL
N_BIG = COL_GC + D_MODEL


def _mm_kernel(x_ref, w_ref, o_ref):
    o_ref[...] = jnp.dot(x_ref[...], w_ref[...], preferred_element_type=F32).astype(o_ref.dtype)


def _pick(n, pref):
    for t in pref:
        if n % t == 0:
            return t
    return n


def pmm(x, w, out_dtype=F32):
    m, k = x.shape
    _, n = w.shape
    tm = _pick(m, (1024, 512, 256, 128, 16))
    tn = _pick(n, (512, 256, 128))
    return pl.pallas_call(
        _mm_kernel,
        grid=(m // tm, n // tn),
        in_specs=[pl.BlockSpec((tm, k), lambda i, j: (i, 0)),
                  pl.BlockSpec((k, tn), lambda i, j: (0, j))],
        out_specs=pl.BlockSpec((tm, tn), lambda i, j: (i, j)),
        out_shape=jax.ShapeDtypeStruct((m, n), out_dtype),
        compiler_params=pltpu.CompilerParams(
            dimension_semantics=("parallel", "parallel"), vmem_limit_bytes=VMEM_LIMIT_BYTES),
        name="pmm",
    )(x.astype(BF16), w.astype(BF16))


def pbmm(x, w, out_dtype=F32):
    e, m, k = x.shape
    _, _, n = w.shape
    tm = _pick(m, (1024, 512, 256, 128, 8))
    tn = _pick(n, (512, 256, 128))
    return pl.pallas_call(
        _mm_kernel,
        grid=(e, m // tm, n // tn),
        in_specs=[pl.BlockSpec((None, tm, k), lambda b, i, j: (b, i, 0)),
                  pl.BlockSpec((None, k, tn), lambda b, i, j: (b, 0, j))],
        out_specs=pl.BlockSpec((None, tm, tn), lambda b, i, j: (b, i, j)),
        out_shape=jax.ShapeDtypeStruct((e, m, n), out_dtype),
        compiler_params=pltpu.CompilerParams(
            dimension_semantics=("parallel", "parallel", "parallel"), vmem_limit_bytes=VMEM_LIMIT_BYTES),
        name="pbmm",
    )(x.astype(BF16), w.astype(BF16))


def _nn(a, b):
    return jnp.dot(a, b, preferred_element_type=F32)


def _nt(a, b):
    return lax.dot_general(a, b, (((1,), (1,)), ((), ())), preferred_element_type=F32)


def _tn(a, b):
    return lax.dot_general(a, b, (((0,), (0,)), ((), ())), preferred_element_type=F32)


def _split(a):
    hi = a.astype(BF16)
    return hi, (a - hi.astype(F32)).astype(BF16)


def _mm1(f, a, b):
    return f(a.astype(BF16), b.astype(BF16))


def _mm3(f, a, b):
    ah, al = _split(a)
    bh, bl = _split(b)
    return f(ah, bh) + (f(ah, bl) + f(al, bh))


def _chunk_masks():
    ri = lax.broadcasted_iota(jnp.int32, (CHUNK, CHUNK), 0)
    ci = lax.broadcasted_iota(jnp.int32, (CHUNK, CHUNK), 1)
    eye = ri == ci
    incl = (ri >= ci, ri <= ci)
    strict = (ri > ci, ri < ci)
    return eye, incl, strict


def _to_col(eye, row):
    return jnp.sum(jnp.where(eye, row, 0.0), axis=1, keepdims=True)


def _gdn_kernel(q_ref, k_ref, v_ref, z_ref, cw_ref, g_ref, s0_ref, nw_ref, y_ref, sf_ref,
                qs, ks, vs, wq_s, u_s, kd_s, p_s, o_s, st_s, *, L):
    n = L // CHUNK
    row = lax.broadcasted_iota(jnp.int32, (L, LANES), 0)

    def conv_silu(x_ref, part):
        x = x_ref[...]
        w = cw_ref[part]
        x_prev = jnp.where(row == 0, 0.0, pltpu.roll(x, 1, 0))
        x_next = jnp.where(row == L - 1, 0.0, pltpu.roll(x, L - 1, 0))
        y = x_prev * w[0:1] + x * w[1:2] + x_next * w[2:3]
        return y * jax.nn.sigmoid(y)

    def l2n(x):
        return x * lax.rsqrt(jnp.sum(x * x, axis=-1, keepdims=True) + LN_EPS)

    qs[...] = l2n(conv_silu(q_ref, 0)) * (DK_A ** -0.5)
    ks[...] = l2n(conv_silu(k_ref, 1))
    vs[...] = conv_silu(v_ref, 2)

    eye, incl, strict = _chunk_masks()
    eye_f = jnp.where(eye, 1.0, 0.0)

    def intra(grp, carry):
        chains = []
        for j in range(INTRA_GROUP):
            c = grp * INTRA_GROUP + j
            r0 = pl.multiple_of(c * CHUNK, CHUNK)
            qc = qs[pl.ds(r0, CHUNK), :]
            kc = ks[pl.ds(r0, CHUNK), :]
            vc = vs[pl.ds(r0, CHUNK), :]
            gates = g_ref[c]
            kk = _mm3(_nt, kc, kc)
            qk = _mm1(_nt, qc, kc)
            for d in range(2):
                g_row = gates[2 * d:2 * d + 1, :CHUNK]
                b_row = gates[2 * d + 1:2 * d + 2, :CHUNK]
                gl_row = gates[4 + d:5 + d, :]
                g_col = _to_col(eye, g_row)
                b_col = _to_col(eye, b_row)
                dec = jnp.exp(jnp.where(incl[d], g_col - g_row, -jnp.inf))
                lmat = b_col * kk * jnp.where(strict[d], dec, 0.0)
                kd_s[d, pl.ds(r0, CHUNK), :] = jnp.exp(gl_row - g_col) * kc
                wq_s[d, c, CHUNK:, :] = jnp.exp(g_col) * qc
                p_s[d, c] = qk * dec
                chains.append(dict(d=d, c=c, r0=r0, pw=lmat, tinv=eye_f - lmat,
                                   rhs_w=(b_col * jnp.exp(g_col)) * kc, rhs_u=b_col * vc))
        for _ in range(int(math.log2(CHUNK)) - 1):
            for ch in chains:
                ch['pw'] = _mm3(_nn, ch['pw'], ch['pw'])
            for ch in chains:
                ch['tinv'] = ch['tinv'] + _mm3(_nn, ch['tinv'], ch['pw'])
        for ch in chains:
            d, r0 = ch['d'], ch['r0']
            wq_s[d, ch['c'], :CHUNK, :] = _mm3(_nn, ch['tinv'], ch['rhs_w'])
            u_s[d, pl.ds(r0, CHUNK), :] = _mm3(_nn, ch['tinv'], ch['rhs_u'])
        return carry

    lax.fori_loop(0, n // INTRA_GROUP, intra, 0)

    st_s[...] = s0_ref[...]

    def scan(i, carry):
        cs = (i, n - 1 - i)
        r0s = [pl.multiple_of(c * CHUNK, CHUNK) for c in cs]
        s = [st_s[d] for d in range(2)]
        ws = [_mm1(_nn, wq_s[d, cs[d]], s[d]) for d in range(2)]
        uc = [u_s[d, pl.ds(r0s[d], CHUNK), :] - ws[d][:CHUNK] for d in range(2)]
        pu = [_mm1(_nn, p_s[d, cs[d]], uc[d]) for d in range(2)]
        ku = [_mm1(_tn, kd_s[d, pl.ds(r0s[d], CHUNK), :], uc[d]) for d in range(2)]
        for d in range(2):
            o_s[d, pl.ds(r0s[d], CHUNK), :] = ws[d][CHUNK:] + pu[d]
            st_s[d] = jnp.exp(g_ref[cs[d]][4 + d:5 + d, :]) * s[d] + ku[d]
        return carry

    lax.fori_loop(0, n, scan, 0)

    o = o_s[0] + o_s[1]
    z = z_ref[...]
    o = o * lax.rsqrt(jnp.mean(o * o, axis=-1, keepdims=True) + LN_EPS) * nw_ref[...]
    y_ref[...] = o * (z * jax.nn.sigmoid(z))
    sf_ref[...] = st_s[...]


def gdn_branch(proj, row_blk0, nb, L, gates, conv_w, s0, norm_w):
    n = L // CHUNK
    seq = lambda col: pl.BlockSpec((L, LANES), lambda b, h, col=col: (row_blk0 + b, col // LANES + h))
    scr = lambda *s: pltpu.VMEM(s, F32)
    return pl.pallas_call(
        functools.partial(_gdn_kernel, L=L),
        grid=(nb, H_A),
        in_specs=[seq(COL_QA), seq(COL_KA), seq(COL_VA), seq(COL_ZA),
                  pl.BlockSpec((3, None, 3, LANES), lambda b, h: (0, h, 0, 0)),
                  pl.BlockSpec((None, None, n, 8, LANES), lambda b, h: (b, h, 0, 0, 0)),
                  pl.BlockSpec((None, 2, None, DK_A, DV_A), lambda b, h: (b, 0, h, 0, 0)),
                  pl.BlockSpec((1, LANES), lambda b, h: (0, 0))],
        out_specs=[pl.BlockSpec((L, LANES), lambda b, h: (b, h)),
                   pl.BlockSpec((None, 2, None, DK_A, DV_A), lambda b, h: (b, 0, h, 0, 0))],
        out_shape=[jax.ShapeDtypeStruct((nb * L, V_A), F32),
                   jax.ShapeDtypeStruct((nb, 2, H_A, DK_A, DV_A), F32)],
        scratch_shapes=[scr(L, LANES), scr(L, LANES), scr(L, LANES),
                        scr(2, n, 2 * CHUNK, LANES), scr(2, L, LANES), scr(2, L, LANES),
                        scr(2, n, CHUNK, CHUNK), scr(2, L, LANES), scr(2, DK_A, DV_A)],
        compiler_params=pltpu.CompilerParams(
            dimension_semantics=("parallel", "parallel"), vmem_limit_bytes=VMEM_LIMIT_BYTES),
        name="gdn_branch",
    )(proj, proj, proj, proj, conv_w, gates, s0, norm_w)


def _mlstm_kernel(q_ref, k_ref, v_ref, ob_ref, g_ref, c0_ref, n0_ref, m0_ref, nw_ref,
                  y_ref, cf_ref, nf_ref, mf_ref, h_s, c_s, n_s, m_s, *, L):
    n = L // CHUNK
    eye, incl, _ = _chunk_masks()
    c_s[...] = c0_ref[...]
    n_s[...] = n0_ref[...]
    m_s[...] = m0_ref[...]

    def body(i, carry):
        for d in range(2):
            c = i if d == 0 else n - 1 - i
            r0 = pl.multiple_of(c * CHUNK, CHUNK)
            qc = q_ref[pl.ds(r0, CHUNK), :]
            kc = k_ref[pl.ds(r0, CHUNK), :] * (DQK_B ** -0.5)
            vc = v_ref[pl.ds(r0, CHUNK), :]
            gates = g_ref[c]
            b_row = gates[2 * d:2 * d + 1, :CHUNK]
            i_row = gates[2 * d + 1:2 * d + 2, :CHUNK]
            bl_row = gates[4 + d:5 + d, :]
            wm_row = gates[6 + d:7 + d, :]
            b_col = _to_col(eye, b_row)
            i_col = _to_col(eye, i_row)
            dlog = jnp.where(incl[d], b_col - b_row + i_row, -jnp.inf)
            dmax = jnp.max(dlog, axis=1, keepdims=True)
            pw = jnp.exp(dlog - dmax) * _mm1(_nt, qc, kc)
            intra_num = _mm1(_nn, pw, vc)
            intra_den = jnp.sum(pw, axis=1, keepdims=True)
            ewk = jnp.exp(bl_row[:, :1] - b_col + i_col - wm_row[:, :1]) * kc
            dc = _mm1(_tn, ewk, vc)
            dn = jnp.sum(ewk, axis=0, keepdims=True)

            cm, nm, mm = c_s[d], n_s[d], m_s[d]
            alog = b_col + mm
            mt = jnp.maximum(alog, dmax)
            wi = jnp.exp(alog - mt)[:, :1]
            wa = jnp.exp(dmax - mt)[:, :1]
            num = wi * _mm1(_nn, qc, cm) + wa * intra_num
            den = wi * jnp.sum(qc * nm, axis=1, keepdims=True) + wa * intra_den
            h_s[d, pl.ds(r0, CHUNK), :] = num / jnp.maximum(jnp.abs(den), jnp.exp(-mt[:, :1]))

            m_new = jnp.maximum(bl_row + mm, wm_row)
            a = jnp.exp(bl_row + mm - m_new)
            e = jnp.exp(wm_row - m_new)
            c_s[d] = a[:, :1] * cm + e[:, :1] * dc
            n_s[d] = a * nm + e * dn
            m_s[d] = m_new
        return carry

    lax.fori_loop(0, n, body, 0)

    h = h_s[0] + h_s[1]
    h = h * lax.rsqrt(jnp.mean(h * h, axis=-1, keepdims=True) + LN_EPS) * nw_ref[...]
    y_ref[...] = h * jax.nn.sigmoid(ob_ref[...])
    cf_ref[...] = c_s[...]
    nf_ref[...] = n_s[...]
    mf_ref[...] = m_s[...]


def mlstm_branch(proj, row_blk0, nb, L, gates, c0, n0, m0, norm_w):
    n = L // CHUNK
    seq = lambda col, w: pl.BlockSpec((L, w), lambda b, h, col=col, w=w: (row_blk0 + b, col // w + h))
    st = lambda *s: pl.BlockSpec((None, 2, None) + s, lambda b, h: (b, 0, h, 0, 0))
    scr = lambda *s: pltpu.VMEM(s, F32)
    return pl.pallas_call(
        functools.partial(_mlstm_kernel, L=L),
        grid=(nb, H_B),
        in_specs=[seq(COL_QB, DQK_B), seq(COL_KB, DQK_B), seq(COL_VB, DV_B), seq(COL_OB, DV_B),
                  pl.BlockSpec((None, None, n, 8, LANES), lambda b, h: (b, h, 0, 0, 0)),
                  st(DQK_B, DV_B), st(1, DQK_B), st(1, LANES),
                  pl.BlockSpec((1, DV_B), lambda b, h: (0, 0))],
        out_specs=[pl.BlockSpec((L, DV_B), lambda b, h: (b, h)),
                   st(DQK_B, DV_B), st(1, DQK_B), st(1, LANES)],
        out_shape=[jax.ShapeDtypeStruct((nb * L, V_B), F32),
                   jax.ShapeDtypeStruct((nb, 2, H_B, DQK_B, DV_B), F32),
                   jax.ShapeDtypeStruct((nb, 2, H_B, 1, DQK_B), F32),
                   jax.ShapeDtypeStruct((nb, 2, H_B, 1, LANES), F32)],
        scratch_shapes=[scr(2, L, DV_B), scr(2, DQK_B, DV_B), scr(2, 1, DQK_B), scr(2, 1, LANES)],
        compiler_params=pltpu.CompilerParams(
            dimension_semantics=("parallel", "parallel"), vmem_limit_bytes=VMEM_LIMIT_BYTES),
        name="mlstm_branch",
    )(proj, proj, proj, proj, gates, c0, n0, m0, norm_w)


def _ln_stats(x):
    mu = jnp.mean(x, -1, keepdims=True)
    var = jnp.mean(jnp.square(x - mu), -1, keepdims=True)
    return (x - mu) * lax.rsqrt(var + LN_EPS)


def centred_conv(x, w):
    k = w.shape[0]
    p = k // 2
    L = x.shape[1]
    xp = jnp.pad(x, ((0, 0), (p, p), (0, 0)))
    y = xp[:, 0:L] * w[0]
    for j in range(1, k):
        y = y + xp[:, j:j + L] * w[j]
    return y


def hyena_filters(L, w1, b1, w2, b2, w3, b3, freq, rate):
    pos = jnp.arange(L, dtype=F32)
    t = pos / L
    ang = (2.0 * math.pi) * t[:, None] * jnp.arange(1, HY_BANDS + 1, dtype=F32)
    feats = jnp.concatenate([t[:, None], jnp.sin(ang), jnp.cos(ang)], -1)
    hp = lax.Precision.HIGHEST
    z = jnp.sin(freq[0] * (jnp.dot(feats, w1, precision=hp) + b1))
    z = jnp.sin(freq[1] * (jnp.dot(z, w2, precision=hp) + b2))
    filt = jnp.dot(z, w3, precision=hp) + b3
    lag = jnp.abs(pos - L // 2) / L
    filt = filt * jnp.exp(-lag[:, None] * rate)
    return filt.reshape(L, HY_ORDER, HY_CH)


def dft_size(L):
    n = 3 * L // 2
    return n if (n // 2) % LANES == 0 else 2 * L


def dft_matrices(L):
    N = dft_size(L)
    F = N // 2
    k = jnp.arange(F, dtype=jnp.int32)
    t = jnp.arange(L, dtype=jnp.int32)
    w = 2.0 * math.pi / N
    ang = w * ((k[:, None] * t[None, :]) % N).astype(F32)
    alt = (1 - 2 * (t % 2)).astype(F32)
    cf = jnp.cos(ang)
    sf = jnp.where(k[:, None] == 0, alt[None, :], -jnp.sin(ang))
    tt = t + L // 2
    ang_i = w * ((tt[:, None] * k[None, :]) % N).astype(F32)
    alt_i = (1 - 2 * (tt % 2)).astype(F32)
    ci = jnp.where(k[None, :] == 0, 1.0 / N, (2.0 / N) * jnp.cos(ang_i))
    si = jnp.where(k[None, :] == 0, alt_i[:, None] / N, (-2.0 / N) * jnp.sin(ang_i))
    return cf, sf, ci, si


def _mm3_kernel(x_ref, w_ref, o_ref):
    o_ref[...] = _mm3(_nn, x_ref[...], w_ref[...])


def pmm3(x, w):
    m, k = x.shape
    _, n = w.shape
    tm = _pick(m, (512, 256, 128))
    tn = _pick(n, (512, 256, 128))
    return pl.pallas_call(
        _mm3_kernel,
        grid=(m // tm, n // tn),
        in_specs=[pl.BlockSpec((tm, k), lambda i, j: (i, 0)),
                  pl.BlockSpec((k, tn), lambda i, j: (0, j))],
        out_specs=pl.BlockSpec((tm, tn), lambda i, j: (i, j)),
        out_shape=jax.ShapeDtypeStruct((m, n), F32),
        compiler_params=pltpu.CompilerParams(
            dimension_semantics=("parallel", "parallel"), vmem_limit_bytes=VMEM_LIMIT_BYTES),
        name="pmm3",
    )(x, w)


def _hyena_kernel(z_ref, un_ref, cwz_ref, cwu_ref, skip_ref, hr_ref, hi_ref, cf_ref, sf_ref, ci_ref, si_ref,
                  o_ref, z_s, zb_s, acc_s, *, L, ft, conv_z):
    f = pl.program_id(2)
    row = lax.broadcasted_iota(jnp.int32, (L, z_ref.shape[1]), 0)

    def conv3(x, w):
        x_prev = jnp.where(row == 0, 0.0, pltpu.roll(x, 1, 0))
        x_next = jnp.where(row == L - 1, 0.0, pltpu.roll(x, L - 1, 0))
        return x_prev * w[0:1] + x * w[1:2] + x_next * w[2:3]

    @pl.when(f == 0)
    def _():
        z = z_ref[...]
        if conv_z:
            z = conv3(z, cwz_ref[...])
        z_s[...] = z
        zb_s[...] = z.astype(BF16)

    zb = zb_s[...]
    zr = _nn(cf_ref[...], zb)
    zi = _nn(sf_ref[...], zb)
    hr = hr_ref[...]
    hi = hi_ref[...]
    packed = (f * ft + lax.broadcasted_iota(jnp.int32, zr.shape, 0)) == 0
    zihi = zi * hi
    yr = zr * hr - jnp.where(packed, 0.0, zihi)
    yi = jnp.where(packed, zihi, zr * hi + zi * hr)
    part = _nn(ci_ref[...], yr.astype(BF16)) + _nn(si_ref[...], yi.astype(BF16))

    @pl.when(f == 0)
    def _():
        acc_s[...] = part

    @pl.when(f > 0)
    def _():
        acc_s[...] += part

    @pl.when(f == pl.num_programs(2) - 1)
    def _():
        un = conv3(un_ref[...], cwu_ref[...])
        o_ref[...] = un * (acc_s[...] + skip_ref[...] * z_s[...])


def hyena_order(z_arr, z_blk0, z_col, conv_z, proj, row_blk0, un_col, nb, L, conv_w, cwz_col, skip,
                hr, hi, h_col, mats):
    cf, sf, ci, si = mats
    F = cf.shape[0]
    cb = HY_CH if L * HY_CH * 4 <= (1 << 20) else 256
    ft = _pick(F, (512, 256))
    seq = lambda blk0, col: pl.BlockSpec((L, cb), lambda b, j, f: (blk0 + b, col // cb + j))
    return pl.pallas_call(
        functools.partial(_hyena_kernel, L=L, ft=ft, conv_z=conv_z),
        grid=(nb, HY_CH // cb, F // ft),
        in_specs=[seq(z_blk0, z_col), seq(row_blk0, un_col),
                  pl.BlockSpec((3, cb), lambda b, j, f: (0, cwz_col // cb + j)),
                  pl.BlockSpec((3, cb), lambda b, j, f: (0, (un_col - COL_HY) // cb + j)),
                  pl.BlockSpec((1, cb), lambda b, j, f: (0, j)),
                  pl.BlockSpec((ft, cb), lambda b, j, f: (f, h_col // cb + j)),
                  pl.BlockSpec((ft, cb), lambda b, j, f: (f, h_col // cb + j)),
                  pl.BlockSpec((ft, L), lambda b, j, f: (f, 0)),
                  pl.BlockSpec((ft, L), lambda b, j, f: (f, 0)),
                  pl.BlockSpec((L, ft), lambda b, j, f: (0, f)),
                  pl.BlockSpec((L, ft), lambda b, j, f: (0, f))],
        out_specs=pl.BlockSpec((L, cb), lambda b, j, f: (b, j)),
        out_shape=jax.ShapeDtypeStruct((nb * L, HY_CH), F32),
        scratch_shapes=[pltpu.VMEM((L, cb), F32), pltpu.VMEM((L, cb), BF16), pltpu.VMEM((L, cb), F32)],
        compiler_params=pltpu.CompilerParams(
            dimension_semantics=("parallel", "parallel", "arbitrary"), vmem_limit_bytes=VMEM_LIMIT_BYTES),
        name="hyena_order",
    )(z_arr, proj, conv_w, conv_w, skip, hr, hi, cf, sf, ci, si)


def grid_pos_embed(n_tokens):
    rows = n_tokens // GRID_W
    quarter = D_MODEL // 4
    omega = 1.0 / (10000.0 ** (jnp.arange(quarter, dtype=F32) / quarter))
    r = jnp.broadcast_to(jnp.arange(rows, dtype=F32)[:, None, None] * omega, (rows, GRID_W, quarter))
    cl = jnp.broadcast_to(jnp.arange(GRID_W, dtype=F32)[None, :, None] * omega, (rows, GRID_W, quarter))
    pe = jnp.concatenate([jnp.sin(r), jnp.cos(r), jnp.sin(cl), jnp.cos(cl)], -1)
    return pe.reshape(rows * GRID_W, D_MODEL)


def _chunk_rows(t):
    b, L, h = t.shape
    return t.reshape(b, L // CHUNK, CHUNK, h).transpose(0, 3, 1, 2)


def _suffix_cumsum(t):
    return jnp.flip(jnp.cumsum(jnp.flip(t, -1), -1), -1)


def _pack_gate_rows(per_token, per_chunk):
    rows = [jnp.pad(t, ((0, 0), (0, 0), (0, 0), (0, LANES - CHUNK))) for t in per_token]
    rows += [jnp.broadcast_to(t[..., None], t.shape + (LANES,)) for t in per_chunk]
    return jnp.stack(rows, axis=3)


def gdn_gate_rows(beta_p, alpha_p, a_log, dt_bias):
    b, L, _ = beta_p.shape
    beta = jax.nn.sigmoid(beta_p).reshape(b, L, 2, H_A)
    loga = -jnp.exp(a_log) * jax.nn.softplus(alpha_p.reshape(b, L, 2, H_A) + dt_bias)
    g_f = jnp.cumsum(_chunk_rows(loga[:, :, 0]), -1)
    g_b = _suffix_cumsum(_chunk_rows(loga[:, :, 1]))
    zero = jnp.zeros_like(g_f[..., 0])
    return _pack_gate_rows([g_f, _chunk_rows(beta[:, :, 0]), g_b, _chunk_rows(beta[:, :, 1])],
                           [g_f[..., -1], g_b[..., 0], zero, zero])


def mlstm_gate_rows(ig_p, fg_p, i_bias, f_bias):
    b, L, _ = ig_p.shape
    ig = ig_p.reshape(b, L, 2, H_B) + i_bias
    lf = jax.nn.log_sigmoid(fg_p.reshape(b, L, 2, H_B) + f_bias)
    ig_f, ig_b = _chunk_rows(ig[:, :, 0]), _chunk_rows(ig[:, :, 1])
    bc_f = jnp.cumsum(_chunk_rows(lf[:, :, 0]), -1)
    bc_b = _suffix_cumsum(_chunk_rows(lf[:, :, 1]))
    bl_f, bl_b = bc_f[..., -1], bc_b[..., 0]
    wm_f = jnp.max(bl_f[..., None] - bc_f + ig_f, -1)
    wm_b = jnp.max(bl_b[..., None] - bc_b + ig_b, -1)
    return _pack_gate_rows([bc_f, ig_f, bc_b, ig_b], [bl_f, bl_b, wm_f, wm_b])


def seq_mixers(proj_big, proj_small, row0, nb, L, lp, mats, s_gdn, s_c, s_n, s_m):
    rows = slice(row0, row0 + nb * L)
    small = proj_small[rows].reshape(nb, L, LANES)
    blk0 = row0 // L

    conv_w = lp['gdn_conv'].reshape(3, 3, H_A, DK_A).transpose(1, 2, 0, 3)
    gates = gdn_gate_rows(small[..., 0:2 * H_A], small[..., 2 * H_A:4 * H_A], lp['gdn_a_log'], lp['gdn_dt_bias'])
    y_a, s_gdn_new = gdn_branch(proj_big, blk0, nb, L, gates, conv_w, s_gdn, lp['gdn_norm'].reshape(1, DV_A))

    o = 4 * H_A
    gates = mlstm_gate_rows(small[..., o:o + 2 * H_B], small[..., o + 2 * H_B:o + 4 * H_B],
                            lp['ml_i_bias'], lp['ml_f_bias'])
    m0 = jnp.broadcast_to(s_m[..., None, None], s_m.shape + (1, LANES))
    y_b, c_new, n_new, m_new = mlstm_branch(proj_big, blk0, nb, L, gates, s_c, s_n[..., None, :], m0,
                                            lp['ml_norm'].reshape(1, DV_B))

    filt = hyena_filters(L, lp['hy_w1'], lp['hy_b1'], lp['hy_w2'], lp['hy_b2'], lp['hy_w3'], lp['hy_b3'],
                         lp['hy_freq'], lp['hy_rate']).reshape(L, HY_ORDER * HY_CH)
    mats32, mats16 = mats[L]
    hr, hi = pmm3(mats32[0], filt), pmm3(mats32[1], filt)
    z_arr, z_blk0, z_col = proj_big, blk0, COL_HY
    for order in range(HY_ORDER):
        z_arr = hyena_order(z_arr, z_blk0, z_col, order == 0, proj_big, blk0, COL_HY + (order + 1) * HY_CH, nb, L,
                            lp['hy_conv'], 0, lp['hy_skip'][order:order + 1], hr, hi, order * HY_CH, mats16)
        z_blk0, z_col = 0, 0
    return (y_a, y_b, z_arr), (s_gdn_new, c_new, n_new[..., 0, :], m_new[..., 0, 0])


def expert_choice_ffn(hf, w_router, b_router, w_gate, w_up, w_down):
    n, d = hf.shape
    cap = CAP_FACTOR * n // N_EXPERTS
    aff = jax.nn.softmax(jnp.dot(hf, w_router, precision=lax.Precision.HIGHEST) + b_router, axis=-1)
    g, idx = lax.top_k(aff.T, cap)
    xe = hf.astype(BF16)[idx]
    hid = jax.nn.silu(pbmm(xe, w_gate)) * pbmm(xe, w_up)
    ye = pbmm(hid, w_down) * g[..., None]
    return jnp.zeros_like(hf).at[idx.reshape(-1)].add(ye.reshape(-1, d))


_IN_SIZES = (QK_A, QK_A, V_A, V_A, 2 * H_A, 2 * H_A, QK_B, QK_B, V_B, V_B, 2 * H_B, 2 * H_B,
             3 * HY_CH, D_MODEL, D_MODEL, D_MODEL)
_IN_OFFS = tuple(int(v) for v in np.concatenate([[0], np.cumsum(_IN_SIZES)]))
_SMALL_PARTS = (4, 5, 10, 11)


def trunk_layer(x_sets, mods, lp, mats, states):
    hs = []
    for x, mod in zip(x_sets, mods):
        shift1, scale1 = mod[..., :D_MODEL], mod[..., D_MODEL:2 * D_MODEL]
        hs.append((_ln_stats(x) * (1.0 + scale1) + shift1).reshape(-1, D_MODEL))
    h_all = jnp.concatenate(hs, 0)

    big_cols = [p for p in range(len(_IN_SIZES)) if p not in _SMALL_PARTS]
    w_in = lp['w_in']
    w_big = jnp.concatenate([w_in[:, _IN_OFFS[p]:_IN_OFFS[p + 1]] for p in big_cols], -1)
    w_small = jnp.concatenate([w_in[:, _IN_OFFS[p]:_IN_OFFS[p + 1]] for p in _SMALL_PARTS], -1)
    w_small = jnp.pad(w_small, ((0, 0), (0, LANES - w_small.shape[1])))
    proj_big = pmm(h_all, w_big)
    proj_small = pmm(h_all, w_small)

    mixed, new_states, row = [], [], 0
    for x, st in zip(x_sets, states):
        b, L, _ = x.shape
        outs, st_new = seq_mixers(proj_big, proj_small, row, b, L, lp, mats, *st)
        row += b * L
        mixed.append(outs)
        new_states.append(st_new)

    y_a, y_b, y_c = (jnp.concatenate(ts, 0) for ts in zip(*mixed))
    merged = (jax.nn.sigmoid(proj_big[:, COL_GA:COL_GB]) * pmm(y_a, lp['w_br_a'])
              + jax.nn.sigmoid(proj_big[:, COL_GB:COL_GC]) * pmm(y_b, lp['w_br_b'])
              + jax.nn.sigmoid(proj_big[:, COL_GC:N_BIG]) * pmm(y_c, lp['w_br_c']))
    y_all = pmm(merged, lp['w_out'])

    out_sets, row = [], 0
    for x, mod in zip(x_sets, mods):
        b, L, _ = x.shape
        gate1 = mod[..., 2 * D_MODEL:3 * D_MODEL]
        shift2, scale2, gate2 = (mod[..., (3 + i) * D_MODEL:(4 + i) * D_MODEL] for i in range(3))
        y = y_all[row:row + b * L].reshape(b, L, D_MODEL)
        row += b * L
        x1 = _ln_stats(DN_ALPHA * x + (1.0 + gate1) * y) * lp['ln1_g'] + lp['ln1_b']
        h2 = _ln_stats(x1) * (1.0 + scale2) + shift2
        y2 = expert_choice_ffn(h2.reshape(b * L, D_MODEL), lp['w_router'], lp['b_router'],
                               lp['w_gate'], lp['w_up'], lp['w_down']).reshape(b, L, D_MODEL)
        out_sets.append(_ln_stats(DN_ALPHA * x1 + (1.0 + gate2) * y2) * lp['ln2_g'] + lp['ln2_b'])
    return out_sets, new_states


def kernel(x_prompt, x_sample, state_gdn, state_mlstm_c, state_mlstm_n, state_mlstm_m, c, c_ctx, w_ada, b_ada, w_in, gdn_conv, gdn_a_log, gdn_dt_bias, gdn_norm, ml_i_bias, ml_f_bias, ml_norm, hy_conv, hy_w1, hy_b1, hy_w2, hy_b2, hy_w3, hy_b3, hy_freq, hy_rate, hy_skip, w_br_a, w_br_b, w_br_c, w_out, ln1_g, ln1_b, ln2_g, ln2_b, w_router, b_router, w_gate, w_up, w_down):
    bp = x_prompt.shape[0]
    ls = x_sample.shape[1]
    xp = x_prompt
    xs = x_sample + grid_pos_embed(ls)[None]
    zero_states = (jnp.zeros((bp, 2, H_A, DK_A, DV_A), F32), jnp.zeros((bp, 2, H_B, DQK_B, DV_B), F32),
                   jnp.zeros((bp, 2, H_B, DQK_B), F32), jnp.zeros((bp, 2, H_B), F32))
    cond = jax.nn.silu(jnp.concatenate([c_ctx[None], c], 0))
    cond = jnp.pad(cond, ((0, 16 - cond.shape[0]), (0, 0)))
    ctx_states = []
    mats = {}
    for L in (x_prompt.shape[1], ls):
        m32 = dft_matrices(L)
        mats[L] = (m32, tuple(m.astype(BF16) for m in m32))
    stacked = dict(w_in=w_in, gdn_conv=gdn_conv, gdn_a_log=gdn_a_log, gdn_dt_bias=gdn_dt_bias, gdn_norm=gdn_norm,
                   ml_i_bias=ml_i_bias, ml_f_bias=ml_f_bias, ml_norm=ml_norm, hy_conv=hy_conv, hy_w1=hy_w1,
                   hy_b1=hy_b1, hy_w2=hy_w2, hy_b2=hy_b2, hy_w3=hy_w3, hy_b3=hy_b3, hy_freq=hy_freq,
                   hy_rate=hy_rate, hy_skip=hy_skip, w_br_a=w_br_a, w_br_b=w_br_b, w_br_c=w_br_c, w_out=w_out,
                   ln1_g=ln1_g, ln1_b=ln1_b, ln2_g=ln2_g, ln2_b=ln2_b, w_router=w_router, b_router=b_router,
                   w_gate=w_gate, w_up=w_up, w_down=w_down)
    for l in range(DEPTH):
        lp = {name: t[l] for name, t in stacked.items()}
        mod = pmm(cond, w_ada[l]) + b_ada[l]
        mod_ctx = mod[0].reshape(1, 1, 6 * D_MODEL)
        mod_lat = mod[1:1 + c.shape[0]][:, None, :]
        lat_states = (state_gdn[:, l], state_mlstm_c[:, l], state_mlstm_n[:, l], state_mlstm_m[:, l])
        (xp, xs), (st_ctx, _) = trunk_layer([xp, xs], [mod_ctx, mod_lat], lp, mats, [zero_states, lat_states])
        ctx_states.append(st_ctx)
    outs = tuple(jnp.stack([st[i] for st in ctx_states], 1) for i in range(4))
    return (xp, xs) + outs
```

```python
import functools
import math

import jax
import jax.numpy as jnp
import numpy as np
from jax import lax
from jax.experimental import pallas as pl
from jax.experimental.pallas import tpu as pltpu

D_MODEL = 2048
DEPTH = 4
GRID_W = 64
CHUNK = 64
H_A = 8
DK_A = 128
DV_A = 128
QK_A = H_A * DK_A
V_A = H_A * DV_A
H_B = 4
DQK_B = 128
DV_B = 256
QK_B = H_B * DQK_B
V_B = H_B * DV_B
HY_CH = 1024
HY_ORDER = 2
HY_BANDS = 16
N_EXPERTS = 16
D_EXPERT = 1024
CAP_FACTOR = 2
DN_ALPHA = (2 * DEPTH) ** 0.25
LN_EPS = 1e-6
F32 = jnp.float32
BF16 = jnp.bfloat16

LANES = 128
INTRA_GROUP = 4
ML_GROUP = 4
VMEM_LIMIT_BYTES = 48 * 1024 * 1024

COL_QA, COL_KA, COL_VA, COL_ZA = 0, QK_A, 2 * QK_A, 2 * QK_A + V_A
COL_QB = 2 * QK_A + 2 * V_A
COL_KB = COL_QB + QK_B
COL_VB = COL_KB + QK_B
COL_OB = COL_VB + V_B
COL_HY = COL_OB + V_B
COL_GA = COL_HY + 3 * HY_CH
COL_GB = COL_GA + D_MODEL
COL_GC = COL_GB + D_MODEL
N_BIG = COL_GC + D_MODEL


def _mm_kernel(x_ref, w_ref, o_ref):
    o_ref[...] = jnp.dot(x_ref[...], w_ref[...], preferred_element_type=F32).astype(o_ref.dtype)


def _pick(n, pref):
    for t in pref:
        if n % t == 0:
            return t
    return n


def pmm(x, w, out_dtype=F32):
    m, k = x.shape
    _, n = w.shape
    tm = _pick(m, (1024, 512, 256, 128, 16))
    tn = _pick(n, (512, 256, 128))
    return pl.pallas_call(
        _mm_kernel,
        grid=(m // tm, n // tn),
        in_specs=[pl.BlockSpec((tm, k), lambda i, j: (i, 0)),
                  pl.BlockSpec((k, tn), lambda i, j: (0, j))],
        out_specs=pl.BlockSpec((tm, tn), lambda i, j: (i, j)),
        out_shape=jax.ShapeDtypeStruct((m, n), out_dtype),
        compiler_params=pltpu.CompilerParams(
            dimension_semantics=("parallel", "parallel"), vmem_limit_bytes=VMEM_LIMIT_BYTES),
        name="pmm",
    )(x.astype(BF16), w.astype(BF16))


def pbmm(x, w, out_dtype=F32):
    e, m, k = x.shape
    _, _, n = w.shape
    tm = _pick(m, (1024, 512, 256, 128, 8))
    tn = _pick(n, (512, 256, 128))
    return pl.pallas_call(
        _mm_kernel,
        grid=(e, m // tm, n // tn),
        in_specs=[pl.BlockSpec((None, tm, k), lambda b, i, j: (b, i, 0)),
                  pl.BlockSpec((None, k, tn), lambda b, i, j: (b, 0, j))],
        out_specs=pl.BlockSpec((None, tm, tn), lambda b, i, j: (b, i, j)),
        out_shape=jax.ShapeDtypeStruct((e, m, n), out_dtype),
        compiler_params=pltpu.CompilerParams(
            dimension_semantics=("parallel", "parallel", "parallel"), vmem_limit_bytes=VMEM_LIMIT_BYTES),
        name="pbmm",
    )(x.astype(BF16), w.astype(BF16))


def _merge_kernel(ya_ref, yb_ref, yc_ref, wa_ref, wb_ref, wc_ref, ga_ref, gb_ref, gc_ref, o_ref):
    acc = jax.nn.sigmoid(ga_ref[...]) * jnp.dot(ya_ref[...], wa_ref[...], preferred_element_type=F32)
    acc = acc + jax.nn.sigmoid(gb_ref[...]) * jnp.dot(yb_ref[...], wb_ref[...], preferred_element_type=F32)
    acc = acc + jax.nn.sigmoid(gc_ref[...]) * jnp.dot(yc_ref[...], wc_ref[...], preferred_element_type=F32)
    o_ref[...] = acc.astype(o_ref.dtype)


def branch_merge(y_a, y_b, y_c, w_a, w_b, w_c, proj):
    m = y_a.shape[0]
    tm, tn = _pick(m, (1024, 512, 256)), 512
    y_spec = lambda k: pl.BlockSpec((tm, k), lambda i, j: (i, 0))
    w_spec = lambda k: pl.BlockSpec((k, tn), lambda i, j: (0, j))
    g_spec = lambda col: pl.BlockSpec((tm, tn), lambda i, j, col=col: (i, col // tn + j))
    return pl.pallas_call(
        _merge_kernel,
        grid=(m // tm, D_MODEL // tn),
        in_specs=[y_spec(V_A), y_spec(V_B), y_spec(HY_CH), w_spec(V_A), w_spec(V_B), w_spec(HY_CH),
                  g_spec(COL_GA), g_spec(COL_GB), g_spec(COL_GC)],
        out_specs=pl.BlockSpec((tm, tn), lambda i, j: (i, j)),
        out_shape=jax.ShapeDtypeStruct((m, D_MODEL), BF16),
        compiler_params=pltpu.CompilerParams(
            dimension_semantics=("parallel", "parallel"), vmem_limit_bytes=VMEM_LIMIT_BYTES),
        name="branch_merge",
    )(y_a, y_b, y_c, w_a.astype(BF16), w_b.astype(BF16), w_c.astype(BF16), proj, proj, proj)


def _expert_hidden_kernel(x_ref, wg_ref, wu_ref, o_ref, wg_s, wu_s):
    @pl.when(pl.program_id(2) == 0)
    def _():
        wg_s[...] = wg_ref[...].astype(BF16)
        wu_s[...] = wu_ref[...].astype(BF16)

    x = x_ref[...]
    g = jnp.dot(x, wg_s[...], preferred_element_type=F32)
    u = jnp.dot(x, wu_s[...], preferred_element_type=F32)
    o_ref[...] = (g * jax.nn.sigmoid(g) * u).astype(o_ref.dtype)


def expert_hidden(xe, w_gate, w_up):
    e, m, k = xe.shape
    f = w_gate.shape[2]
    tm, tn = _pick(m, (1280, 1024, 512, 256)), 256
    return pl.pallas_call(
        _expert_hidden_kernel,
        grid=(e, f // tn, m // tm),
        in_specs=[pl.BlockSpec((None, tm, k), lambda b, j, i: (b, i, 0)),
                  pl.BlockSpec((None, k, tn), lambda b, j, i: (b, 0, j)),
                  pl.BlockSpec((None, k, tn), lambda b, j, i: (b, 0, j))],
        out_specs=pl.BlockSpec((None, tm, tn), lambda b, j, i: (b, i, j)),
        out_shape=jax.ShapeDtypeStruct((e, m, f), BF16),
        scratch_shapes=[pltpu.VMEM((k, tn), BF16), pltpu.VMEM((k, tn), BF16)],
        compiler_params=pltpu.CompilerParams(
            dimension_semantics=("parallel", "parallel", "arbitrary"), vmem_limit_bytes=VMEM_LIMIT_BYTES),
        name="expert_hidden",
    )(xe, w_gate, w_up)


def _expert_down_kernel(h_ref, w_ref, g_ref, o_ref, w_s):
    @pl.when(pl.program_id(2) == 0)
    def _():
        w_s[...] = w_ref[...].astype(BF16)

    o_ref[...] = jnp.dot(h_ref[...], w_s[...], preferred_element_type=F32) * g_ref[...]


def expert_down(hid, w_down, gate):
    e, m, k = hid.shape
    n = w_down.shape[2]
    tm, tn = _pick(m, (1280, 1024, 512, 256)), 512
    return pl.pallas_call(
        _expert_down_kernel,
        grid=(e, n // tn, m // tm),
        in_specs=[pl.BlockSpec((None, tm, k), lambda b, j, i: (b, i, 0)),
                  pl.BlockSpec((None, k, tn), lambda b, j, i: (b, 0, j)),
                  pl.BlockSpec((None, tm, 1), lambda b, j, i: (b, i, 0))],
        out_specs=pl.BlockSpec((None, tm, tn), lambda b, j, i: (b, i, j)),
        out_shape=jax.ShapeDtypeStruct((e, m, n), F32),
        scratch_shapes=[pltpu.VMEM((k, tn), BF16)],
        compiler_params=pltpu.CompilerParams(
            dimension_semantics=("parallel", "parallel", "arbitrary"), vmem_limit_bytes=VMEM_LIMIT_BYTES),
        name="expert_down",
    )(hid, w_down, gate)


def _nn(a, b):
    return jnp.dot(a, b, preferred_element_type=F32)


def _nt(a, b):
    return lax.dot_general(a, b, (((1,), (1,)), ((), ())), preferred_element_type=F32)


def _tn(a, b):
    return lax.dot_general(a, b, (((0,), (0,)), ((), ())), preferred_element_type=F32)


def _split(a):
    hi = a.astype(BF16)
    return hi, (a - hi.astype(F32)).astype(BF16)


def _mm1(f, a, b):
    return f(a.astype(BF16), b.astype(BF16))


def _mm3(f, a, b):
    ah, al = _split(a)
    bh, bl = _split(b)
    return f(ah, bh) + (f(ah, bl) + f(al, bh))


def _chunk_masks():
    ri = lax.broadcasted_iota(jnp.int32, (CHUNK, CHUNK), 0)
    ci = lax.broadcasted_iota(jnp.int32, (CHUNK, CHUNK), 1)
    eye = ri == ci
    incl = (ri >= ci, ri <= ci)
    strict = (ri > ci, ri < ci)
    return eye, incl, strict


def _to_col(eye, row):
    return jnp.sum(jnp.where(eye, row, 0.0), axis=1, keepdims=True)


def _gdn_kernel(q_ref, k_ref, v_ref, z_ref, cw_ref, g_ref, s0_ref, nw_ref, y_ref, sf_ref,
                qs, ks, vs, wq_s, u_s, kd_s, p_s, o_s, st_s, *, L):
    n = L // CHUNK
    row = lax.broadcasted_iota(jnp.int32, (L, LANES), 0)

    def conv_silu(x_ref, part):
        x = x_ref[...]
        w = cw_ref[part]
        x_prev = jnp.where(row == 0, 0.0, pltpu.roll(x, 1, 0))
        x_next = jnp.where(row == L - 1, 0.0, pltpu.roll(x, L - 1, 0))
        y = x_prev * w[0:1] + x * w[1:2] + x_next * w[2:3]
        return y * jax.nn.sigmoid(y)

    def l2n(x):
        return x * lax.rsqrt(jnp.sum(x * x, axis=-1, keepdims=True) + LN_EPS)

    qs[...] = l2n(conv_silu(q_ref, 0)) * (DK_A ** -0.5)
    ks[...] = l2n(conv_silu(k_ref, 1))
    vs[...] = conv_silu(v_ref, 2)

    eye, incl, strict = _chunk_masks()
    eye_f = jnp.where(eye, 1.0, 0.0)

    def intra(grp, carry):
        chains = []
        for j in range(INTRA_GROUP):
            c = grp * INTRA_GROUP + j
            r0 = pl.multiple_of(c * CHUNK, CHUNK)
            qc = qs[pl.ds(r0, CHUNK), :]
            kc = ks[pl.ds(r0, CHUNK), :]
            vc = vs[pl.ds(r0, CHUNK), :]
            gates = g_ref[c]
            kk = _mm3(_nt, kc, kc)
            qk = _mm1(_nt, qc, kc)
            for d in range(2):
                g_row = gates[2 * d:2 * d + 1, :CHUNK]
                b_row = gates[2 * d + 1:2 * d + 2, :CHUNK]
                gl_row = gates[4 + d:5 + d, :]
                g_col = _to_col(eye, g_row)
                b_col = _to_col(eye, b_row)
                dec = jnp.exp(jnp.where(incl[d], g_col - g_row, -jnp.inf))
                lmat = b_col * kk * jnp.where(strict[d], dec, 0.0)
                kd_s[d, pl.ds(r0, CHUNK), :] = jnp.exp(gl_row - g_col) * kc
                wq_s[d, c, CHUNK:, :] = jnp.exp(g_col) * qc
                p_s[d, c] = qk * dec
                chains.append(dict(d=d, c=c, r0=r0, pw=lmat, tinv=eye_f - lmat,
                                   rhs_w=(b_col * jnp.exp(g_col)) * kc, rhs_u=b_col * vc))
        for _ in range(int(math.log2(CHUNK)) - 1):
            for ch in chains:
                ch['pw'] = _mm3(_nn, ch['pw'], ch['pw'])
            for ch in chains:
                ch['tinv'] = ch['tinv'] + _mm3(_nn, ch['tinv'], ch['pw'])
        for ch in chains:
            d, r0 = ch['d'], ch['r0']
            wq_s[d, ch['c'], :CHUNK, :] = _mm3(_nn, ch['tinv'], ch['rhs_w'])
            u_s[d, pl.ds(r0, CHUNK), :] = _mm3(_nn, ch['tinv'], ch['rhs_u'])
        return carry

    lax.fori_loop(0, n // INTRA_GROUP, intra, 0)

    st_s[...] = s0_ref[...]

    def scan(i, carry):
        cs = (i, n - 1 - i)
        r0s = [pl.multiple_of(c * CHUNK, CHUNK) for c in cs]
        s = [st_s[d] for d in range(2)]
        ws = [_mm1(_nn, wq_s[d, cs[d]], s[d]) for d in range(2)]
        uc = [u_s[d, pl.ds(r0s[d], CHUNK), :] - ws[d][:CHUNK] for d in range(2)]
        pu = [_mm1(_nn, p_s[d, cs[d]], uc[d]) for d in range(2)]
        ku = [_mm1(_tn, kd_s[d, pl.ds(r0s[d], CHUNK), :], uc[d]) for d in range(2)]
        for d in range(2):
            o_s[d, pl.ds(r0s[d], CHUNK), :] = ws[d][CHUNK:] + pu[d]
            st_s[d] = jnp.exp(g_ref[cs[d]][4 + d:5 + d, :]) * s[d] + ku[d]
        return carry

    lax.fori_loop(0, n, scan, 0)

    o = o_s[0] + o_s[1]
    z = z_ref[...]
    o = o * lax.rsqrt(jnp.mean(o * o, axis=-1, keepdims=True) + LN_EPS) * nw_ref[...]
    y_ref[...] = (o * (z * jax.nn.sigmoid(z))).astype(y_ref.dtype)
    sf_ref[...] = st_s[...]


def gdn_branch(proj, row_blk0, nb, L, gates, conv_w, s0, norm_w):
    n = L // CHUNK
    seq = lambda col: pl.BlockSpec((L, LANES), lambda b, h, col=col: (row_blk0 + b, col // LANES + h))
    scr = lambda *s: pltpu.VMEM(s, F32)
    return pl.pallas_call(
        functools.partial(_gdn_kernel, L=L),
        grid=(nb, H_A),
        in_specs=[seq(COL_QA), seq(COL_KA), seq(COL_VA), seq(COL_ZA),
                  pl.BlockSpec((3, None, 3, LANES), lambda b, h: (0, h, 0, 0)),
                  pl.BlockSpec((None, None, n, 8, LANES), lambda b, h: (b, h, 0, 0, 0)),
                  pl.BlockSpec((None, 2, None, DK_A, DV_A), lambda b, h: (b, 0, h, 0, 0)),
                  pl.BlockSpec((1, LANES), lambda b, h: (0, 0))],
        out_specs=[pl.BlockSpec((L, LANES), lambda b, h: (b, h)),
                   pl.BlockSpec((None, 2, None, DK_A, DV_A), lambda b, h: (b, 0, h, 0, 0))],
        out_shape=[jax.ShapeDtypeStruct((nb * L, V_A), BF16),
                   jax.ShapeDtypeStruct((nb, 2, H_A, DK_A, DV_A), F32)],
        scratch_shapes=[scr(L, LANES), scr(L, LANES), scr(L, LANES),
                        scr(2, n, 2 * CHUNK, LANES), scr(2, L, LANES), scr(2, L, LANES),
                        scr(2, n, CHUNK, CHUNK), scr(2, L, LANES), scr(2, DK_A, DV_A)],
        compiler_params=pltpu.CompilerParams(
            dimension_semantics=("parallel", "parallel"), vmem_limit_bytes=VMEM_LIMIT_BYTES),
        name="gdn_branch",
    )(proj, proj, proj, proj, conv_w, gates, s0, norm_w)


def _mlstm_kernel(q_ref, k_ref, v_ref, ob_ref, g_ref, c0_ref, n0_ref, m0_ref, nw_ref,
                  y_ref, cf_ref, nf_ref, mf_ref, h_s, c_s, n_s, m_s, *, L):
    n = L // CHUNK
    eye, incl, _ = _chunk_masks()
    c_s[...] = c0_ref[...]
    n_s[...] = n0_ref[...]
    m_s[...] = m0_ref[...]

    def body(grp, carry):
        ch = []
        for j in range(ML_GROUP):
            i = grp * ML_GROUP + j
            for d in range(2):
                c = i if d == 0 else n - 1 - i
                r0 = pl.multiple_of(c * CHUNK, CHUNK)
                gates = g_ref[c]
                ch.append(dict(d=d, r0=r0, qc=q_ref[pl.ds(r0, CHUNK), :],
                               kc=k_ref[pl.ds(r0, CHUNK), :] * (DQK_B ** -0.5), vc=v_ref[pl.ds(r0, CHUNK), :],
                               b_row=gates[2 * d:2 * d + 1, :CHUNK], i_row=gates[2 * d + 1:2 * d + 2, :CHUNK],
                               bl_row=gates[4 + d:5 + d, :], wm_row=gates[6 + d:7 + d, :]))
        for x in ch:
            x['qk'] = _mm1(_nt, x['qc'], x['kc'])
        for x in ch:
            x['b_col'] = _to_col(eye, x['b_row'])
            i_col = _to_col(eye, x['i_row'])
            dlog = jnp.where(incl[x['d']], x['b_col'] - x['b_row'] + x['i_row'], -jnp.inf)
            x['dmax'] = jnp.max(dlog, axis=1, keepdims=True)
            x['pw'] = jnp.exp(dlog - x['dmax']) * x['qk']
            x['ewk'] = jnp.exp(x['bl_row'][:, :1] - x['b_col'] + i_col - x['wm_row'][:, :1]) * x['kc']
        for x in ch:
            x['intra_num'] = _mm1(_nn, x['pw'], x['vc'])
            x['dc'] = _mm1(_tn, x['ewk'], x['vc'])
        for x in ch:
            x['intra_den'] = jnp.sum(x['pw'], axis=1, keepdims=True)
            x['dn'] = jnp.sum(x['ewk'], axis=0, keepdims=True)
        for x in ch:
            d, r0 = x['d'], x['r0']
            cm, nm, mm = c_s[d], n_s[d], m_s[d]
            alog = x['b_col'] + mm
            mt = jnp.maximum(alog, x['dmax'])
            wi = jnp.exp(alog - mt)[:, :1]
            wa = jnp.exp(x['dmax'] - mt)[:, :1]
            num = wi * _mm1(_nn, x['qc'], cm) + wa * x['intra_num']
            den = wi * jnp.sum(x['qc'] * nm, axis=1, keepdims=True) + wa * x['intra_den']
            h_s[d, pl.ds(r0, CHUNK), :] = num / jnp.maximum(jnp.abs(den), jnp.exp(-mt[:, :1]))

            m_new = jnp.maximum(x['bl_row'] + mm, x['wm_row'])
            a = jnp.exp(x['bl_row'] + mm - m_new)
            e = jnp.exp(x['wm_row'] - m_new)
            c_s[d] = a[:, :1] * cm + e[:, :1] * x['dc']
            n_s[d] = a * nm + e * x['dn']
            m_s[d] = m_new
        return carry

    lax.fori_loop(0, n // ML_GROUP, body, 0)

    h = h_s[0] + h_s[1]
    h = h * lax.rsqrt(jnp.mean(h * h, axis=-1, keepdims=True) + LN_EPS) * nw_ref[...]
    y_ref[...] = (h * jax.nn.sigmoid(ob_ref[...])).astype(y_ref.dtype)
    cf_ref[...] = c_s[...]
    nf_ref[...] = n_s[...]
    mf_ref[...] = m_s[...]


def mlstm_branch(proj, row_blk0, nb, L, gates, c0, n0, m0, norm_w):
    n = L // CHUNK
    seq = lambda col, w: pl.BlockSpec((L, w), lambda b, h, col=col, w=w: (row_blk0 + b, col // w + h))
    st = lambda *s: pl.BlockSpec((None, 2, None) + s, lambda b, h: (b, 0, h, 0, 0))
    scr = lambda *s: pltpu.VMEM(s, F32)
    return pl.pallas_call(
        functools.partial(_mlstm_kernel, L=L),
        grid=(nb, H_B),
        in_specs=[seq(COL_QB, DQK_B), seq(COL_KB, DQK_B), seq(COL_VB, DV_B), seq(COL_OB, DV_B),
                  pl.BlockSpec((None, None, n, 8, LANES), lambda b, h: (b, h, 0, 0, 0)),
                  st(DQK_B, DV_B), st(1, DQK_B), st(1, LANES),
                  pl.BlockSpec((1, DV_B), lambda b, h: (0, 0))],
        out_specs=[pl.BlockSpec((L, DV_B), lambda b, h: (b, h)),
                   st(DQK_B, DV_B), st(1, DQK_B), st(1, LANES)],
        out_shape=[jax.ShapeDtypeStruct((nb * L, V_B), BF16),
                   jax.ShapeDtypeStruct((nb, 2, H_B, DQK_B, DV_B), F32),
                   jax.ShapeDtypeStruct((nb, 2, H_B, 1, DQK_B), F32),
                   jax.ShapeDtypeStruct((nb, 2, H_B, 1, LANES), F32)],
        scratch_shapes=[scr(2, L, DV_B), scr(2, DQK_B, DV_B), scr(2, 1, DQK_B), scr(2, 1, LANES)],
        compiler_params=pltpu.CompilerParams(
            dimension_semantics=("parallel", "parallel"), vmem_limit_bytes=VMEM_LIMIT_BYTES),
        name="mlstm_branch",
    )(proj, proj, proj, proj, gates, c0, n0, m0, norm_w)


def _ln_stats(x):
    mu = jnp.mean(x, -1, keepdims=True)
    var = jnp.mean(jnp.square(x - mu), -1, keepdims=True)
    return (x - mu) * lax.rsqrt(var + LN_EPS)


def centred_conv(x, w):
    k = w.shape[0]
    p = k // 2
    L = x.shape[1]
    xp = jnp.pad(x, ((0, 0), (p, p), (0, 0)))
    y = xp[:, 0:L] * w[0]
    for j in range(1, k):
        y = y + xp[:, j:j + L] * w[j]
    return y


def hyena_filters(L, w1, b1, w2, b2, w3, b3, freq, rate):
    pos = jnp.arange(L, dtype=F32)
    t = pos / L
    ang = (2.0 * math.pi) * t[:, None] * jnp.arange(1, HY_BANDS + 1, dtype=F32)
    feats = jnp.concatenate([t[:, None], jnp.sin(ang), jnp.cos(ang)], -1)
    hp = lax.Precision.HIGHEST
    z = jnp.sin(freq[0] * (jnp.dot(feats, w1, precision=hp) + b1))
    z = jnp.sin(freq[1] * (jnp.dot(z, w2, precision=hp) + b2))
    filt = jnp.dot(z, w3, precision=hp) + b3
    lag = jnp.abs(pos - L // 2) / L
    filt = filt * jnp.exp(-lag[:, None] * rate)
    return filt.reshape(L, HY_ORDER, HY_CH)


def dft_size(L):
    n = 3 * L // 2
    return n if (n // 2) % LANES == 0 else 2 * L


def dft_matrices(L):
    N = dft_size(L)
    F = N // 2
    k = jnp.arange(F, dtype=jnp.int32)
    t = jnp.arange(L, dtype=jnp.int32)
    w = 2.0 * math.pi / N
    ang = w * ((k[:, None] * t[None, :]) % N).astype(F32)
    alt = (1 - 2 * (t % 2)).astype(F32)
    cf = jnp.cos(ang)
    sf = jnp.where(k[:, None] == 0, alt[None, :], -jnp.sin(ang))
    tt = t + L // 2
    ang_i = w * ((tt[:, None] * k[None, :]) % N).astype(F32)
    alt_i = (1 - 2 * (tt % 2)).astype(F32)
    ci = jnp.where(k[None, :] == 0, 1.0 / N, (2.0 / N) * jnp.cos(ang_i))
    si = jnp.where(k[None, :] == 0, alt_i[:, None] / N, (-2.0 / N) * jnp.sin(ang_i))
    return cf, sf, ci, si


def _mm3_kernel(x_ref, w_ref, o_ref):
    o_ref[...] = _mm3(_nn, x_ref[...], w_ref[...])


def pmm3(x, w):
    m, k = x.shape
    _, n = w.shape
    tm = _pick(m, (512, 256, 128))
    tn = _pick(n, (512, 256, 128))
    return pl.pallas_call(
        _mm3_kernel,
        grid=(m // tm, n // tn),
        in_specs=[pl.BlockSpec((tm, k), lambda i, j: (i, 0)),
                  pl.BlockSpec((k, tn), lambda i, j: (0, j))],
        out_specs=pl.BlockSpec((tm, tn), lambda i, j: (i, j)),
        out_shape=jax.ShapeDtypeStruct((m, n), F32),
        compiler_params=pltpu.CompilerParams(
            dimension_semantics=("parallel", "parallel"), vmem_limit_bytes=VMEM_LIMIT_BYTES),
        name="pmm3",
    )(x, w)


def _hyena_kernel(z_ref, un_ref, cwz_ref, cwu_ref, skip_ref, hr_ref, hi_ref, cf_ref, sf_ref, ci_ref, si_ref,
                  o_ref, z_s, zb_s, acc_s, *, L, ft, conv_z):
    f = pl.program_id(2)
    row = lax.broadcasted_iota(jnp.int32, (L, z_ref.shape[1]), 0)

    def conv3(x, w):
        x_prev = jnp.where(row == 0, 0.0, pltpu.roll(x, 1, 0))
        x_next = jnp.where(row == L - 1, 0.0, pltpu.roll(x, L - 1, 0))
        return x_prev * w[0:1] + x * w[1:2] + x_next * w[2:3]

    @pl.when(f == 0)
    def _():
        z = z_ref[...]
        if conv_z:
            z = conv3(z, cwz_ref[...])
        z_s[...] = z
        zb_s[...] = z.astype(BF16)

    zb = zb_s[...]
    zr = _nn(cf_ref[...], zb)
    zi = _nn(sf_ref[...], zb)
    hr = hr_ref[...]
    hi = hi_ref[...]
    packed = (f * ft + lax.broadcasted_iota(jnp.int32, zr.shape, 0)) == 0
    zihi = zi * hi
    yr = zr * hr - jnp.where(packed, 0.0, zihi)
    yi = jnp.where(packed, zihi, zr * hi + zi * hr)
    part = _nn(ci_ref[...], yr.astype(BF16)) + _nn(si_ref[...], yi.astype(BF16))

    @pl.when(f == 0)
    def _():
        acc_s[...] = part

    @pl.when(f > 0)
    def _():
        acc_s[...] += part

    @pl.when(f == pl.num_programs(2) - 1)
    def _():
        un = conv3(un_ref[...], cwu_ref[...])
        o_ref[...] = (un * (acc_s[...] + skip_ref[...] * z_s[...])).astype(o_ref.dtype)


def hyena_order(z_arr, z_blk0, z_col, conv_z, proj, row_blk0, un_col, nb, L, conv_w, cwz_col, skip,
                hr, hi, h_col, mats, out_dtype=F32):
    cf, sf, ci, si = mats
    F = cf.shape[0]
    cb = HY_CH if L * HY_CH * 4 <= (1 << 20) else 256
    ft = _pick(F, (512, 256))
    seq = lambda blk0, col: pl.BlockSpec((L, cb), lambda b, j, f: (blk0 + b, col // cb + j))
    return pl.pallas_call(
        functools.partial(_hyena_kernel, L=L, ft=ft, conv_z=conv_z),
        grid=(nb, HY_CH // cb, F // ft),
        in_specs=[seq(z_blk0, z_col), seq(row_blk0, un_col),
                  pl.BlockSpec((3, cb), lambda b, j, f: (0, cwz_col // cb + j)),
                  pl.BlockSpec((3, cb), lambda b, j, f: (0, (un_col - COL_HY) // cb + j)),
                  pl.BlockSpec((1, cb), lambda b, j, f: (0, j)),
                  pl.BlockSpec((ft, cb), lambda b, j, f: (f, h_col // cb + j)),
                  pl.BlockSpec((ft, cb), lambda b, j, f: (f, h_col // cb + j)),
                  pl.BlockSpec((ft, L), lambda b, j, f: (f, 0)),
                  pl.BlockSpec((ft, L), lambda b, j, f: (f, 0)),
                  pl.BlockSpec((L, ft), lambda b, j, f: (0, f)),
                  pl.BlockSpec((L, ft), lambda b, j, f: (0, f))],
        out_specs=pl.BlockSpec((L, cb), lambda b, j, f: (b, j)),
        out_shape=jax.ShapeDtypeStruct((nb * L, HY_CH), out_dtype),
        scratch_shapes=[pltpu.VMEM((L, cb), F32), pltpu.VMEM((L, cb), BF16), pltpu.VMEM((L, cb), F32)],
        compiler_params=pltpu.CompilerParams(
            dimension_semantics=("parallel", "parallel", "arbitrary"), vmem_limit_bytes=VMEM_LIMIT_BYTES),
        name="hyena_order",
    )(z_arr, proj, conv_w, conv_w, skip, hr, hi, cf, sf, ci, si)


def grid_pos_embed(n_tokens):
    rows = n_tokens // GRID_W
    quarter = D_MODEL // 4
    omega = 1.0 / (10000.0 ** (jnp.arange(quarter, dtype=F32) / quarter))
    r = jnp.broadcast_to(jnp.arange(rows, dtype=F32)[:, None, None] * omega, (rows, GRID_W, quarter))
    cl = jnp.broadcast_to(jnp.arange(GRID_W, dtype=F32)[None, :, None] * omega, (rows, GRID_W, quarter))
    pe = jnp.concatenate([jnp.sin(r), jnp.cos(r), jnp.sin(cl), jnp.cos(cl)], -1)
    return pe.reshape(rows * GRID_W, D_MODEL)


def _chunk_rows(t):
    b, L, h = t.shape
    return t.reshape(b, L // CHUNK, CHUNK, h).transpose(0, 3, 1, 2)


def _suffix_cumsum(t):
    return jnp.flip(jnp.cumsum(jnp.flip(t, -1), -1), -1)


def _pack_gate_rows(per_token, per_chunk):
    rows = [jnp.pad(t, ((0, 0), (0, 0), (0, 0), (0, LANES - CHUNK))) for t in per_token]
    rows += [jnp.broadcast_to(t[..., None], t.shape + (LANES,)) for t in per_chunk]
    return jnp.stack(rows, axis=3)


def gdn_gate_rows(beta_p, alpha_p, a_log, dt_bias):
    b, L, _ = beta_p.shape
    beta = jax.nn.sigmoid(beta_p).reshape(b, L, 2, H_A)
    loga = -jnp.exp(a_log) * jax.nn.softplus(alpha_p.reshape(b, L, 2, H_A) + dt_bias)
    g_f = jnp.cumsum(_chunk_rows(loga[:, :, 0]), -1)
    g_b = _suffix_cumsum(_chunk_rows(loga[:, :, 1]))
    zero = jnp.zeros_like(g_f[..., 0])
    return _pack_gate_rows([g_f, _chunk_rows(beta[:, :, 0]), g_b, _chunk_rows(beta[:, :, 1])],
                           [g_f[..., -1], g_b[..., 0], zero, zero])


def mlstm_gate_rows(ig_p, fg_p, i_bias, f_bias):
    b, L, _ = ig_p.shape
    ig = ig_p.reshape(b, L, 2, H_B) + i_bias
    lf = jax.nn.log_sigmoid(fg_p.reshape(b, L, 2, H_B) + f_bias)
    ig_f, ig_b = _chunk_rows(ig[:, :, 0]), _chunk_rows(ig[:, :, 1])
    bc_f = jnp.cumsum(_chunk_rows(lf[:, :, 0]), -1)
    bc_b = _suffix_cumsum(_chunk_rows(lf[:, :, 1]))
    bl_f, bl_b = bc_f[..., -1], bc_b[..., 0]
    wm_f = jnp.max(bl_f[..., None] - bc_f + ig_f, -1)
    wm_b = jnp.max(bl_b[..., None] - bc_b + ig_b, -1)
    return _pack_gate_rows([bc_f, ig_f, bc_b, ig_b], [bl_f, bl_b, wm_f, wm_b])


def seq_mixers(proj_big, proj_small, row0, nb, L, lp, mats, s_gdn, s_c, s_n, s_m):
    rows = slice(row0, row0 + nb * L)
    small = proj_small[rows].reshape(nb, L, LANES)
    blk0 = row0 // L

    conv_w = lp['gdn_conv'].reshape(3, 3, H_A, DK_A).transpose(1, 2, 0, 3)
    gates = gdn_gate_rows(small[..., 0:2 * H_A], small[..., 2 * H_A:4 * H_A], lp['gdn_a_log'], lp['gdn_dt_bias'])
    y_a, s_gdn_new = gdn_branch(proj_big, blk0, nb, L, gates, conv_w, s_gdn, lp['gdn_norm'].reshape(1, DV_A))

    o = 4 * H_A
    gates = mlstm_gate_rows(small[..., o:o + 2 * H_B], small[..., o + 2 * H_B:o + 4 * H_B],
                            lp['ml_i_bias'], lp['ml_f_bias'])
    m0 = jnp.broadcast_to(s_m[..., None, None], s_m.shape + (1, LANES))
    y_b, c_new, n_new, m_new = mlstm_branch(proj_big, blk0, nb, L, gates, s_c, s_n[..., None, :], m0,
                                            lp['ml_norm'].reshape(1, DV_B))

    filt = hyena_filters(L, lp['hy_w1'], lp['hy_b1'], lp['hy_w2'], lp['hy_b2'], lp['hy_w3'], lp['hy_b3'],
                         lp['hy_freq'], lp['hy_rate']).reshape(L, HY_ORDER * HY_CH)
    mats32, mats16 = mats[L]
    hr, hi = pmm3(mats32[0], filt), pmm3(mats32[1], filt)
    z_arr, z_blk0, z_col = proj_big, blk0, COL_HY
    for order in range(HY_ORDER):
        z_arr = hyena_order(z_arr, z_blk0, z_col, order == 0, proj_big, blk0, COL_HY + (order + 1) * HY_CH, nb, L,
                            lp['hy_conv'], 0, lp['hy_skip'][order:order + 1], hr, hi, order * HY_CH, mats16,
                            out_dtype=BF16 if order == HY_ORDER - 1 else F32)
        z_blk0, z_col = 0, 0
    return (y_a, y_b, z_arr), (s_gdn_new, c_new, n_new[..., 0, :], m_new[..., 0, 0])


def expert_choice_ffn(hfs, w_router, b_router, w_gate, w_up, w_down):
    routes = []
    for hf in hfs:
        cap = CAP_FACTOR * hf.shape[0] // N_EXPERTS
        aff = jax.nn.softmax(jnp.dot(hf, w_router, precision=lax.Precision.HIGHEST) + b_router, axis=-1)
        routes.append(lax.top_k(aff.T, cap))
    xe = jnp.concatenate([hf.astype(BF16)[idx] for hf, (_, idx) in zip(hfs, routes)], axis=1)
    gate = jnp.concatenate([g for g, _ in routes], axis=1)[..., None]
    ye = expert_down(expert_hidden(xe, w_gate, w_up), w_down, gate)
    outs, off = [], 0
    for hf, (_, idx) in zip(hfs, routes):
        cap = idx.shape[1]
        contrib = ye[:, off:off + cap].reshape(-1, hf.shape[1])
        outs.append(jnp.zeros_like(hf).at[idx.reshape(-1)].add(contrib))
        off += cap
    return outs


_IN_SIZES = (QK_A, QK_A, V_A, V_A, 2 * H_A, 2 * H_A, QK_B, QK_B, V_B, V_B, 2 * H_B, 2 * H_B,
             3 * HY_CH, D_MODEL, D_MODEL, D_MODEL)
_IN_OFFS = tuple(int(v) for v in np.concatenate([[0], np.cumsum(_IN_SIZES)]))
_SMALL_PARTS = (4, 5, 10, 11)


def trunk_layer(x_sets, mods, lp, mats, states):
    hs = []
    for x, mod in zip(x_sets, mods):
        shift1, scale1 = mod[..., :D_MODEL], mod[..., D_MODEL:2 * D_MODEL]
        hs.append((_ln_stats(x) * (1.0 + scale1) + shift1).reshape(-1, D_MODEL))
    h_all = jnp.concatenate(hs, 0)

    big_cols = [p for p in range(len(_IN_SIZES)) if p not in _SMALL_PARTS]
    w_in = lp['w_in']
    w_big = jnp.concatenate([w_in[:, _IN_OFFS[p]:_IN_OFFS[p + 1]] for p in big_cols], -1)
    w_small = jnp.concatenate([w_in[:, _IN_OFFS[p]:_IN_OFFS[p + 1]] for p in _SMALL_PARTS], -1)
    w_small = jnp.pad(w_small, ((0, 0), (0, LANES - w_small.shape[1])))
    proj_big = pmm(h_all, w_big)
    proj_small = pmm(h_all, w_small)

    mixed, new_states, row = [], [], 0
    for x, st in zip(x_sets, states):
        b, L, _ = x.shape
        outs, st_new = seq_mixers(proj_big, proj_small, row, b, L, lp, mats, *st)
        row += b * L
        mixed.append(outs)
        new_states.append(st_new)

    y_a, y_b, y_c = (jnp.concatenate(ts, 0) for ts in zip(*mixed))
    merged = branch_merge(y_a, y_b, y_c, lp['w_br_a'], lp['w_br_b'], lp['w_br_c'], proj_big)
    y_all = pmm(merged, lp['w_out'])

    x1s, h2s, row = [], [], 0
    for x, mod in zip(x_sets, mods):
        b, L, _ = x.shape
        gate1 = mod[..., 2 * D_MODEL:3 * D_MODEL]
        shift2, scale2 = mod[..., 3 * D_MODEL:4 * D_MODEL], mod[..., 4 * D_MODEL:5 * D_MODEL]
        y = y_all[row:row + b * L].reshape(b, L, D_MODEL)
        row += b * L
        x1 = _ln_stats(DN_ALPHA * x + (1.0 + gate1) * y) * lp['ln1_g'] + lp['ln1_b']
        x1s.append(x1)
        h2s.append((_ln_stats(x1) * (1.0 + scale2) + shift2).reshape(b * L, D_MODEL))
    y2s = expert_choice_ffn(h2s, lp['w_router'], lp['b_router'], lp['w_gate'], lp['w_up'], lp['w_down'])
    out_sets = []
    for x1, y2, mod in zip(x1s, y2s, mods):
        gate2 = mod[..., 5 * D_MODEL:6 * D_MODEL]
        out_sets.append(_ln_stats(DN_ALPHA * x1 + (1.0 + gate2) * y2.reshape(x1.shape)) * lp['ln2_g'] + lp['ln2_b'])
    return out_sets, new_states


def kernel(x_prompt, x_sample, state_gdn, state_mlstm_c, state_mlstm_n, state_mlstm_m, c, c_ctx, w_ada, b_ada, w_in, gdn_conv, gdn_a_log, gdn_dt_bias, gdn_norm, ml_i_bias, ml_f_bias, ml_norm, hy_conv, hy_w1, hy_b1, hy_w2, hy_b2, hy_w3, hy_b3, hy_freq, hy_rate, hy_skip, w_br_a, w_br_b, w_br_c, w_out, ln1_g, ln1_b, ln2_g, ln2_b, w_router, b_router, w_gate, w_up, w_down):
    bp = x_prompt.shape[0]
    ls = x_sample.shape[1]
    xp = x_prompt
    xs = x_sample + grid_pos_embed(ls)[None]
    zero_states = (jnp.zeros((bp, 2, H_A, DK_A, DV_A), F32), jnp.zeros((bp, 2, H_B, DQK_B, DV_B), F32),
                   jnp.zeros((bp, 2, H_B, DQK_B), F32), jnp.zeros((bp, 2, H_B), F32))
    cond = jax.nn.silu(jnp.concatenate([c_ctx[None], c], 0))
    cond = jnp.pad(cond, ((0, 16 - cond.shape[0]), (0, 0)))
    ctx_states = []
    mats = {}
    for L in (x_prompt.shape[1], ls):
        m32 = dft_matrices(L)
        mats[L] = (m32, tuple(m.astype(BF16) for m in m32))
    stacked = dict(w_in=w_in, gdn_conv=gdn_conv, gdn_a_log=gdn_a_log, gdn_dt_bias=gdn_dt_bias, gdn_norm=gdn_norm,
                   ml_i_bias=ml_i_bias, ml_f_bias=ml_f_bias, ml_norm=ml_norm, hy_conv=hy_conv, hy_w1=hy_w1,
                   hy_b1=hy_b1, hy_w2=hy_w2, hy_b2=hy_b2, hy_w3=hy_w3, hy_b3=hy_b3, hy_freq=hy_freq,
                   hy_rate=hy_rate, hy_skip=hy_skip, w_br_a=w_br_a, w_br_b=w_br_b, w_br_c=w_br_c, w_out=w_out,
                   ln1_g=ln1_g, ln1_b=ln1_b, ln2_g=ln2_g, ln2_b=ln2_b, w_router=w_router, b_router=b_router,
                   w_gate=w_gate, w_up=w_up, w_down=w_down)
    for l in range(DEPTH):
        lp = {name: t[l] for name, t in stacked.items()}
        mod = pmm(cond, w_ada[l]) + b_ada[l]
        mod_ctx = mod[0].reshape(1, 1, 6 * D_MODEL)
        mod_lat = mod[1:1 + c.shape[0]][:, None, :]
        lat_states = (state_gdn[:, l], state_mlstm_c[:, l], state_mlstm_n[:, l], state_mlstm_m[:, l])
        (xp, xs), (st_ctx, _) = trunk_layer([xp, xs], [mod_ctx, mod_lat], lp, mats, [zero_states, lat_states])
        ctx_states.append(st_ctx)
    outs = tuple(jnp.stack([st[i] for st in ctx_states], 1) for i in range(4))
    return (xp, xs) + outs
```

```python
import functools
import math

import jax
import jax.numpy as jnp
import numpy as np
from jax import lax
from jax.experimental import pallas as pl
from jax.experimental.pallas import tpu as pltpu

D_MODEL = 2048
DEPTH = 4
GRID_W = 64
CHUNK = 64
H_A = 8
DK_A = 128
DV_A = 128
QK_A = H_A * DK_A
V_A = H_A * DV_A
H_B = 4
DQK_B = 128
DV_B = 256
QK_B = H_B * DQK_B
V_B = H_B * DV_B
HY_CH = 1024
HY_ORDER = 2
HY_BANDS = 16
N_EXPERTS = 16
D_EXPERT = 1024
CAP_FACTOR = 2
DN_ALPHA = (2 * DEPTH) ** 0.25
LN_EPS = 1e-6
F32 = jnp.float32
BF16 = jnp.bfloat16

LANES = 128
INTRA_GROUP = 4
ML_GROUP = 4
VMEM_LIMIT_BYTES = 48 * 1024 * 1024

COL_QA, COL_KA, COL_VA, COL_ZA = 0, QK_A, 2 * QK_A, 2 * QK_A + V_A
COL_QB = 2 * QK_A + 2 * V_A
COL_KB = COL_QB + QK_B
COL_VB = COL_KB + QK_B
COL_OB = COL_VB + V_B
COL_HY = COL_OB + V_B
COL_GA = COL_HY + 3 * HY_CH
COL_GB = COL_GA + D_MODEL
COL_GC = COL_GB + D_MODEL
N_BIG = COL_GC + D_MODEL


def _mm_kernel(x_ref, w_ref, o_ref):
    o_ref[...] = jnp.dot(x_ref[...], w_ref[...], preferred_element_type=F32).astype(o_ref.dtype)


def _pick(n, pref):
    for t in pref:
        if n % t == 0:
            return t
    return n


def pmm(x, w, out_dtype=F32):
    m, k = x.shape
    _, n = w.shape
    tm = _pick(m, (1024, 512, 256, 128, 16))
    tn = _pick(n, (512, 256, 128))
    return pl.pallas_call(
        _mm_kernel,
        grid=(m // tm, n // tn),
        in_specs=[pl.BlockSpec((tm, k), lambda i, j: (i, 0)),
                  pl.BlockSpec((k, tn), lambda i, j: (0, j))],
        out_specs=pl.BlockSpec((tm, tn), lambda i, j: (i, j)),
        out_shape=jax.ShapeDtypeStruct((m, n), out_dtype),
        compiler_params=pltpu.CompilerParams(
            dimension_semantics=("parallel", "parallel"), vmem_limit_bytes=VMEM_LIMIT_BYTES),
        name="pmm",
    )(x.astype(BF16), w.astype(BF16))


def pbmm(x, w, out_dtype=F32):
    e, m, k = x.shape
    _, _, n = w.shape
    tm = _pick(m, (1024, 512, 256, 128, 8))
    tn = _pick(n, (512, 256, 128))
    return pl.pallas_call(
        _mm_kernel,
        grid=(e, m // tm, n // tn),
        in_specs=[pl.BlockSpec((None, tm, k), lambda b, i, j: (b, i, 0)),
                  pl.BlockSpec((None, k, tn), lambda b, i, j: (b, 0, j))],
        out_specs=pl.BlockSpec((None, tm, tn), lambda b, i, j: (b, i, j)),
        out_shape=jax.ShapeDtypeStruct((e, m, n), out_dtype),
        compiler_params=pltpu.CompilerParams(
            dimension_semantics=("parallel", "parallel", "parallel"), vmem_limit_bytes=VMEM_LIMIT_BYTES),
        name="pbmm",
    )(x.astype(BF16), w.astype(BF16))


def _merge_kernel(ya_ref, yb_ref, yc_ref, wa_ref, wb_ref, wc_ref, ga_ref, gb_ref, gc_ref, o_ref):
    acc = jax.nn.sigmoid(ga_ref[...]) * jnp.dot(ya_ref[...], wa_ref[...], preferred_element_type=F32)
    acc = acc + jax.nn.sigmoid(gb_ref[...]) * jnp.dot(yb_ref[...], wb_ref[...], preferred_element_type=F32)
    acc = acc + jax.nn.sigmoid(gc_ref[...]) * jnp.dot(yc_ref[...], wc_ref[...], preferred_element_type=F32)
    o_ref[...] = acc.astype(o_ref.dtype)


def branch_merge(y_a, y_b, y_c, w_a, w_b, w_c, proj):
    m = y_a.shape[0]
    tm, tn = _pick(m, (1024, 512, 256)), 512
    y_spec = lambda k: pl.BlockSpec((tm, k), lambda i, j: (i, 0))
    w_spec = lambda k: pl.BlockSpec((k, tn), lambda i, j: (0, j))
    g_spec = lambda col: pl.BlockSpec((tm, tn), lambda i, j, col=col: (i, col // tn + j))
    return pl.pallas_call(
        _merge_kernel,
        grid=(m // tm, D_MODEL // tn),
        in_specs=[y_spec(V_A), y_spec(V_B), y_spec(HY_CH), w_spec(V_A), w_spec(V_B), w_spec(HY_CH),
                  g_spec(COL_GA), g_spec(COL_GB), g_spec(COL_GC)],
        out_specs=pl.BlockSpec((tm, tn), lambda i, j: (i, j)),
        out_shape=jax.ShapeDtypeStruct((m, D_MODEL), BF16),
        compiler_params=pltpu.CompilerParams(
            dimension_semantics=("parallel", "parallel"), vmem_limit_bytes=VMEM_LIMIT_BYTES),
        name="branch_merge",
    )(y_a, y_b, y_c, w_a.astype(BF16), w_b.astype(BF16), w_c.astype(BF16), proj, proj, proj)


def _expert_hidden_kernel(x_ref, wg_ref, wu_ref, o_ref, wg_s, wu_s):
    @pl.when(pl.program_id(2) == 0)
    def _():
        wg_s[...] = wg_ref[...].astype(BF16)
        wu_s[...] = wu_ref[...].astype(BF16)

    x = x_ref[...]
    g = jnp.dot(x, wg_s[...], preferred_element_type=F32)
    u = jnp.dot(x, wu_s[...], preferred_element_type=F32)
    o_ref[...] = (g * jax.nn.sigmoid(g) * u).astype(o_ref.dtype)


def expert_hidden(xe, w_gate, w_up):
    e, m, k = xe.shape
    f = w_gate.shape[2]
    tm, tn = _pick(m, (1280, 1024, 512, 256)), 256
    return pl.pallas_call(
        _expert_hidden_kernel,
        grid=(e, f // tn, m // tm),
        in_specs=[pl.BlockSpec((None, tm, k), lambda b, j, i: (b, i, 0)),
                  pl.BlockSpec((None, k, tn), lambda b, j, i: (b, 0, j)),
                  pl.BlockSpec((None, k, tn), lambda b, j, i: (b, 0, j))],
        out_specs=pl.BlockSpec((None, tm, tn), lambda b, j, i: (b, i, j)),
        out_shape=jax.ShapeDtypeStruct((e, m, f), BF16),
        scratch_shapes=[pltpu.VMEM((k, tn), BF16), pltpu.VMEM((k, tn), BF16)],
        compiler_params=pltpu.CompilerParams(
            dimension_semantics=("parallel", "parallel", "arbitrary"), vmem_limit_bytes=VMEM_LIMIT_BYTES),
        name="expert_hidden",
    )(xe, w_gate, w_up)


def _expert_down_kernel(h_ref, w_ref, g_ref, o_ref, w_s):
    @pl.when(pl.program_id(2) == 0)
    def _():
        w_s[...] = w_ref[...].astype(BF16)

    o_ref[...] = jnp.dot(h_ref[...], w_s[...], preferred_element_type=F32) * g_ref[...]


def expert_down(hid, w_down, gate):
    e, m, k = hid.shape
    n = w_down.shape[2]
    tm, tn = _pick(m, (1280, 1024, 512, 256)), 512
    return pl.pallas_call(
        _expert_down_kernel,
        grid=(e, n // tn, m // tm),
        in_specs=[pl.BlockSpec((None, tm, k), lambda b, j, i: (b, i, 0)),
                  pl.BlockSpec((None, k, tn), lambda b, j, i: (b, 0, j)),
                  pl.BlockSpec((None, tm, 1), lambda b, j, i: (b, i, 0))],
        out_specs=pl.BlockSpec((None, tm, tn), lambda b, j, i: (b, i, j)),
        out_shape=jax.ShapeDtypeStruct((e, m, n), F32),
        scratch_shapes=[pltpu.VMEM((k, tn), BF16)],
        compiler_params=pltpu.CompilerParams(
            dimension_semantics=("parallel", "parallel", "arbitrary"), vmem_limit_bytes=VMEM_LIMIT_BYTES),
        name="expert_down",
    )(hid, w_down, gate)


ROW_TILE = 256


def _ln(v):
    mu = jnp.mean(v, axis=-1, keepdims=True)
    d = v - mu
    return d * lax.rsqrt(jnp.mean(d * d, axis=-1, keepdims=True) + LN_EPS)


def _mix_out_kernel(m_ref, w_ref, x_ref, g1_ref, sh2_ref, sc2_ref, lng_ref, lnb_ref, wr_ref, br_ref,
                    x1_ref, h2_ref, lg_ref):
    y = jnp.dot(m_ref[...], w_ref[...], preferred_element_type=F32)
    x1 = _ln(DN_ALPHA * x_ref[...] + (1.0 + g1_ref[...]) * y) * lng_ref[...] + lnb_ref[...]
    x1_ref[...] = x1
    h2 = _ln(x1) * (1.0 + sc2_ref[...]) + sh2_ref[...]
    h2_ref[...] = h2.astype(h2_ref.dtype)
    lg_ref[...] = _mm3(_nn, h2, wr_ref[...]) + br_ref[...]


def mix_out(merged, row0, x, mod, L, w_out, ln_g, ln_b, w_router, b_router):
    n = x.shape[0]
    nm = mod.shape[0]
    mod4 = mod.reshape(nm, 6, 1, D_MODEL)
    per_seq = L // ROW_TILE
    mod_spec = lambda part: pl.BlockSpec(
        (None, None, 1, D_MODEL), lambda i, part=part: (0 if nm == 1 else i // per_seq, part, 0, 0))
    vec = pl.BlockSpec((1, D_MODEL), lambda i: (0, 0))
    wr = jnp.pad(w_router, ((0, 0), (0, LANES - N_EXPERTS)))
    br = jnp.pad(b_router, (0, LANES - N_EXPERTS)).reshape(1, LANES)
    return pl.pallas_call(
        _mix_out_kernel,
        grid=(n // ROW_TILE,),
        in_specs=[pl.BlockSpec((ROW_TILE, D_MODEL), lambda i: (row0 // ROW_TILE + i, 0)),
                  pl.BlockSpec((D_MODEL, D_MODEL), lambda i: (0, 0)),
                  pl.BlockSpec((ROW_TILE, D_MODEL), lambda i: (i, 0)),
                  mod_spec(2), mod_spec(3), mod_spec(4), vec, vec,
                  pl.BlockSpec((D_MODEL, LANES), lambda i: (0, 0)),
                  pl.BlockSpec((1, LANES), lambda i: (0, 0))],
        out_specs=[pl.BlockSpec((ROW_TILE, D_MODEL), lambda i: (i, 0)),
                   pl.BlockSpec((ROW_TILE, D_MODEL), lambda i: (i, 0)),
                   pl.BlockSpec((ROW_TILE, LANES), lambda i: (i, 0))],
        out_shape=[jax.ShapeDtypeStruct((n, D_MODEL), F32), jax.ShapeDtypeStruct((n, D_MODEL), BF16),
                   jax.ShapeDtypeStruct((n, LANES), F32)],
        compiler_params=pltpu.CompilerParams(dimension_semantics=("parallel",), vmem_limit_bytes=VMEM_LIMIT_BYTES),
        name="mix_out",
    )(merged, w_out, x, mod4, mod4, mod4, ln_g.reshape(1, -1), ln_b.reshape(1, -1), wr, br)


def _post_norm_kernel(x_ref, y_ref, gate_ref, lng_ref, lnb_ref, o_ref):
    o_ref[...] = _ln(DN_ALPHA * x_ref[...] + (1.0 + gate_ref[...]) * y_ref[...]) * lng_ref[...] + lnb_ref[...]


def post_norm(x, y, mod, part, L, ln_g, ln_b):
    n = x.shape[0]
    nm = mod.shape[0]
    per_seq = L // ROW_TILE
    rows = pl.BlockSpec((ROW_TILE, D_MODEL), lambda i: (i, 0))
    vec = pl.BlockSpec((1, D_MODEL), lambda i: (0, 0))
    return pl.pallas_call(
        _post_norm_kernel,
        grid=(n // ROW_TILE,),
        in_specs=[rows, rows,
                  pl.BlockSpec((None, None, 1, D_MODEL), lambda i: (0 if nm == 1 else i // per_seq, part, 0, 0)),
                  vec, vec],
        out_specs=rows,
        out_shape=jax.ShapeDtypeStruct((n, D_MODEL), F32),
        compiler_params=pltpu.CompilerParams(dimension_semantics=("parallel",), vmem_limit_bytes=VMEM_LIMIT_BYTES),
        name="post_norm",
    )(x, y, mod.reshape(nm, 6, 1, D_MODEL), ln_g.reshape(1, -1), ln_b.reshape(1, -1))


def _nn(a, b):
    return jnp.dot(a, b, preferred_element_type=F32)


def _nt(a, b):
    return lax.dot_general(a, b, (((1,), (1,)), ((), ())), preferred_element_type=F32)


def _tn(a, b):
    return lax.dot_general(a, b, (((0,), (0,)), ((), ())), preferred_element_type=F32)


def _split(a):
    hi = a.astype(BF16)
    return hi, (a - hi.astype(F32)).astype(BF16)


def _mm1(f, a, b):
    return f(a.astype(BF16), b.astype(BF16))


def _mm3(f, a, b):
    ah, al = _split(a)
    bh, bl = _split(b)
    return f(ah, bh) + (f(ah, bl) + f(al, bh))


def _chunk_masks():
    ri = lax.broadcasted_iota(jnp.int32, (CHUNK, CHUNK), 0)
    ci = lax.broadcasted_iota(jnp.int32, (CHUNK, CHUNK), 1)
    eye = ri == ci
    incl = (ri >= ci, ri <= ci)
    strict = (ri > ci, ri < ci)
    return eye, incl, strict


def _to_col(eye, row):
    return jnp.sum(jnp.where(eye, row, 0.0), axis=1, keepdims=True)


GC_BETA, GC_ALPHA, GC_IG, GC_FG, GC_END = 0, 2 * H_A, 4 * H_A, 4 * H_A + 2 * H_B, 4 * H_A + 4 * H_B


def _softplus(x):
    return jnp.maximum(x, 0.0) + jnp.log(1.0 + jnp.exp(-jnp.abs(x)))


def _gate_kernel(x_ref, p_ref, tab_ref, scal_ref, val_s, *, L):
    n = L // CHUNK
    x = x_ref[...]
    p = p_ref[...]
    lane = lax.broadcasted_iota(jnp.int32, (L, LANES), 1)
    beta = jax.nn.sigmoid(x)
    loga = p[0:1] * _softplus(x + p[1:2])
    ig = x + p[2:3]
    lf = -_softplus(-(x + p[3:4]))
    val_s[...] = jnp.where(lane < GC_ALPHA, beta, jnp.where(lane < GC_IG, loga, jnp.where(lane < GC_FG, ig, lf)))

    ri = lax.broadcasted_iota(jnp.int32, (CHUNK, CHUNK), 0)
    ci = lax.broadcasted_iota(jnp.int32, (CHUNK, CHUNK), 1)
    lower = jnp.where(ri >= ci, 1.0, 0.0).astype(BF16)
    upper = jnp.where(ri <= ci, 1.0, 0.0).astype(BF16)
    ident = jnp.where(ri == ci, 1.0, 0.0).astype(BF16)
    cl = lax.broadcasted_iota(jnp.int32, (CHUNK, LANES), 1)
    cumulative = ((cl >= GC_ALPHA) & (cl < GC_IG)) | ((cl >= GC_FG) & (cl < GC_END))
    backward = ((cl >= GC_ALPHA + H_A) & (cl < GC_IG)) | (cl >= GC_FG + H_B)

    def split3(v):
        h1 = v.astype(BF16)
        r1 = v - h1.astype(F32)
        h2 = r1.astype(BF16)
        return h1, h2, (r1 - h2.astype(F32)).astype(BF16)

    def chunk(c, carry):
        v = val_s[pl.ds(pl.multiple_of(c * CHUNK, CHUNK), CHUNK), :]
        parts = split3(v)
        pre = sum(_nn(lower, h) for h in parts)
        suf = sum(_nn(upper, h) for h in parts)
        out = jnp.where(cumulative, jnp.where(backward, suf, pre), v)
        tot = jnp.where(backward[0:1], suf[0:1], pre[CHUNK - 1:CHUNK])
        wlog = tot - out + pltpu.roll(out, GC_FG - GC_IG, 1)
        wmax = jnp.max(wlog, axis=0, keepdims=True)
        tab_ref[c] = sum(_tn(h, ident) for h in split3(out))
        scal_ref[c] = jnp.concatenate([tot, wmax, jnp.zeros((6, LANES), F32)], axis=0)
        return carry

    lax.fori_loop(0, n, chunk, 0)


def gate_tables(proj_small, row_blk0, nb, L, lp):
    n = L // CHUNK
    rows = [jnp.pad(v.reshape(-1), (off, LANES - off - v.size)) for v, off in (
        (-jnp.exp(lp['gdn_a_log']), GC_ALPHA), (lp['gdn_dt_bias'], GC_ALPHA),
        (lp['ml_i_bias'], GC_IG), (lp['ml_f_bias'], GC_FG))]
    params = jnp.stack(rows + [jnp.zeros((LANES,), F32)] * 4)
    return pl.pallas_call(
        functools.partial(_gate_kernel, L=L),
        grid=(nb,),
        in_specs=[pl.BlockSpec((L, LANES), lambda b: (row_blk0 + b, 0)),
                  pl.BlockSpec((8, LANES), lambda b: (0, 0))],
        out_specs=[pl.BlockSpec((None, n, LANES, CHUNK), lambda b: (b, 0, 0, 0)),
                   pl.BlockSpec((None, n, 8, LANES), lambda b: (b, 0, 0, 0))],
        out_shape=[jax.ShapeDtypeStruct((nb, n, LANES, CHUNK), F32),
                   jax.ShapeDtypeStruct((nb, n, 8, LANES), F32)],
        scratch_shapes=[pltpu.VMEM((L, LANES), F32)],
        compiler_params=pltpu.CompilerParams(dimension_semantics=("parallel",), vmem_limit_bytes=VMEM_LIMIT_BYTES),
        name="gate_tables",
    )(proj_small, params)


def _gate_row(tab_ref, c, lane):
    return tab_ref[c, pl.ds(lane, 1), :]


def _gate_scalar(scal_ref, c, row, lane):
    v = scal_ref[c][row:row + 1, :]
    li = lax.broadcasted_iota(jnp.int32, (1, LANES), 1)
    return jnp.sum(jnp.where(li == lane, v, 0.0), axis=1, keepdims=True)


def _gdn_kernel(q_ref, k_ref, v_ref, z_ref, cw_ref, tab_ref, scal_ref, s0_ref, nw_ref, y_ref, sf_ref,
                qs, ks, vs, wq_s, u_s, kd_s, p_s, o_s, st_s, *, L):
    n = L // CHUNK
    head = pl.program_id(1)
    beta_lane = [GC_BETA + d * H_A + head for d in range(2)]
    g_lane = [GC_ALPHA + d * H_A + head for d in range(2)]
    row = lax.broadcasted_iota(jnp.int32, (L, LANES), 0)

    def conv_silu(x_ref, part):
        x = x_ref[...]
        w = cw_ref[part]
        x_prev = jnp.where(row == 0, 0.0, pltpu.roll(x, 1, 0))
        x_next = jnp.where(row == L - 1, 0.0, pltpu.roll(x, L - 1, 0))
        y = x_prev * w[0:1] + x * w[1:2] + x_next * w[2:3]
        return y * jax.nn.sigmoid(y)

    def l2n(x):
        return x * lax.rsqrt(jnp.sum(x * x, axis=-1, keepdims=True) + LN_EPS)

    qs[...] = l2n(conv_silu(q_ref, 0)) * (DK_A ** -0.5)
    ks[...] = l2n(conv_silu(k_ref, 1))
    vs[...] = conv_silu(v_ref, 2)

    eye, incl, strict = _chunk_masks()
    eye_f = jnp.where(eye, 1.0, 0.0)

    def intra(grp, carry):
        chains = []
        for j in range(INTRA_GROUP):
            c = grp * INTRA_GROUP + j
            r0 = pl.multiple_of(c * CHUNK, CHUNK)
            qc = qs[pl.ds(r0, CHUNK), :]
            kc = ks[pl.ds(r0, CHUNK), :]
            vc = vs[pl.ds(r0, CHUNK), :]
            kk = _mm3(_nt, kc, kc)
            qk = _mm1(_nt, qc, kc)
            for d in range(2):
                g_row = _gate_row(tab_ref, c, g_lane[d])
                b_row = _gate_row(tab_ref, c, beta_lane[d])
                gl_row = _gate_scalar(scal_ref, c, 0, g_lane[d])
                g_col = _to_col(eye, g_row)
                b_col = _to_col(eye, b_row)
                dec = jnp.exp(jnp.where(incl[d], g_col - g_row, -jnp.inf))
                lmat = b_col * kk * jnp.where(strict[d], dec, 0.0)
                kd_s[d, pl.ds(r0, CHUNK), :] = jnp.exp(gl_row - g_col) * kc
                wq_s[d, c, CHUNK:, :] = jnp.exp(g_col) * qc
                p_s[d, c] = qk * dec
                chains.append(dict(d=d, c=c, r0=r0, pw=lmat, tinv=eye_f - lmat,
                                   rhs_w=(b_col * jnp.exp(g_col)) * kc, rhs_u=b_col * vc))
        for _ in range(int(math.log2(CHUNK)) - 1):
            for ch in chains:
                ch['pw'] = _mm3(_nn, ch['pw'], ch['pw'])
            for ch in chains:
                ch['tinv'] = ch['tinv'] + _mm3(_nn, ch['tinv'], ch['pw'])
        for ch in chains:
            d, r0 = ch['d'], ch['r0']
            wq_s[d, ch['c'], :CHUNK, :] = _mm3(_nn, ch['tinv'], ch['rhs_w'])
            u_s[d, pl.ds(r0, CHUNK), :] = _mm3(_nn, ch['tinv'], ch['rhs_u'])
        return carry

    lax.fori_loop(0, n // INTRA_GROUP, intra, 0)

    st_s[...] = s0_ref[...]

    def scan(i, carry):
        cs = (i, n - 1 - i)
        r0s = [pl.multiple_of(c * CHUNK, CHUNK) for c in cs]
        s = [st_s[d] for d in range(2)]
        ws = [_mm1(_nn, wq_s[d, cs[d]], s[d]) for d in range(2)]
        uc = [u_s[d, pl.ds(r0s[d], CHUNK), :] - ws[d][:CHUNK] for d in range(2)]
        pu = [_mm1(_nn, p_s[d, cs[d]], uc[d]) for d in range(2)]
        ku = [_mm1(_tn, kd_s[d, pl.ds(r0s[d], CHUNK), :], uc[d]) for d in range(2)]
        for d in range(2):
            o_s[d, pl.ds(r0s[d], CHUNK), :] = ws[d][CHUNK:] + pu[d]
            st_s[d] = jnp.exp(_gate_scalar(scal_ref, cs[d], 0, g_lane[d])) * s[d] + ku[d]
        return carry

    lax.fori_loop(0, n, scan, 0)

    o = o_s[0] + o_s[1]
    z = z_ref[...]
    o = o * lax.rsqrt(jnp.mean(o * o, axis=-1, keepdims=True) + LN_EPS) * nw_ref[...]
    y_ref[...] = (o * (z * jax.nn.sigmoid(z))).astype(y_ref.dtype)
    sf_ref[...] = st_s[...]


def _gate_specs(n):
    return [pl.BlockSpec((None, n, LANES, CHUNK), lambda b, h: (b, 0, 0, 0)),
            pl.BlockSpec((None, n, 8, LANES), lambda b, h: (b, 0, 0, 0))]


def gdn_branch(proj, row_blk0, nb, L, gates, conv_w, s0, norm_w):
    n = L // CHUNK
    seq = lambda col: pl.BlockSpec((L, LANES), lambda b, h, col=col: (row_blk0 + b, col // LANES + h))
    scr = lambda *s: pltpu.VMEM(s, F32)
    return pl.pallas_call(
        functools.partial(_gdn_kernel, L=L),
        grid=(nb, H_A),
        in_specs=[seq(COL_QA), seq(COL_KA), seq(COL_VA), seq(COL_ZA),
                  pl.BlockSpec((3, None, 3, LANES), lambda b, h: (0, h, 0, 0)),
                  *_gate_specs(n),
                  pl.BlockSpec((None, 2, None, DK_A, DV_A), lambda b, h: (b, 0, h, 0, 0)),
                  pl.BlockSpec((1, LANES), lambda b, h: (0, 0))],
        out_specs=[pl.BlockSpec((L, LANES), lambda b, h: (b, h)),
                   pl.BlockSpec((None, 2, None, DK_A, DV_A), lambda b, h: (b, 0, h, 0, 0))],
        out_shape=[jax.ShapeDtypeStruct((nb * L, V_A), BF16),
                   jax.ShapeDtypeStruct((nb, 2, H_A, DK_A, DV_A), F32)],
        scratch_shapes=[scr(L, LANES), scr(L, LANES), scr(L, LANES),
                        scr(2, n, 2 * CHUNK, LANES), scr(2, L, LANES), scr(2, L, LANES),
                        scr(2, n, CHUNK, CHUNK), scr(2, L, LANES), scr(2, DK_A, DV_A)],
        compiler_params=pltpu.CompilerParams(
            dimension_semantics=("parallel", "parallel"), vmem_limit_bytes=VMEM_LIMIT_BYTES),
        name="gdn_branch",
    )(proj, proj, proj, proj, conv_w, *gates, s0, norm_w)


def _mlstm_kernel(q_ref, k_ref, v_ref, ob_ref, tab_ref, scal_ref, c0_ref, n0_ref, m0_ref, nw_ref,
                  y_ref, cf_ref, nf_ref, mf_ref, h_s, c_s, n_s, m_s, *, L):
    n = L // CHUNK
    head = pl.program_id(1)
    ig_lane = [GC_IG + d * H_B + head for d in range(2)]
    fg_lane = [GC_FG + d * H_B + head for d in range(2)]
    eye, incl, _ = _chunk_masks()
    c_s[...] = c0_ref[...]
    n_s[...] = n0_ref[...]
    m_s[...] = m0_ref[...]

    def body(grp, carry):
        ch = []
        for j in range(ML_GROUP):
            i = grp * ML_GROUP + j
            for d in range(2):
                c = i if d == 0 else n - 1 - i
                r0 = pl.multiple_of(c * CHUNK, CHUNK)
                ch.append(dict(d=d, r0=r0, qc=q_ref[pl.ds(r0, CHUNK), :],
                               kc=k_ref[pl.ds(r0, CHUNK), :] * (DQK_B ** -0.5), vc=v_ref[pl.ds(r0, CHUNK), :],
                               b_row=_gate_row(tab_ref, c, fg_lane[d]), i_row=_gate_row(tab_ref, c, ig_lane[d]),
                               bl_row=_gate_scalar(scal_ref, c, 0, fg_lane[d]),
                               wm_row=_gate_scalar(scal_ref, c, 1, fg_lane[d])))
        for x in ch:
            x['qk'] = _mm1(_nt, x['qc'], x['kc'])
        for x in ch:
            x['b_col'] = _to_col(eye, x['b_row'])
            i_col = _to_col(eye, x['i_row'])
            dlog = jnp.where(incl[x['d']], x['b_col'] - x['b_row'] + x['i_row'], -jnp.inf)
            x['dmax'] = jnp.max(dlog, axis=1, keepdims=True)
            x['pw'] = jnp.exp(dlog - x['dmax']) * x['qk']
            x['ewk'] = jnp.exp(x['bl_row'][:, :1] - x['b_col'] + i_col - x['wm_row'][:, :1]) * x['kc']
        for x in ch:
            x['intra_num'] = _mm1(_nn, x['pw'], x['vc'])
            x['dc'] = _mm1(_tn, x['ewk'], x['vc'])
        for x in ch:
            x['intra_den'] = jnp.sum(x['pw'], axis=1, keepdims=True)
            x['dn'] = jnp.sum(x['ewk'], axis=0, keepdims=True)
        for x in ch:
            d, r0 = x['d'], x['r0']
            cm, nm, mm = c_s[d], n_s[d], m_s[d]
            alog = x['b_col'] + mm
            mt = jnp.maximum(alog, x['dmax'])
            wi = jnp.exp(alog - mt)[:, :1]
            wa = jnp.exp(x['dmax'] - mt)[:, :1]
            num = wi * _mm1(_nn, x['qc'], cm) + wa * x['intra_num']
            den = wi * jnp.sum(x['qc'] * nm, axis=1, keepdims=True) + wa * x['intra_den']
            h_s[d, pl.ds(r0, CHUNK), :] = num / jnp.maximum(jnp.abs(den), jnp.exp(-mt[:, :1]))

            m_new = jnp.maximum(x['bl_row'] + mm, x['wm_row'])
            a = jnp.exp(x['bl_row'] + mm - m_new)
            e = jnp.exp(x['wm_row'] - m_new)
            c_s[d] = a[:, :1] * cm + e[:, :1] * x['dc']
            n_s[d] = a * nm + e * x['dn']
            m_s[d] = m_new
        return carry

    lax.fori_loop(0, n // ML_GROUP, body, 0)

    h = h_s[0] + h_s[1]
    h = h * lax.rsqrt(jnp.mean(h * h, axis=-1, keepdims=True) + LN_EPS) * nw_ref[...]
    y_ref[...] = (h * jax.nn.sigmoid(ob_ref[...])).astype(y_ref.dtype)
    cf_ref[...] = c_s[...]
    nf_ref[...] = n_s[...]
    mf_ref[...] = m_s[...]


def mlstm_branch(proj, row_blk0, nb, L, gates, c0, n0, m0, norm_w):
    n = L // CHUNK
    seq = lambda col, w: pl.BlockSpec((L, w), lambda b, h, col=col, w=w: (row_blk0 + b, col // w + h))
    st = lambda *s: pl.BlockSpec((None, 2, None) + s, lambda b, h: (b, 0, h, 0, 0))
    scr = lambda *s: pltpu.VMEM(s, F32)
    return pl.pallas_call(
        functools.partial(_mlstm_kernel, L=L),
        grid=(nb, H_B),
        in_specs=[seq(COL_QB, DQK_B), seq(COL_KB, DQK_B), seq(COL_VB, DV_B), seq(COL_OB, DV_B),
                  *_gate_specs(n),
                  st(DQK_B, DV_B), st(1, DQK_B), st(1, LANES),
                  pl.BlockSpec((1, DV_B), lambda b, h: (0, 0))],
        out_specs=[pl.BlockSpec((L, DV_B), lambda b, h: (b, h)),
                   st(DQK_B, DV_B), st(1, DQK_B), st(1, LANES)],
        out_shape=[jax.ShapeDtypeStruct((nb * L, V_B), BF16),
                   jax.ShapeDtypeStruct((nb, 2, H_B, DQK_B, DV_B), F32),
                   jax.ShapeDtypeStruct((nb, 2, H_B, 1, DQK_B), F32),
                   jax.ShapeDtypeStruct((nb, 2, H_B, 1, LANES), F32)],
        scratch_shapes=[scr(2, L, DV_B), scr(2, DQK_B, DV_B), scr(2, 1, DQK_B), scr(2, 1, LANES)],
        compiler_params=pltpu.CompilerParams(
            dimension_semantics=("parallel", "parallel"), vmem_limit_bytes=VMEM_LIMIT_BYTES),
        name="mlstm_branch",
    )(proj, proj, proj, proj, *gates, c0, n0, m0, norm_w)


def _ln_stats(x):
    mu = jnp.mean(x, -1, keepdims=True)
    var = jnp.mean(jnp.square(x - mu), -1, keepdims=True)
    return (x - mu) * lax.rsqrt(var + LN_EPS)


def centred_conv(x, w):
    k = w.shape[0]
    p = k // 2
    L = x.shape[1]
    xp = jnp.pad(x, ((0, 0), (p, p), (0, 0)))
    y = xp[:, 0:L] * w[0]
    for j in range(1, k):
        y = y + xp[:, j:j + L] * w[j]
    return y


def hyena_filters(L, w1, b1, w2, b2, w3, b3, freq, rate):
    pos = jnp.arange(L, dtype=F32)
    t = pos / L
    ang = (2.0 * math.pi) * t[:, None] * jnp.arange(1, HY_BANDS + 1, dtype=F32)
    feats = jnp.concatenate([t[:, None], jnp.sin(ang), jnp.cos(ang)], -1)
    hp = lax.Precision.HIGHEST
    z = jnp.sin(freq[0] * (jnp.dot(feats, w1, precision=hp) + b1))
    z = jnp.sin(freq[1] * (jnp.dot(z, w2, precision=hp) + b2))
    filt = jnp.dot(z, w3, precision=hp) + b3
    lag = jnp.abs(pos - L // 2) / L
    filt = filt * jnp.exp(-lag[:, None] * rate)
    return filt.reshape(L, HY_ORDER, HY_CH)


def dft_size(L):
    n = 3 * L // 2
    return n if (n // 2) % LANES == 0 else 2 * L


def dft_matrices(L):
    N = dft_size(L)
    F = N // 2
    k = jnp.arange(F, dtype=jnp.int32)
    t = jnp.arange(L, dtype=jnp.int32)
    w = 2.0 * math.pi / N

    def cos_sin(rows, cols):
        hi = jnp.arange(cols.shape[0] // LANES, dtype=jnp.int32) * LANES
        lo = jnp.arange(LANES, dtype=jnp.int32)
        a = w * ((rows[:, None] * hi[None, :]) % N).astype(F32)[:, :, None]
        b = w * ((rows[:, None] * lo[None, :]) % N).astype(F32)[:, None, :]
        ca, sa, cb, sb = jnp.cos(a), jnp.sin(a), jnp.cos(b), jnp.sin(b)
        shape = (rows.shape[0], cols.shape[0])
        return (ca * cb - sa * sb).reshape(shape), (sa * cb + ca * sb).reshape(shape)

    alt = (1 - 2 * (t % 2)).astype(F32)
    cf, sin_f = cos_sin(k, t)
    sf = jnp.where(k[:, None] == 0, alt[None, :], -sin_f)
    tt = t + L // 2
    alt_i = (1 - 2 * (tt % 2)).astype(F32)
    cos_i, sin_i = cos_sin(tt, k)
    ci = jnp.where(k[None, :] == 0, 1.0 / N, (2.0 / N) * cos_i)
    si = jnp.where(k[None, :] == 0, alt_i[:, None] / N, (-2.0 / N) * sin_i)
    return cf, sf, ci, si


def _mm3_kernel(x_ref, w_ref, o_ref):
    o_ref[...] = _mm3(_nn, x_ref[...], w_ref[...])


def pmm3(x, w):
    m, k = x.shape
    _, n = w.shape
    tm = _pick(m, (512, 256, 128))
    tn = _pick(n, (512, 256, 128))
    return pl.pallas_call(
        _mm3_kernel,
        grid=(m // tm, n // tn),
        in_specs=[pl.BlockSpec((tm, k), lambda i, j: (i, 0)),
                  pl.BlockSpec((k, tn), lambda i, j: (0, j))],
        out_specs=pl.BlockSpec((tm, tn), lambda i, j: (i, j)),
        out_shape=jax.ShapeDtypeStruct((m, n), F32),
        compiler_params=pltpu.CompilerParams(
            dimension_semantics=("parallel", "parallel"), vmem_limit_bytes=VMEM_LIMIT_BYTES),
        name="pmm3",
    )(x, w)


def _hyena_kernel(z_ref, un_ref, cwz_ref, cwu_ref, skip_ref, hr_ref, hi_ref, cf_ref, sf_ref, ci_ref, si_ref,
                  o_ref, z_s, zb_s, acc_s, *, L, ft, conv_z):
    f = pl.program_id(2)
    row = lax.broadcasted_iota(jnp.int32, (L, z_ref.shape[1]), 0)

    def conv3(x, w):
        x_prev = jnp.where(row == 0, 0.0, pltpu.roll(x, 1, 0))
        x_next = jnp.where(row == L - 1, 0.0, pltpu.roll(x, L - 1, 0))
        return x_prev * w[0:1] + x * w[1:2] + x_next * w[2:3]

    @pl.when(f == 0)
    def _():
        z = z_ref[...]
        if conv_z:
            z = conv3(z, cwz_ref[...])
        z_s[...] = z
        zb_s[...] = z.astype(BF16)

    zb = zb_s[...]
    zr = _nn(cf_ref[...], zb)
    zi = _nn(sf_ref[...], zb)
    hr = hr_ref[...]
    hi = hi_ref[...]
    packed = (f * ft + lax.broadcasted_iota(jnp.int32, zr.shape, 0)) == 0
    zihi = zi * hi
    yr = zr * hr - jnp.where(packed, 0.0, zihi)
    yi = jnp.where(packed, zihi, zr * hi + zi * hr)
    part = _nn(ci_ref[...], yr.astype(BF16)) + _nn(si_ref[...], yi.astype(BF16))

    @pl.when(f == 0)
    def _():
        acc_s[...] = part

    @pl.when(f > 0)
    def _():
        acc_s[...] += part

    @pl.when(f == pl.num_programs(2) - 1)
    def _():
        un = conv3(un_ref[...], cwu_ref[...])
        o_ref[...] = (un * (acc_s[...] + skip_ref[...] * z_s[...])).astype(o_ref.dtype)


def hyena_order(z_arr, z_blk0, z_col, conv_z, proj, row_blk0, un_col, nb, L, conv_w, cwz_col, skip,
                hr, hi, h_col, mats, out_dtype=F32):
    cf, sf, ci, si = mats
    F = cf.shape[0]
    cb = HY_CH if L * HY_CH * 4 <= (1 << 20) else 256
    ft = _pick(F, (512, 256))
    seq = lambda blk0, col: pl.BlockSpec((L, cb), lambda b, j, f: (blk0 + b, col // cb + j))
    return pl.pallas_call(
        functools.partial(_hyena_kernel, L=L, ft=ft, conv_z=conv_z),
        grid=(nb, HY_CH // cb, F // ft),
        in_specs=[seq(z_blk0, z_col), seq(row_blk0, un_col),
                  pl.BlockSpec((3, cb), lambda b, j, f: (0, cwz_col // cb + j)),
                  pl.BlockSpec((3, cb), lambda b, j, f: (0, (un_col - COL_HY) // cb + j)),
                  pl.BlockSpec((1, cb), lambda b, j, f: (0, j)),
                  pl.BlockSpec((ft, cb), lambda b, j, f: (f, h_col // cb + j)),
                  pl.BlockSpec((ft, cb), lambda b, j, f: (f, h_col // cb + j)),
                  pl.BlockSpec((ft, L), lambda b, j, f: (f, 0)),
                  pl.BlockSpec((ft, L), lambda b, j, f: (f, 0)),
                  pl.BlockSpec((L, ft), lambda b, j, f: (0, f)),
                  pl.BlockSpec((L, ft), lambda b, j, f: (0, f))],
        out_specs=pl.BlockSpec((L, cb), lambda b, j, f: (b, j)),
        out_shape=jax.ShapeDtypeStruct((nb * L, HY_CH), out_dtype),
        scratch_shapes=[pltpu.VMEM((L, cb), F32), pltpu.VMEM((L, cb), BF16), pltpu.VMEM((L, cb), F32)],
        compiler_params=pltpu.CompilerParams(
            dimension_semantics=("parallel", "parallel", "arbitrary"), vmem_limit_bytes=VMEM_LIMIT_BYTES),
        name="hyena_order",
    )(z_arr, proj, conv_w, conv_w, skip, hr, hi, cf, sf, ci, si)


def grid_pos_embed(n_tokens):
    rows = n_tokens // GRID_W
    quarter = D_MODEL // 4
    omega = 1.0 / (10000.0 ** (jnp.arange(quarter, dtype=F32) / quarter))
    r = jnp.arange(rows, dtype=F32)[:, None] * omega
    cl = jnp.arange(GRID_W, dtype=F32)[:, None] * omega
    full = lambda v, axis: jnp.broadcast_to(jnp.expand_dims(v, axis), (rows, GRID_W, quarter))
    pe = jnp.concatenate([full(jnp.sin(r), 1), full(jnp.cos(r), 1), full(jnp.sin(cl), 0), full(jnp.cos(cl), 0)], -1)
    return pe.reshape(rows * GRID_W, D_MODEL)


def seq_mixers(proj_big, proj_small, row0, nb, L, lp, mats, s_gdn, s_c, s_n, s_m):
    blk0 = row0 // L
    gates = gate_tables(proj_small, blk0, nb, L, lp)

    conv_w = lp['gdn_conv'].reshape(3, 3, H_A, DK_A).transpose(1, 2, 0, 3)
    y_a, s_gdn_new = gdn_branch(proj_big, blk0, nb, L, gates, conv_w, s_gdn, lp['gdn_norm'].reshape(1, DV_A))

    m0 = jnp.broadcast_to(s_m[..., None, None], s_m.shape + (1, LANES))
    y_b, c_new, n_new, m_new = mlstm_branch(proj_big, blk0, nb, L, gates, s_c, s_n[..., None, :], m0,
                                            lp['ml_norm'].reshape(1, DV_B))

    filt = hyena_filters(L, lp['hy_w1'], lp['hy_b1'], lp['hy_w2'], lp['hy_b2'], lp['hy_w3'], lp['hy_b3'],
                         lp['hy_freq'], lp['hy_rate']).reshape(L, HY_ORDER * HY_CH)
    mats32, mats16 = mats[L]
    hr, hi = pmm3(mats32[0], filt), pmm3(mats32[1], filt)
    z_arr, z_blk0, z_col = proj_big, blk0, COL_HY
    for order in range(HY_ORDER):
        z_arr = hyena_order(z_arr, z_blk0, z_col, order == 0, proj_big, blk0, COL_HY + (order + 1) * HY_CH, nb, L,
                            lp['hy_conv'], 0, lp['hy_skip'][order:order + 1], hr, hi, order * HY_CH, mats16,
                            out_dtype=BF16 if order == HY_ORDER - 1 else F32)
        z_blk0, z_col = 0, 0
    return (y_a, y_b, z_arr), (s_gdn_new, c_new, n_new[..., 0, :], m_new[..., 0, 0])


def expert_choice_ffn(hfs, logits, w_gate, w_up, w_down):
    routes = []
    for hf, lg in zip(hfs, logits):
        cap = CAP_FACTOR * hf.shape[0] // N_EXPERTS
        routes.append(lax.top_k(jax.nn.softmax(lg, axis=-1).T, cap))
    xe = jnp.concatenate([hf[idx] for hf, (_, idx) in zip(hfs, routes)], axis=1)
    gate = jnp.concatenate([g for g, _ in routes], axis=1)[..., None]
    ye = expert_down(expert_hidden(xe, w_gate, w_up), w_down, gate)
    outs, off = [], 0
    for hf, (_, idx) in zip(hfs, routes):
        cap = idx.shape[1]
        contrib = ye[:, off:off + cap].reshape(-1, hf.shape[1])
        outs.append(jnp.zeros(hf.shape, F32).at[idx.reshape(-1)].add(contrib))
        off += cap
    return outs


_IN_SIZES = (QK_A, QK_A, V_A, V_A, 2 * H_A, 2 * H_A, QK_B, QK_B, V_B, V_B, 2 * H_B, 2 * H_B,
             3 * HY_CH, D_MODEL, D_MODEL, D_MODEL)
_IN_OFFS = tuple(int(v) for v in np.concatenate([[0], np.cumsum(_IN_SIZES)]))
_SMALL_PARTS = (4, 5, 10, 11)


def trunk_layer(x_sets, mods, lp, mats, states):
    hs = []
    for x, mod in zip(x_sets, mods):
        shift1, scale1 = mod[..., :D_MODEL], mod[..., D_MODEL:2 * D_MODEL]
        hs.append((_ln_stats(x) * (1.0 + scale1) + shift1).reshape(-1, D_MODEL))
    h_all = jnp.concatenate(hs, 0)

    big_cols = [p for p in range(len(_IN_SIZES)) if p not in _SMALL_PARTS]
    w_in = lp['w_in']
    w_big = jnp.concatenate([w_in[:, _IN_OFFS[p]:_IN_OFFS[p + 1]] for p in big_cols], -1)
    w_small = jnp.concatenate([w_in[:, _IN_OFFS[p]:_IN_OFFS[p + 1]] for p in _SMALL_PARTS], -1)
    w_small = jnp.pad(w_small, ((0, 0), (0, LANES - w_small.shape[1])))
    proj_big = pmm(h_all, w_big)
    proj_small = pmm(h_all, w_small)

    mixed, new_states, row = [], [], 0
    for x, st in zip(x_sets, states):
        b, L, _ = x.shape
        outs, st_new = seq_mixers(proj_big, proj_small, row, b, L, lp, mats, *st)
        row += b * L
        mixed.append(outs)
        new_states.append(st_new)

    y_a, y_b, y_c = (jnp.concatenate(ts, 0) for ts in zip(*mixed))
    merged = branch_merge(y_a, y_b, y_c, lp['w_br_a'], lp['w_br_b'], lp['w_br_c'], proj_big)
    w_out = lp['w_out'].astype(BF16)

    x1s, h2s, logits, row = [], [], [], 0
    for x, mod in zip(x_sets, mods):
        b, L, _ = x.shape
        x1, h2, lg = mix_out(merged, row, x.reshape(b * L, D_MODEL), mod, L, w_out, lp['ln1_g'], lp['ln1_b'],
                             lp['w_router'], lp['b_router'])
        row += b * L
        x1s.append(x1)
        h2s.append(h2)
        logits.append(lg[:, :N_EXPERTS])
    y2s = expert_choice_ffn(h2s, logits, lp['w_gate'], lp['w_up'], lp['w_down'])
    out_sets = []
    for x, x1, y2, mod in zip(x_sets, x1s, y2s, mods):
        out = post_norm(x1, y2, mod, 5, x.shape[1], lp['ln2_g'], lp['ln2_b'])
        out_sets.append(out.reshape(x.shape))
    return out_sets, new_states


def kernel(x_prompt, x_sample, state_gdn, state_mlstm_c, state_mlstm_n, state_mlstm_m, c, c_ctx, w_ada, b_ada, w_in, gdn_conv, gdn_a_log, gdn_dt_bias, gdn_norm, ml_i_bias, ml_f_bias, ml_norm, hy_conv, hy_w1, hy_b1, hy_w2, hy_b2, hy_w3, hy_b3, hy_freq, hy_rate, hy_skip, w_br_a, w_br_b, w_br_c, w_out, ln1_g, ln1_b, ln2_g, ln2_b, w_router, b_router, w_gate, w_up, w_down):
    bp = x_prompt.shape[0]
    ls = x_sample.shape[1]
    xp = x_prompt
    xs = x_sample + grid_pos_embed(ls)[None]
    zero_states = (jnp.zeros((bp, 2, H_A, DK_A, DV_A), F32), jnp.zeros((bp, 2, H_B, DQK_B, DV_B), F32),
                   jnp.zeros((bp, 2, H_B, DQK_B), F32), jnp.zeros((bp, 2, H_B), F32))
    cond = jax.nn.silu(jnp.concatenate([c_ctx[None], c], 0))
    cond = jnp.pad(cond, ((0, 16 - cond.shape[0]), (0, 0)))
    ctx_states = []
    mats = {}
    for L in (x_prompt.shape[1], ls):
        m32 = dft_matrices(L)
        mats[L] = (m32, tuple(m.astype(BF16) for m in m32))
    stacked = dict(w_in=w_in, gdn_conv=gdn_conv, gdn_a_log=gdn_a_log, gdn_dt_bias=gdn_dt_bias, gdn_norm=gdn_norm,
                   ml_i_bias=ml_i_bias, ml_f_bias=ml_f_bias, ml_norm=ml_norm, hy_conv=hy_conv, hy_w1=hy_w1,
                   hy_b1=hy_b1, hy_w2=hy_w2, hy_b2=hy_b2, hy_w3=hy_w3, hy_b3=hy_b3, hy_freq=hy_freq,
                   hy_rate=hy_rate, hy_skip=hy_skip, w_br_a=w_br_a, w_br_b=w_br_b, w_br_c=w_br_c, w_out=w_out,
                   ln1_g=ln1_g, ln1_b=ln1_b, ln2_g=ln2_g, ln2_b=ln2_b, w_router=w_router, b_router=b_router,
                   w_gate=w_gate, w_up=w_up, w_down=w_down)
    for l in range(DEPTH):
        lp = {name: t[l] for name, t in stacked.items()}
        mod = pmm(cond, w_ada[l]) + b_ada[l]
        mod_ctx = mod[0].reshape(1, 1, 6 * D_MODEL)
        mod_lat = mod[1:1 + c.shape[0]][:, None, :]
        lat_states = (state_gdn[:, l], state_mlstm_c[:, l], state_mlstm_n[:, l], state_mlstm_m[:, l])
        (xp, xs), (st_ctx, _) = trunk_layer([xp, xs], [mod_ctx, mod_lat], lp, mats, [zero_states, lat_states])
        ctx_states.append(st_ctx)
    outs = tuple(jnp.stack([st[i] for st in ctx_states], 1) for i in range(4))
    return (xp, xs) + outs
```

```python
import functools
import math

import jax
import jax.numpy as jnp
import numpy as np
from jax import lax
from jax.experimental import pallas as pl
from jax.experimental.pallas import tpu as pltpu

D_MODEL = 2048
DEPTH = 4
GRID_W = 64
CHUNK = 64
H_A = 8
DK_A = 128
DV_A = 128
QK_A = H_A * DK_A
V_A = H_A * DV_A
H_B = 4
DQK_B = 128
DV_B = 256
QK_B = H_B * DQK_B
V_B = H_B * DV_B
HY_CH = 1024
HY_ORDER = 2
HY_BANDS = 16
N_EXPERTS = 16
D_EXPERT = 1024
CAP_FACTOR = 2
DN_ALPHA = (2 * DEPTH) ** 0.25
LN_EPS = 1e-6
F32 = jnp.float32
BF16 = jnp.bfloat16

LANES = 128
INTRA_GROUP = 4
ML_GROUP = 4
VMEM_LIMIT_BYTES = 48 * 1024 * 1024

COL_QA, COL_KA, COL_VA, COL_ZA = 0, QK_A, 2 * QK_A, 2 * QK_A + V_A
COL_QB = 2 * QK_A + 2 * V_A
COL_KB = COL_QB + QK_B
COL_VB = COL_KB + QK_B
COL_OB = COL_VB + V_B
COL_HY = COL_OB + V_B
COL_GA = COL_HY + 3 * HY_CH
COL_GB = COL_GA + D_MODEL
COL_GC = COL_GB + D_MODEL
N_BIG = COL_GC + D_MODEL


def _mm_kernel(x_ref, w_ref, o_ref):
    o_ref[...] = jnp.dot(x_ref[...], w_ref[...], preferred_element_type=F32).astype(o_ref.dtype)


def _pick(n, pref):
    for t in pref:
        if n % t == 0:
            return t
    return n


def pmm(x, w, out_dtype=F32):
    m, k = x.shape
    _, n = w.shape
    tm = _pick(m, (1024, 512, 256, 128, 16))
    tn = _pick(n, (512, 256, 128))
    return pl.pallas_call(
        _mm_kernel,
        grid=(m // tm, n // tn),
        in_specs=[pl.BlockSpec((tm, k), lambda i, j: (i, 0)),
                  pl.BlockSpec((k, tn), lambda i, j: (0, j))],
        out_specs=pl.BlockSpec((tm, tn), lambda i, j: (i, j)),
        out_shape=jax.ShapeDtypeStruct((m, n), out_dtype),
        compiler_params=pltpu.CompilerParams(
            dimension_semantics=("parallel", "parallel"), vmem_limit_bytes=VMEM_LIMIT_BYTES),
        name="pmm",
    )(x.astype(BF16), w.astype(BF16))


def pbmm(x, w, out_dtype=F32):
    e, m, k = x.shape
    _, _, n = w.shape
    tm = _pick(m, (1024, 512, 256, 128, 8))
    tn = _pick(n, (512, 256, 128))
    return pl.pallas_call(
        _mm_kernel,
        grid=(e, m // tm, n // tn),
        in_specs=[pl.BlockSpec((None, tm, k), lambda b, i, j: (b, i, 0)),
                  pl.BlockSpec((None, k, tn), lambda b, i, j: (b, 0, j))],
        out_specs=pl.BlockSpec((None, tm, tn), lambda b, i, j: (b, i, j)),
        out_shape=jax.ShapeDtypeStruct((e, m, n), out_dtype),
        compiler_params=pltpu.CompilerParams(
            dimension_semantics=("parallel", "parallel", "parallel"), vmem_limit_bytes=VMEM_LIMIT_BYTES),
        name="pbmm",
    )(x.astype(BF16), w.astype(BF16))


def _merge_kernel(ya_ref, yb_ref, yc_ref, wa_ref, wb_ref, wc_ref, ga_ref, gb_ref, gc_ref, o_ref):
    acc = jax.nn.sigmoid(ga_ref[...]) * jnp.dot(ya_ref[...], wa_ref[...], preferred_element_type=F32)
    acc = acc + jax.nn.sigmoid(gb_ref[...]) * jnp.dot(yb_ref[...], wb_ref[...], preferred_element_type=F32)
    acc = acc + jax.nn.sigmoid(gc_ref[...]) * jnp.dot(yc_ref[...], wc_ref[...], preferred_element_type=F32)
    o_ref[...] = acc.astype(o_ref.dtype)


def branch_merge(y_a, y_b, y_c, w_a, w_b, w_c, proj):
    m = y_a.shape[0]
    tm, tn = _pick(m, (1024, 512, 256)), 512
    y_spec = lambda k: pl.BlockSpec((tm, k), lambda i, j: (i, 0))
    w_spec = lambda k: pl.BlockSpec((k, tn), lambda i, j: (0, j))
    g_spec = lambda col: pl.BlockSpec((tm, tn), lambda i, j, col=col: (i, col // tn + j))
    return pl.pallas_call(
        _merge_kernel,
        grid=(m // tm, D_MODEL // tn),
        in_specs=[y_spec(V_A), y_spec(V_B), y_spec(HY_CH), w_spec(V_A), w_spec(V_B), w_spec(HY_CH),
                  g_spec(COL_GA), g_spec(COL_GB), g_spec(COL_GC)],
        out_specs=pl.BlockSpec((tm, tn), lambda i, j: (i, j)),
        out_shape=jax.ShapeDtypeStruct((m, D_MODEL), BF16),
        compiler_params=pltpu.CompilerParams(
            dimension_semantics=("parallel", "parallel"), vmem_limit_bytes=VMEM_LIMIT_BYTES),
        name="branch_merge",
    )(y_a, y_b, y_c, w_a.astype(BF16), w_b.astype(BF16), w_c.astype(BF16), proj, proj, proj)


def _expert_hidden_kernel(x_ref, wg_ref, wu_ref, o_ref, wg_s, wu_s):
    @pl.when(pl.program_id(2) == 0)
    def _():
        wg_s[...] = wg_ref[...].astype(BF16)
        wu_s[...] = wu_ref[...].astype(BF16)

    x = x_ref[...]
    g = jnp.dot(x, wg_s[...], preferred_element_type=F32)
    u = jnp.dot(x, wu_s[...], preferred_element_type=F32)
    o_ref[...] = (g * jax.nn.sigmoid(g) * u).astype(o_ref.dtype)


def expert_hidden(xe, w_gate, w_up):
    e, m, k = xe.shape
    f = w_gate.shape[2]
    tm, tn = _pick(m, (1280, 1024, 512, 256)), 256
    return pl.pallas_call(
        _expert_hidden_kernel,
        grid=(e, f // tn, m // tm),
        in_specs=[pl.BlockSpec((None, tm, k), lambda b, j, i: (b, i, 0)),
                  pl.BlockSpec((None, k, tn), lambda b, j, i: (b, 0, j)),
                  pl.BlockSpec((None, k, tn), lambda b, j, i: (b, 0, j))],
        out_specs=pl.BlockSpec((None, tm, tn), lambda b, j, i: (b, i, j)),
        out_shape=jax.ShapeDtypeStruct((e, m, f), BF16),
        scratch_shapes=[pltpu.VMEM((k, tn), BF16), pltpu.VMEM((k, tn), BF16)],
        compiler_params=pltpu.CompilerParams(
            dimension_semantics=("parallel", "parallel", "arbitrary"), vmem_limit_bytes=VMEM_LIMIT_BYTES),
        name="expert_hidden",
    )(xe, w_gate, w_up)


def _expert_down_kernel(h_ref, w_ref, g_ref, o_ref, w_s):
    @pl.when(pl.program_id(2) == 0)
    def _():
        w_s[...] = w_ref[...].astype(BF16)

    o_ref[...] = jnp.dot(h_ref[...], w_s[...], preferred_element_type=F32) * g_ref[...]


def expert_down(hid, w_down, gate):
    e, m, k = hid.shape
    n = w_down.shape[2]
    tm, tn = _pick(m, (1280, 1024, 512, 256)), 512
    return pl.pallas_call(
        _expert_down_kernel,
        grid=(e, n // tn, m // tm),
        in_specs=[pl.BlockSpec((None, tm, k), lambda b, j, i: (b, i, 0)),
                  pl.BlockSpec((None, k, tn), lambda b, j, i: (b, 0, j)),
                  pl.BlockSpec((None, tm, 1), lambda b, j, i: (b, i, 0))],
        out_specs=pl.BlockSpec((None, tm, tn), lambda b, j, i: (b, i, j)),
        out_shape=jax.ShapeDtypeStruct((e, m, n), F32),
        scratch_shapes=[pltpu.VMEM((k, tn), BF16)],
        compiler_params=pltpu.CompilerParams(
            dimension_semantics=("parallel", "parallel", "arbitrary"), vmem_limit_bytes=VMEM_LIMIT_BYTES),
        name="expert_down",
    )(hid, w_down, gate)


ROW_TILE = 256


def _ln(v):
    mu = jnp.mean(v, axis=-1, keepdims=True)
    d = v - mu
    return d * lax.rsqrt(jnp.mean(d * d, axis=-1, keepdims=True) + LN_EPS)


def _mix_out_kernel(m_ref, w_ref, x_ref, g1_ref, sh2_ref, sc2_ref, lng_ref, lnb_ref, wr_ref, br_ref,
                    x1_ref, h2_ref, lg_ref):
    y = jnp.dot(m_ref[...], w_ref[...], preferred_element_type=F32)
    x1 = _ln(DN_ALPHA * x_ref[...] + (1.0 + g1_ref[...]) * y) * lng_ref[...] + lnb_ref[...]
    x1_ref[...] = x1
    h2 = _ln(x1) * (1.0 + sc2_ref[...]) + sh2_ref[...]
    h2_ref[...] = h2.astype(h2_ref.dtype)
    lg_ref[...] = _mm3(_nn, h2, wr_ref[...]) + br_ref[...]


def mix_out(merged, row0, x, mod, L, w_out, ln_g, ln_b, w_router, b_router):
    n = x.shape[0]
    nm = mod.shape[0]
    mod4 = mod.reshape(nm, 6, 1, D_MODEL)
    per_seq = L // ROW_TILE
    mod_spec = lambda part: pl.BlockSpec(
        (None, None, 1, D_MODEL), lambda i, part=part: (0 if nm == 1 else i // per_seq, part, 0, 0))
    vec = pl.BlockSpec((1, D_MODEL), lambda i: (0, 0))
    wr = jnp.pad(w_router, ((0, 0), (0, LANES - N_EXPERTS)))
    br = jnp.pad(b_router, (0, LANES - N_EXPERTS)).reshape(1, LANES)
    return pl.pallas_call(
        _mix_out_kernel,
        grid=(n // ROW_TILE,),
        in_specs=[pl.BlockSpec((ROW_TILE, D_MODEL), lambda i: (row0 // ROW_TILE + i, 0)),
                  pl.BlockSpec((D_MODEL, D_MODEL), lambda i: (0, 0)),
                  pl.BlockSpec((ROW_TILE, D_MODEL), lambda i: (i, 0)),
                  mod_spec(2), mod_spec(3), mod_spec(4), vec, vec,
                  pl.BlockSpec((D_MODEL, LANES), lambda i: (0, 0)),
                  pl.BlockSpec((1, LANES), lambda i: (0, 0))],
        out_specs=[pl.BlockSpec((ROW_TILE, D_MODEL), lambda i: (i, 0)),
                   pl.BlockSpec((ROW_TILE, D_MODEL), lambda i: (i, 0)),
                   pl.BlockSpec((ROW_TILE, LANES), lambda i: (i, 0))],
        out_shape=[jax.ShapeDtypeStruct((n, D_MODEL), F32), jax.ShapeDtypeStruct((n, D_MODEL), BF16),
                   jax.ShapeDtypeStruct((n, LANES), F32)],
        compiler_params=pltpu.CompilerParams(dimension_semantics=("parallel",), vmem_limit_bytes=VMEM_LIMIT_BYTES),
        name="mix_out",
    )(merged, w_out, x, mod4, mod4, mod4, ln_g.reshape(1, -1), ln_b.reshape(1, -1), wr, br)


def _post_norm_kernel(x_ref, y_ref, gate_ref, lng_ref, lnb_ref, o_ref):
    o_ref[...] = _ln(DN_ALPHA * x_ref[...] + (1.0 + gate_ref[...]) * y_ref[...]) * lng_ref[...] + lnb_ref[...]


def post_norm(x, y, mod, part, L, ln_g, ln_b):
    n = x.shape[0]
    nm = mod.shape[0]
    per_seq = L // ROW_TILE
    rows = pl.BlockSpec((ROW_TILE, D_MODEL), lambda i: (i, 0))
    vec = pl.BlockSpec((1, D_MODEL), lambda i: (0, 0))
    return pl.pallas_call(
        _post_norm_kernel,
        grid=(n // ROW_TILE,),
        in_specs=[rows, rows,
                  pl.BlockSpec((None, None, 1, D_MODEL), lambda i: (0 if nm == 1 else i // per_seq, part, 0, 0)),
                  vec, vec],
        out_specs=rows,
        out_shape=jax.ShapeDtypeStruct((n, D_MODEL), F32),
        compiler_params=pltpu.CompilerParams(dimension_semantics=("parallel",), vmem_limit_bytes=VMEM_LIMIT_BYTES),
        name="post_norm",
    )(x, y, mod.reshape(nm, 6, 1, D_MODEL), ln_g.reshape(1, -1), ln_b.reshape(1, -1))


def _nn(a, b):
    return jnp.dot(a, b, preferred_element_type=F32)


def _nt(a, b):
    return lax.dot_general(a, b, (((1,), (1,)), ((), ())), preferred_element_type=F32)


def _tn(a, b):
    return lax.dot_general(a, b, (((0,), (0,)), ((), ())), preferred_element_type=F32)


def _split(a):
    hi = a.astype(BF16)
    return hi, (a - hi.astype(F32)).astype(BF16)


def _mm1(f, a, b):
    return f(a.astype(BF16), b.astype(BF16))


def _mm3(f, a, b):
    ah, al = _split(a)
    bh, bl = _split(b)
    return f(ah, bh) + (f(ah, bl) + f(al, bh))


def _chunk_masks():
    ri = lax.broadcasted_iota(jnp.int32, (CHUNK, CHUNK), 0)
    ci = lax.broadcasted_iota(jnp.int32, (CHUNK, CHUNK), 1)
    eye = ri == ci
    incl = (ri >= ci, ri <= ci)
    strict = (ri > ci, ri < ci)
    return eye, incl, strict


def _to_col(eye, row):
    return jnp.sum(jnp.where(eye, row, 0.0), axis=1, keepdims=True)


GC_BETA, GC_ALPHA, GC_IG, GC_FG, GC_END = 0, 2 * H_A, 4 * H_A, 4 * H_A + 2 * H_B, 4 * H_A + 4 * H_B


def _softplus(x):
    return jnp.maximum(x, 0.0) + jnp.log(1.0 + jnp.exp(-jnp.abs(x)))


def _gate_kernel(x_ref, p_ref, tab_ref, scal_ref, val_s, *, L):
    n = L // CHUNK
    x = x_ref[...]
    p = p_ref[...]
    lane = lax.broadcasted_iota(jnp.int32, (L, LANES), 1)
    beta = jax.nn.sigmoid(x)
    loga = p[0:1] * _softplus(x + p[1:2])
    ig = x + p[2:3]
    lf = -_softplus(-(x + p[3:4]))
    val_s[...] = jnp.where(lane < GC_ALPHA, beta, jnp.where(lane < GC_IG, loga, jnp.where(lane < GC_FG, ig, lf)))

    ri = lax.broadcasted_iota(jnp.int32, (CHUNK, CHUNK), 0)
    ci = lax.broadcasted_iota(jnp.int32, (CHUNK, CHUNK), 1)
    lower = jnp.where(ri >= ci, 1.0, 0.0).astype(BF16)
    upper = jnp.where(ri <= ci, 1.0, 0.0).astype(BF16)
    ident = jnp.where(ri == ci, 1.0, 0.0).astype(BF16)
    cl = lax.broadcasted_iota(jnp.int32, (CHUNK, LANES), 1)
    cumulative = ((cl >= GC_ALPHA) & (cl < GC_IG)) | ((cl >= GC_FG) & (cl < GC_END))
    backward = ((cl >= GC_ALPHA + H_A) & (cl < GC_IG)) | (cl >= GC_FG + H_B)

    def split3(v):
        h1 = v.astype(BF16)
        r1 = v - h1.astype(F32)
        h2 = r1.astype(BF16)
        return h1, h2, (r1 - h2.astype(F32)).astype(BF16)

    def chunk(c, carry):
        v = val_s[pl.ds(pl.multiple_of(c * CHUNK, CHUNK), CHUNK), :]
        parts = split3(v)
        pre = sum(_nn(lower, h) for h in parts)
        suf = sum(_nn(upper, h) for h in parts)
        out = jnp.where(cumulative, jnp.where(backward, suf, pre), v)
        tot = jnp.where(backward[0:1], suf[0:1], pre[CHUNK - 1:CHUNK])
        wlog = tot - out + pltpu.roll(out, GC_FG - GC_IG, 1)
        wmax = jnp.max(wlog, axis=0, keepdims=True)
        tab_ref[c] = sum(_tn(h, ident) for h in split3(out))
        scal_ref[c] = jnp.concatenate([tot, wmax, jnp.zeros((6, LANES), F32)], axis=0)
        return carry

    lax.fori_loop(0, n, chunk, 0)


def gate_tables(proj_small, row_blk0, nb, L, lp):
    n = L // CHUNK
    rows = [jnp.pad(v.reshape(-1), (off, LANES - off - v.size)) for v, off in (
        (-jnp.exp(lp['gdn_a_log']), GC_ALPHA), (lp['gdn_dt_bias'], GC_ALPHA),
        (lp['ml_i_bias'], GC_IG), (lp['ml_f_bias'], GC_FG))]
    params = jnp.stack(rows + [jnp.zeros((LANES,), F32)] * 4)
    return pl.pallas_call(
        functools.partial(_gate_kernel, L=L),
        grid=(nb,),
        in_specs=[pl.BlockSpec((L, LANES), lambda b: (row_blk0 + b, 0)),
                  pl.BlockSpec((8, LANES), lambda b: (0, 0))],
        out_specs=[pl.BlockSpec((None, n, LANES, CHUNK), lambda b: (b, 0, 0, 0)),
                   pl.BlockSpec((None, n, 8, LANES), lambda b: (b, 0, 0, 0))],
        out_shape=[jax.ShapeDtypeStruct((nb, n, LANES, CHUNK), F32),
                   jax.ShapeDtypeStruct((nb, n, 8, LANES), F32)],
        scratch_shapes=[pltpu.VMEM((L, LANES), F32)],
        compiler_params=pltpu.CompilerParams(dimension_semantics=("parallel",), vmem_limit_bytes=VMEM_LIMIT_BYTES),
        name="gate_tables",
    )(proj_small, params)


def _gate_row(tab_ref, c, lane):
    return tab_ref[c, pl.ds(lane, 1), :]


def _gate_scalar(scal_ref, c, row, lane):
    v = scal_ref[c][row:row + 1, :]
    li = lax.broadcasted_iota(jnp.int32, (1, LANES), 1)
    return jnp.sum(jnp.where(li == lane, v, 0.0), axis=1, keepdims=True)


def _gdn_kernel(q_ref, k_ref, v_ref, z_ref, cw_ref, tab_ref, scal_ref, s0_ref, nw_ref, y_ref, sf_ref,
                qs, ks, vs, wq_s, u_s, kd_s, p_s, o_s, st_s, *, L):
    n = L // CHUNK
    head = pl.program_id(1)
    beta_lane = [GC_BETA + d * H_A + head for d in range(2)]
    g_lane = [GC_ALPHA + d * H_A + head for d in range(2)]
    row = lax.broadcasted_iota(jnp.int32, (L, LANES), 0)

    def conv_silu(x_ref, part):
        x = x_ref[...]
        w = cw_ref[part]
        x_prev = jnp.where(row == 0, 0.0, pltpu.roll(x, 1, 0))
        x_next = jnp.where(row == L - 1, 0.0, pltpu.roll(x, L - 1, 0))
        y = x_prev * w[0:1] + x * w[1:2] + x_next * w[2:3]
        return y * jax.nn.sigmoid(y)

    def l2n(x):
        return x * lax.rsqrt(jnp.sum(x * x, axis=-1, keepdims=True) + LN_EPS)

    qs[...] = l2n(conv_silu(q_ref, 0)) * (DK_A ** -0.5)
    ks[...] = l2n(conv_silu(k_ref, 1))
    vs[...] = conv_silu(v_ref, 2)

    eye, incl, strict = _chunk_masks()
    eye_f = jnp.where(eye, 1.0, 0.0)

    def intra(grp, carry):
        chains = []
        for j in range(INTRA_GROUP):
            c = grp * INTRA_GROUP + j
            r0 = pl.multiple_of(c * CHUNK, CHUNK)
            qc = qs[pl.ds(r0, CHUNK), :]
            kc = ks[pl.ds(r0, CHUNK), :]
            vc = vs[pl.ds(r0, CHUNK), :]
            kk = _mm3(_nt, kc, kc)
            qk = _mm1(_nt, qc, kc)
            for d in range(2):
                g_row = _gate_row(tab_ref, c, g_lane[d])
                b_row = _gate_row(tab_ref, c, beta_lane[d])
                gl_row = _gate_scalar(scal_ref, c, 0, g_lane[d])
                g_col = _to_col(eye, g_row)
                b_col = _to_col(eye, b_row)
                dec = jnp.exp(jnp.where(incl[d], g_col - g_row, -jnp.inf))
                lmat = b_col * kk * jnp.where(strict[d], dec, 0.0)
                kd_s[d, pl.ds(r0, CHUNK), :] = jnp.exp(gl_row - g_col) * kc
                wq_s[d, c, CHUNK:, :] = jnp.exp(g_col) * qc
                p_s[d, c] = qk * dec
                chains.append(dict(d=d, c=c, r0=r0, pw=lmat, tinv=eye_f - lmat,
                                   rhs_w=(b_col * jnp.exp(g_col)) * kc, rhs_u=b_col * vc))
        for _ in range(int(math.log2(CHUNK)) - 1):
            for ch in chains:
                ch['pw'] = _mm3(_nn, ch['pw'], ch['pw'])
            for ch in chains:
                ch['tinv'] = ch['tinv'] + _mm3(_nn, ch['tinv'], ch['pw'])
        for ch in chains:
            d, r0 = ch['d'], ch['r0']
            wq_s[d, ch['c'], :CHUNK, :] = _mm3(_nn, ch['tinv'], ch['rhs_w'])
            u_s[d, pl.ds(r0, CHUNK), :] = _mm3(_nn, ch['tinv'], ch['rhs_u'])
        return carry

    lax.fori_loop(0, n // INTRA_GROUP, intra, 0)

    st_s[...] = s0_ref[...]

    def scan(i, carry):
        cs = (i, n - 1 - i)
        r0s = [pl.multiple_of(c * CHUNK, CHUNK) for c in cs]
        s = [st_s[d] for d in range(2)]
        ws = [_mm1(_nn, wq_s[d, cs[d]], s[d]) for d in range(2)]
        uc = [u_s[d, pl.ds(r0s[d], CHUNK), :] - ws[d][:CHUNK] for d in range(2)]
        pu = [_mm1(_nn, p_s[d, cs[d]], uc[d]) for d in range(2)]
        ku = [_mm1(_tn, kd_s[d, pl.ds(r0s[d], CHUNK), :], uc[d]) for d in range(2)]
        for d in range(2):
            o_s[d, pl.ds(r0s[d], CHUNK), :] = ws[d][CHUNK:] + pu[d]
            st_s[d] = jnp.exp(_gate_scalar(scal_ref, cs[d], 0, g_lane[d])) * s[d] + ku[d]
        return carry

    lax.fori_loop(0, n, scan, 0)

    o = o_s[0] + o_s[1]
    z = z_ref[...]
    o = o * lax.rsqrt(jnp.mean(o * o, axis=-1, keepdims=True) + LN_EPS) * nw_ref[...]
    y_ref[...] = (o * (z * jax.nn.sigmoid(z))).astype(y_ref.dtype)
    sf_ref[...] = st_s[...]


def _gate_specs(n):
    return [pl.BlockSpec((None, n, LANES, CHUNK), lambda b, h: (b, 0, 0, 0)),
            pl.BlockSpec((None, n, 8, LANES), lambda b, h: (b, 0, 0, 0))]


def gdn_branch(proj, row_blk0, nb, L, gates, conv_w, s0, norm_w):
    n = L // CHUNK
    seq = lambda col: pl.BlockSpec((L, LANES), lambda b, h, col=col: (row_blk0 + b, col // LANES + h))
    scr = lambda *s: pltpu.VMEM(s, F32)
    return pl.pallas_call(
        functools.partial(_gdn_kernel, L=L),
        grid=(nb, H_A),
        in_specs=[seq(COL_QA), seq(COL_KA), seq(COL_VA), seq(COL_ZA),
                  pl.BlockSpec((3, None, 3, LANES), lambda b, h: (0, h, 0, 0)),
                  *_gate_specs(n),
                  pl.BlockSpec((None, 2, None, DK_A, DV_A), lambda b, h: (b, 0, h, 0, 0)),
                  pl.BlockSpec((1, LANES), lambda b, h: (0, 0))],
        out_specs=[pl.BlockSpec((L, LANES), lambda b, h: (b, h)),
                   pl.BlockSpec((None, 2, None, DK_A, DV_A), lambda b, h: (b, 0, h, 0, 0))],
        out_shape=[jax.ShapeDtypeStruct((nb * L, V_A), BF16),
                   jax.ShapeDtypeStruct((nb, 2, H_A, DK_A, DV_A), F32)],
        scratch_shapes=[scr(L, LANES), scr(L, LANES), scr(L, LANES),
                        scr(2, n, 2 * CHUNK, LANES), scr(2, L, LANES), scr(2, L, LANES),
                        scr(2, n, CHUNK, CHUNK), scr(2, L, LANES), scr(2, DK_A, DV_A)],
        compiler_params=pltpu.CompilerParams(
            dimension_semantics=("parallel", "parallel"), vmem_limit_bytes=VMEM_LIMIT_BYTES),
        name="gdn_branch",
    )(proj, proj, proj, proj, conv_w, *gates, s0, norm_w)


def _mlstm_kernel(q_ref, k_ref, v_ref, ob_ref, tab_ref, scal_ref, c0_ref, n0_ref, m0_ref, nw_ref,
                  y_ref, cf_ref, nf_ref, mf_ref, h_s, c_s, n_s, m_s, *, L):
    n = L // CHUNK
    head = pl.program_id(1)
    ig_lane = [GC_IG + d * H_B + head for d in range(2)]
    fg_lane = [GC_FG + d * H_B + head for d in range(2)]
    eye, incl, _ = _chunk_masks()
    c_s[...] = c0_ref[...]
    n_s[...] = n0_ref[...]
    m_s[...] = m0_ref[...]

    def body(grp, carry):
        ch = []
        for j in range(ML_GROUP):
            i = grp * ML_GROUP + j
            for d in range(2):
                c = i if d == 0 else n - 1 - i
                r0 = pl.multiple_of(c * CHUNK, CHUNK)
                ch.append(dict(d=d, r0=r0, qc=q_ref[pl.ds(r0, CHUNK), :],
                               kc=k_ref[pl.ds(r0, CHUNK), :] * (DQK_B ** -0.5), vc=v_ref[pl.ds(r0, CHUNK), :],
                               b_row=_gate_row(tab_ref, c, fg_lane[d]), i_row=_gate_row(tab_ref, c, ig_lane[d]),
                               bl_row=_gate_scalar(scal_ref, c, 0, fg_lane[d]),
                               wm_row=_gate_scalar(scal_ref, c, 1, fg_lane[d])))
        for x in ch:
            x['qk'] = _mm1(_nt, x['qc'], x['kc'])
        for x in ch:
            x['b_col'] = _to_col(eye, x['b_row'])
            i_col = _to_col(eye, x['i_row'])
            dlog = jnp.where(incl[x['d']], x['b_col'] - x['b_row'] + x['i_row'], -jnp.inf)
            x['dmax'] = jnp.max(dlog, axis=1, keepdims=True)
            x['pw'] = jnp.exp(dlog - x['dmax']) * x['qk']
            x['ewk'] = jnp.exp(x['bl_row'][:, :1] - x['b_col'] + i_col - x['wm_row'][:, :1]) * x['kc']
        for x in ch:
            x['intra_num'] = _mm1(_nn, x['pw'], x['vc'])
            x['dc'] = _mm1(_tn, x['ewk'], x['vc'])
        for x in ch:
            x['intra_den'] = jnp.sum(x['pw'], axis=1, keepdims=True)
            x['dn'] = jnp.sum(x['ewk'], axis=0, keepdims=True)
        for x in ch:
            d, r0 = x['d'], x['r0']
            cm, nm, mm = c_s[d], n_s[d], m_s[d]
            alog = x['b_col'] + mm
            mt = jnp.maximum(alog, x['dmax'])
            wi = jnp.exp(alog - mt)[:, :1]
            wa = jnp.exp(x['dmax'] - mt)[:, :1]
            num = wi * _mm1(_nn, x['qc'], cm) + wa * x['intra_num']
            den = wi * jnp.sum(x['qc'] * nm, axis=1, keepdims=True) + wa * x['intra_den']
            h_s[d, pl.ds(r0, CHUNK), :] = num / jnp.maximum(jnp.abs(den), jnp.exp(-mt[:, :1]))

            m_new = jnp.maximum(x['bl_row'] + mm, x['wm_row'])
            a = jnp.exp(x['bl_row'] + mm - m_new)
            e = jnp.exp(x['wm_row'] - m_new)
            c_s[d] = a[:, :1] * cm + e[:, :1] * x['dc']
            n_s[d] = a * nm + e * x['dn']
            m_s[d] = m_new
        return carry

    lax.fori_loop(0, n // ML_GROUP, body, 0)

    h = h_s[0] + h_s[1]
    h = h * lax.rsqrt(jnp.mean(h * h, axis=-1, keepdims=True) + LN_EPS) * nw_ref[...]
    y_ref[...] = (h * jax.nn.sigmoid(ob_ref[...])).astype(y_ref.dtype)
    cf_ref[...] = c_s[...]
    nf_ref[...] = n_s[...]
    mf_ref[...] = m_s[...]


def mlstm_branch(proj, row_blk0, nb, L, gates, c0, n0, m0, norm_w):
    n = L // CHUNK
    seq = lambda col, w: pl.BlockSpec((L, w), lambda b, h, col=col, w=w: (row_blk0 + b, col // w + h))
    st = lambda *s: pl.BlockSpec((None, 2, None) + s, lambda b, h: (b, 0, h, 0, 0))
    scr = lambda *s: pltpu.VMEM(s, F32)
    return pl.pallas_call(
        functools.partial(_mlstm_kernel, L=L),
        grid=(nb, H_B),
        in_specs=[seq(COL_QB, DQK_B), seq(COL_KB, DQK_B), seq(COL_VB, DV_B), seq(COL_OB, DV_B),
                  *_gate_specs(n),
                  st(DQK_B, DV_B), st(1, DQK_B), st(1, LANES),
                  pl.BlockSpec((1, DV_B), lambda b, h: (0, 0))],
        out_specs=[pl.BlockSpec((L, DV_B), lambda b, h: (b, h)),
                   st(DQK_B, DV_B), st(1, DQK_B), st(1, LANES)],
        out_shape=[jax.ShapeDtypeStruct((nb * L, V_B), BF16),
                   jax.ShapeDtypeStruct((nb, 2, H_B, DQK_B, DV_B), F32),
                   jax.ShapeDtypeStruct((nb, 2, H_B, 1, DQK_B), F32),
                   jax.ShapeDtypeStruct((nb, 2, H_B, 1, LANES), F32)],
        scratch_shapes=[scr(2, L, DV_B), scr(2, DQK_B, DV_B), scr(2, 1, DQK_B), scr(2, 1, LANES)],
        compiler_params=pltpu.CompilerParams(
            dimension_semantics=("parallel", "parallel"), vmem_limit_bytes=VMEM_LIMIT_BYTES),
        name="mlstm_branch",
    )(proj, proj, proj, proj, *gates, c0, n0, m0, norm_w)


def _ln_stats(x):
    mu = jnp.mean(x, -1, keepdims=True)
    var = jnp.mean(jnp.square(x - mu), -1, keepdims=True)
    return (x - mu) * lax.rsqrt(var + LN_EPS)


def centred_conv(x, w):
    k = w.shape[0]
    p = k // 2
    L = x.shape[1]
    xp = jnp.pad(x, ((0, 0), (p, p), (0, 0)))
    y = xp[:, 0:L] * w[0]
    for j in range(1, k):
        y = y + xp[:, j:j + L] * w[j]
    return y


def hyena_filters(L, w1, b1, w2, b2, w3, b3, freq, rate):
    pos = jnp.arange(L, dtype=F32)
    t = pos / L
    ang = (2.0 * math.pi) * t[:, None] * jnp.arange(1, HY_BANDS + 1, dtype=F32)
    feats = jnp.concatenate([t[:, None], jnp.sin(ang), jnp.cos(ang)], -1)
    hp = lax.Precision.HIGHEST
    z = jnp.sin(freq[0] * (jnp.dot(feats, w1, precision=hp) + b1))
    z = jnp.sin(freq[1] * (jnp.dot(z, w2, precision=hp) + b2))
    filt = jnp.dot(z, w3, precision=hp) + b3
    lag = jnp.abs(pos - L // 2) / L
    filt = filt * jnp.exp(-lag[:, None] * rate)
    return filt.reshape(L, HY_ORDER, HY_CH)


def dft_size(L):
    n = 3 * L // 2
    return n if (n // 2) % LANES == 0 else 2 * L


def dft_matrices(L):
    N = dft_size(L)
    F = N // 2
    k = jnp.arange(F, dtype=jnp.int32)
    t = jnp.arange(L, dtype=jnp.int32)
    w = 2.0 * math.pi / N

    def cos_sin(rows, cols):
        hi = jnp.arange(cols.shape[0] // LANES, dtype=jnp.int32) * LANES
        lo = jnp.arange(LANES, dtype=jnp.int32)
        a = w * ((rows[:, None] * hi[None, :]) % N).astype(F32)[:, :, None]
        b = w * ((rows[:, None] * lo[None, :]) % N).astype(F32)[:, None, :]
        ca, sa, cb, sb = jnp.cos(a), jnp.sin(a), jnp.cos(b), jnp.sin(b)
        shape = (rows.shape[0], cols.shape[0])
        return (ca * cb - sa * sb).reshape(shape), (sa * cb + ca * sb).reshape(shape)

    alt = (1 - 2 * (t % 2)).astype(F32)
    cf, sin_f = cos_sin(k, t)
    sf = jnp.where(k[:, None] == 0, alt[None, :], -sin_f)
    tt = t + L // 2
    alt_i = (1 - 2 * (tt % 2)).astype(F32)
    cos_i, sin_i = cos_sin(tt, k)
    ci = jnp.where(k[None, :] == 0, 1.0 / N, (2.0 / N) * cos_i)
    si = jnp.where(k[None, :] == 0, alt_i[:, None] / N, (-2.0 / N) * sin_i)
    return cf, sf, ci, si


def _mm3_kernel(x_ref, w_ref, o_ref):
    o_ref[...] = _mm3(_nn, x_ref[...], w_ref[...])


def pmm3(x, w):
    m, k = x.shape
    _, n = w.shape
    tm = _pick(m, (512, 256, 128))
    tn = _pick(n, (512, 256, 128))
    return pl.pallas_call(
        _mm3_kernel,
        grid=(m // tm, n // tn),
        in_specs=[pl.BlockSpec((tm, k), lambda i, j: (i, 0)),
                  pl.BlockSpec((k, tn), lambda i, j: (0, j))],
        out_specs=pl.BlockSpec((tm, tn), lambda i, j: (i, j)),
        out_shape=jax.ShapeDtypeStruct((m, n), F32),
        compiler_params=pltpu.CompilerParams(
            dimension_semantics=("parallel", "parallel"), vmem_limit_bytes=VMEM_LIMIT_BYTES),
        name="pmm3",
    )(x, w)


def _hyena_kernel(z_ref, un_ref, cwz_ref, cwu_ref, skip_ref, hr_ref, hi_ref, cf_ref, sf_ref, ci_ref, si_ref,
                  o_ref, z_s, zb_s, acc_s, *, L, ft, conv_z):
    f = pl.program_id(2)
    row = lax.broadcasted_iota(jnp.int32, (L, z_ref.shape[1]), 0)

    def conv3(x, w):
        x_prev = jnp.where(row == 0, 0.0, pltpu.roll(x, 1, 0))
        x_next = jnp.where(row == L - 1, 0.0, pltpu.roll(x, L - 1, 0))
        return x_prev * w[0:1] + x * w[1:2] + x_next * w[2:3]

    @pl.when(f == 0)
    def _():
        z = z_ref[...]
        if conv_z:
            z = conv3(z, cwz_ref[...])
        z_s[...] = z
        zb_s[...] = z.astype(BF16)

    zb = zb_s[...]
    zr = _nn(cf_ref[...], zb)
    zi = _nn(sf_ref[...], zb)
    hr = hr_ref[...]
    hi = hi_ref[...]
    packed = (f * ft + lax.broadcasted_iota(jnp.int32, zr.shape, 0)) == 0
    zihi = zi * hi
    yr = zr * hr - jnp.where(packed, 0.0, zihi)
    yi = jnp.where(packed, zihi, zr * hi + zi * hr)
    part = _nn(ci_ref[...], yr.astype(BF16)) + _nn(si_ref[...], yi.astype(BF16))

    @pl.when(f == 0)
    def _():
        acc_s[...] = part

    @pl.when(f > 0)
    def _():
        acc_s[...] += part

    @pl.when(f == pl.num_programs(2) - 1)
    def _():
        un = conv3(un_ref[...], cwu_ref[...])
        o_ref[...] = (un * (acc_s[...] + skip_ref[...] * z_s[...])).astype(o_ref.dtype)


def hyena_order(z_arr, z_blk0, z_col, conv_z, proj, row_blk0, un_col, nb, L, conv_w, cwz_col, skip,
                hr, hi, h_col, mats, out_dtype=F32):
    cf, sf, ci, si = mats
    F = cf.shape[0]
    cb = HY_CH if L * HY_CH * 4 <= (1 << 20) else 256
    ft = _pick(F, (512, 256))
    seq = lambda blk0, col: pl.BlockSpec((L, cb), lambda b, j, f: (blk0 + b, col // cb + j))
    return pl.pallas_call(
        functools.partial(_hyena_kernel, L=L, ft=ft, conv_z=conv_z),
        grid=(nb, HY_CH // cb, F // ft),
        in_specs=[seq(z_blk0, z_col), seq(row_blk0, un_col),
                  pl.BlockSpec((3, cb), lambda b, j, f: (0, cwz_col // cb + j)),
                  pl.BlockSpec((3, cb), lambda b, j, f: (0, (un_col - COL_HY) // cb + j)),
                  pl.BlockSpec((1, cb), lambda b, j, f: (0, j)),
                  pl.BlockSpec((ft, cb), lambda b, j, f: (f, h_col // cb + j)),
                  pl.BlockSpec((ft, cb), lambda b, j, f: (f, h_col // cb + j)),
                  pl.BlockSpec((ft, L), lambda b, j, f: (f, 0)),
                  pl.BlockSpec((ft, L), lambda b, j, f: (f, 0)),
                  pl.BlockSpec((L, ft), lambda b, j, f: (0, f)),
                  pl.BlockSpec((L, ft), lambda b, j, f: (0, f))],
        out_specs=pl.BlockSpec((L, cb), lambda b, j, f: (b, j)),
        out_shape=jax.ShapeDtypeStruct((nb * L, HY_CH), out_dtype),
        scratch_shapes=[pltpu.VMEM((L, cb), F32), pltpu.VMEM((L, cb), BF16), pltpu.VMEM((L, cb), F32)],
        compiler_params=pltpu.CompilerParams(
            dimension_semantics=("parallel", "parallel", "arbitrary"), vmem_limit_bytes=VMEM_LIMIT_BYTES),
        name="hyena_order",
    )(z_arr, proj, conv_w, conv_w, skip, hr, hi, cf, sf, ci, si)


def grid_pos_embed(n_tokens):
    rows = n_tokens // GRID_W
    quarter = D_MODEL // 4
    omega = 1.0 / (10000.0 ** (jnp.arange(quarter, dtype=F32) / quarter))
    r = jnp.arange(rows, dtype=F32)[:, None] * omega
    cl = jnp.arange(GRID_W, dtype=F32)[:, None] * omega
    full = lambda v, axis: jnp.broadcast_to(jnp.expand_dims(v, axis), (rows, GRID_W, quarter))
    pe = jnp.concatenate([full(jnp.sin(r), 1), full(jnp.cos(r), 1), full(jnp.sin(cl), 0), full(jnp.cos(cl), 0)], -1)
    return pe.reshape(rows * GRID_W, D_MODEL)


def seq_mixers(proj_big, proj_small, row0, nb, L, lp, mats, s_gdn, s_c, s_n, s_m):
    blk0 = row0 // L
    gates = gate_tables(proj_small, blk0, nb, L, lp)

    conv_w = lp['gdn_conv'].reshape(3, 3, H_A, DK_A).transpose(1, 2, 0, 3)
    y_a, s_gdn_new = gdn_branch(proj_big, blk0, nb, L, gates, conv_w, s_gdn, lp['gdn_norm'].reshape(1, DV_A))

    m0 = jnp.broadcast_to(s_m[..., None, None], s_m.shape + (1, LANES))
    y_b, c_new, n_new, m_new = mlstm_branch(proj_big, blk0, nb, L, gates, s_c, s_n[..., None, :], m0,
                                            lp['ml_norm'].reshape(1, DV_B))

    filt = hyena_filters(L, lp['hy_w1'], lp['hy_b1'], lp['hy_w2'], lp['hy_b2'], lp['hy_w3'], lp['hy_b3'],
                         lp['hy_freq'], lp['hy_rate']).reshape(L, HY_ORDER * HY_CH)
    mats32, mats16 = mats[L]
    hr, hi = pmm3(mats32[0], filt), pmm3(mats32[1], filt)
    z_arr, z_blk0, z_col = proj_big, blk0, COL_HY
    for order in range(HY_ORDER):
        z_arr = hyena_order(z_arr, z_blk0, z_col, order == 0, proj_big, blk0, COL_HY + (order + 1) * HY_CH, nb, L,
                            lp['hy_conv'], 0, lp['hy_skip'][order:order + 1], hr, hi, order * HY_CH, mats16,
                            out_dtype=BF16 if order == HY_ORDER - 1 else F32)
        z_blk0, z_col = 0, 0
    return (y_a, y_b, z_arr), (s_gdn_new, c_new, n_new[..., 0, :], m_new[..., 0, 0])


TOK_BLK = LANES
COMBINE_ROWS = 48


def _route_kernel(a_ref, rm_ref, idx_ref, th_s, *, nblk, cap):
    bits = pltpu.bitcast(a_ref[...], jnp.int32)

    def search(i, th):
        cand = th | jnp.left_shift(jnp.int32(1), 30 - i)
        cnt = jnp.sum(jnp.sum(jnp.where(bits >= cand, 1.0, 0.0), axis=2, keepdims=True), axis=1, keepdims=True)
        return jnp.where(cnt >= cap, cand, th)

    th_s[...] = lax.fori_loop(0, 31, search, jnp.zeros((N_EXPERTS, 1, LANES), jnp.int32))

    li = lax.broadcasted_iota(jnp.int32, (LANES, LANES), 0)
    lj = lax.broadcasted_iota(jnp.int32, (LANES, LANES), 1)
    before_lane = jnp.where(li < lj, 1.0, 0.0).astype(BF16)
    bi = lax.broadcasted_iota(jnp.int32, (nblk, nblk), 0)
    bj = lax.broadcasted_iota(jnp.int32, (nblk, nblk), 1)
    before_blk = jnp.where(bj < bi, 1.0, 0.0).astype(BF16)
    diag = lax.broadcasted_iota(jnp.int32, (nblk, LANES), 0) == lax.broadcasted_iota(jnp.int32, (nblk, LANES), 1)
    slot = lax.broadcasted_iota(jnp.int32, (cap, LANES), 0).astype(F32)
    lane = lax.broadcasted_iota(jnp.int32, (cap, LANES), 1)
    lane_f = lane.astype(F32)

    def ranks(mask):
        m = mask.astype(BF16)
        inside = _nn(m, before_lane)
        cnt = jnp.sum(mask, axis=1, keepdims=True)
        earlier = _nn(before_blk, jnp.broadcast_to(cnt, (nblk, LANES)).astype(BF16))
        return inside, earlier, cnt

    def per_expert(e, carry):
        b = pltpu.bitcast(a_ref[e], jnp.int32)
        th = th_s[e]
        gt = jnp.where(b > th, 1.0, 0.0)
        eq = jnp.where(b == th, 1.0, 0.0)
        need = cap - jnp.sum(jnp.sum(gt, axis=1, keepdims=True), axis=0, keepdims=True)
        eq_in, eq_before, _ = ranks(eq)
        sel = jnp.maximum(gt, jnp.where(eq_in + eq_before < need, eq, 0.0))
        inside, earlier, cnt = ranks(sel)
        rm = jnp.where(sel > 0.0, inside, -1.0)
        rm_ref[e] = rm
        end_row = jnp.sum(jnp.where(diag, earlier + cnt, 0.0), axis=0, keepdims=True)
        cnt_row = jnp.sum(jnp.where(diag, jnp.broadcast_to(cnt, (nblk, LANES)), 0.0), axis=0, keepdims=True)
        done = jnp.where((lane < nblk) & (end_row <= slot), 1.0, 0.0)
        blk = jnp.sum(done, axis=1, keepdims=True)
        local = slot[:, :1] - jnp.sum(done * cnt_row, axis=1, keepdims=True)
        onehot = jnp.where(lane_f == blk, 1.0, 0.0).astype(BF16)
        rows = _nn(onehot[:, :nblk], rm.astype(BF16))
        tok_lane = jnp.sum(jnp.where(rows == local, lane_f, 0.0), axis=1, keepdims=True)
        idx_ref[e] = (blk * TOK_BLK + tok_lane).astype(jnp.int32)
        return carry

    lax.fori_loop(0, N_EXPERTS, per_expert, 0)


def route(aff_t):
    e, n = aff_t.shape
    nblk, cap = n // TOK_BLK, CAP_FACTOR * n // N_EXPERTS
    full = lambda *s: pl.BlockSpec(s, lambda i: (0,) * len(s))
    rm, idx = pl.pallas_call(
        functools.partial(_route_kernel, nblk=nblk, cap=cap),
        grid=(1,),
        in_specs=[full(e, nblk, TOK_BLK)],
        out_specs=[full(e, nblk, TOK_BLK), full(e, cap, 1)],
        out_shape=[jax.ShapeDtypeStruct((e, nblk, TOK_BLK), F32), jax.ShapeDtypeStruct((e, cap, 1), jnp.int32)],
        scratch_shapes=[pltpu.VMEM((e, 1, LANES), jnp.int32)],
        compiler_params=pltpu.CompilerParams(dimension_semantics=("arbitrary",), vmem_limit_bytes=VMEM_LIMIT_BYTES),
        name="route",
    )(aff_t.reshape(e, nblk, TOK_BLK))
    return rm, idx[..., 0]


def _combine_kernel(base_ref, cnt_ref, rm_ref, x_ref, gate_ref, lng_ref, lnb_ref, ye_ref, o_ref,
                    buf, sem, acc, *, m_rows):
    b = pl.program_id(0)
    w_rows = COMBINE_ROWS
    r_iota = lax.broadcasted_iota(jnp.int32, (w_rows, LANES), 0).astype(F32)
    base = [base_ref[b * N_EXPERTS + e] for e in range(N_EXPERTS)]
    first = [(v // 8) * 8 for v in base]

    def window(e, w):
        return pl.multiple_of(jnp.minimum(first[e] + w * w_rows, m_rows - w_rows), 8)

    def copy(e, start):
        return pltpu.make_async_copy(ye_ref.at[e, pl.ds(start, w_rows), :], buf.at[e], sem.at[e])

    def place(e, w, start):
        rank = rm_ref[e:e + 1, :]
        rel = rank + (base[e] - first[e]).astype(F32)
        lo_row = w * jnp.float32(w_rows)
        mine = (rank >= 0.0) & (rel >= lo_row) & (rel < lo_row + w_rows)
        target = jnp.where(mine, rel + (first[e] - start).astype(F32), -1.0)
        put = jnp.where(r_iota == target, 1.0, 0.0).astype(BF16)
        hi, lo = _split(buf[e])
        return _tn(put, hi) + _tn(put, lo)

    for e in range(N_EXPERTS):
        copy(e, window(e, 0)).start()
    for e in range(N_EXPERTS):
        start = window(e, 0)
        copy(e, start).wait()
        part = place(e, 0, start)
        if e == 0:
            acc[...] = part
        else:
            acc[...] += part
        n_win = (base[e] - first[e] + cnt_ref[b * N_EXPERTS + e] + w_rows - 1) // w_rows

        def extra(w, carry, e=e):
            s = window(e, w)
            cp = copy(e, s)
            cp.start()
            cp.wait()
            acc[...] += place(e, w, s)
            return carry

        lax.fori_loop(1, n_win, extra, 0)

    o_ref[...] = _ln(DN_ALPHA * x_ref[...] + (1.0 + gate_ref[...]) * acc[...]) * lng_ref[...] + lnb_ref[...]


def combine_post_norm(ye, m0, rm, x1, mod, L, ln_g, ln_b):
    e, m_rows, d = ye.shape
    n = x1.shape[0]
    nblk = n // TOK_BLK
    nm = mod.shape[0]
    per_seq = L // TOK_BLK
    rm_t = rm.transpose(1, 0, 2)
    cnt = jnp.sum(rm_t >= 0.0, axis=-1).astype(jnp.int32)
    base = m0 + jnp.cumsum(cnt, axis=0) - cnt
    rows = pl.BlockSpec((TOK_BLK, d), lambda i, *_: (i, 0))
    vec = pl.BlockSpec((1, d), lambda i, *_: (0, 0))
    return pl.pallas_call(
        functools.partial(_combine_kernel, m_rows=m_rows),
        grid_spec=pltpu.PrefetchScalarGridSpec(
            num_scalar_prefetch=2,
            grid=(nblk,),
            in_specs=[pl.BlockSpec((None, e, TOK_BLK), lambda i, *_: (i, 0, 0)),
                      rows,
                      pl.BlockSpec((None, None, 1, d), lambda i, *_: (0 if nm == 1 else i // per_seq, 5, 0, 0)),
                      vec, vec,
                      pl.BlockSpec(memory_space=pl.ANY)],
            out_specs=rows,
            scratch_shapes=[pltpu.VMEM((e, COMBINE_ROWS, d), F32), pltpu.SemaphoreType.DMA((e,)),
                            pltpu.VMEM((TOK_BLK, d), F32)]),
        out_shape=jax.ShapeDtypeStruct((n, d), F32),
        compiler_params=pltpu.CompilerParams(dimension_semantics=("arbitrary",), vmem_limit_bytes=VMEM_LIMIT_BYTES),
        name="combine_post_norm",
    )(base.reshape(-1), cnt.reshape(-1), rm_t, x1, mod.reshape(nm, 6, 1, d), ln_g.reshape(1, -1),
      ln_b.reshape(1, -1), ye)


def expert_choice_ffn(hfs, logits, w_gate, w_up, w_down):
    xes, gates, tables, offs, off = [], [], [], [], 0
    for hf, lg in zip(hfs, logits):
        aff_t = jax.nn.softmax(lg, axis=-1).T
        rm, idx = route(aff_t)
        xes.append(hf[idx])
        gates.append(jnp.take_along_axis(aff_t, idx, axis=1))
        tables.append(rm)
        offs.append(off)
        off += idx.shape[1]
    gate = jnp.concatenate(gates, axis=1)[..., None]
    ye = expert_down(expert_hidden(jnp.concatenate(xes, axis=1), w_gate, w_up), w_down, gate)
    return ye, tables, offs


_IN_SIZES = (QK_A, QK_A, V_A, V_A, 2 * H_A, 2 * H_A, QK_B, QK_B, V_B, V_B, 2 * H_B, 2 * H_B,
             3 * HY_CH, D_MODEL, D_MODEL, D_MODEL)
_IN_OFFS = tuple(int(v) for v in np.concatenate([[0], np.cumsum(_IN_SIZES)]))
_SMALL_PARTS = (4, 5, 10, 11)


def trunk_layer(x_sets, mods, lp, mats, states):
    hs = []
    for x, mod in zip(x_sets, mods):
        shift1, scale1 = mod[..., :D_MODEL], mod[..., D_MODEL:2 * D_MODEL]
        hs.append((_ln_stats(x) * (1.0 + scale1) + shift1).reshape(-1, D_MODEL))
    h_all = jnp.concatenate(hs, 0)

    big_cols = [p for p in range(len(_IN_SIZES)) if p not in _SMALL_PARTS]
    w_in = lp['w_in']
    w_big = jnp.concatenate([w_in[:, _IN_OFFS[p]:_IN_OFFS[p + 1]] for p in big_cols], -1)
    w_small = jnp.concatenate([w_in[:, _IN_OFFS[p]:_IN_OFFS[p + 1]] for p in _SMALL_PARTS], -1)
    w_small = jnp.pad(w_small, ((0, 0), (0, LANES - w_small.shape[1])))
    proj_big = pmm(h_all, w_big)
    proj_small = pmm(h_all, w_small)

    mixed, new_states, row = [], [], 0
    for x, st in zip(x_sets, states):
        b, L, _ = x.shape
        outs, st_new = seq_mixers(proj_big, proj_small, row, b, L, lp, mats, *st)
        row += b * L
        mixed.append(outs)
        new_states.append(st_new)

    y_a, y_b, y_c = (jnp.concatenate(ts, 0) for ts in zip(*mixed))
    merged = branch_merge(y_a, y_b, y_c, lp['w_br_a'], lp['w_br_b'], lp['w_br_c'], proj_big)
    w_out = lp['w_out'].astype(BF16)

    x1s, h2s, logits, row = [], [], [], 0
    for x, mod in zip(x_sets, mods):
        b, L, _ = x.shape
        x1, h2, lg = mix_out(merged, row, x.reshape(b * L, D_MODEL), mod, L, w_out, lp['ln1_g'], lp['ln1_b'],
                             lp['w_router'], lp['b_router'])
        row += b * L
        x1s.append(x1)
        h2s.append(h2)
        logits.append(lg[:, :N_EXPERTS])
    ye, tables, offs = expert_choice_ffn(h2s, logits, lp['w_gate'], lp['w_up'], lp['w_down'])
    out_sets = []
    for x, x1, rm, m0, mod in zip(x_sets, x1s, tables, offs, mods):
        out = combine_post_norm(ye, m0, rm, x1, mod, x.shape[1], lp['ln2_g'], lp['ln2_b'])
        out_sets.append(out.reshape(x.shape))
    return out_sets, new_states


def kernel(x_prompt, x_sample, state_gdn, state_mlstm_c, state_mlstm_n, state_mlstm_m, c, c_ctx, w_ada, b_ada, w_in, gdn_conv, gdn_a_log, gdn_dt_bias, gdn_norm, ml_i_bias, ml_f_bias, ml_norm, hy_conv, hy_w1, hy_b1, hy_w2, hy_b2, hy_w3, hy_b3, hy_freq, hy_rate, hy_skip, w_br_a, w_br_b, w_br_c, w_out, ln1_g, ln1_b, ln2_g, ln2_b, w_router, b_router, w_gate, w_up, w_down):
    bp = x_prompt.shape[0]
    ls = x_sample.shape[1]
    xp = x_prompt
    xs = x_sample + grid_pos_embed(ls)[None]
    zero_states = (jnp.zeros((bp, 2, H_A, DK_A, DV_A), F32), jnp.zeros((bp, 2, H_B, DQK_B, DV_B), F32),
                   jnp.zeros((bp, 2, H_B, DQK_B), F32), jnp.zeros((bp, 2, H_B), F32))
    cond = jax.nn.silu(jnp.concatenate([c_ctx[None], c], 0))
    cond = jnp.pad(cond, ((0, 16 - cond.shape[0]), (0, 0)))
    ctx_states = []
    mats = {}
    for L in (x_prompt.shape[1], ls):
        m32 = dft_matrices(L)
        mats[L] = (m32, tuple(m.astype(BF16) for m in m32))
    stacked = dict(w_in=w_in, gdn_conv=gdn_conv, gdn_a_log=gdn_a_log, gdn_dt_bias=gdn_dt_bias, gdn_norm=gdn_norm,
                   ml_i_bias=ml_i_bias, ml_f_bias=ml_f_bias, ml_norm=ml_norm, hy_conv=hy_conv, hy_w1=hy_w1,
                   hy_b1=hy_b1, hy_w2=hy_w2, hy_b2=hy_b2, hy_w3=hy_w3, hy_b3=hy_b3, hy_freq=hy_freq,
                   hy_rate=hy_rate, hy_skip=hy_skip, w_br_a=w_br_a, w_br_b=w_br_b, w_br_c=w_br_c, w_out=w_out,
                   ln1_g=ln1_g, ln1_b=ln1_b, ln2_g=ln2_g, ln2_b=ln2_b, w_router=w_router, b_router=b_router,
                   w_gate=w_gate, w_up=w_up, w_down=w_down)
    for l in range(DEPTH):
        lp = {name: t[l] for name, t in stacked.items()}
        mod = pmm(cond, w_ada[l]) + b_ada[l]
        mod_ctx = mod[0].reshape(1, 1, 6 * D_MODEL)
        mod_lat = mod[1:1 + c.shape[0]][:, None, :]
        lat_states = (state_gdn[:, l], state_mlstm_c[:, l], state_mlstm_n[:, l], state_mlstm_m[:, l])
        (xp, xs), (st_ctx, _) = trunk_layer([xp, xs], [mod_ctx, mod_lat], lp, mats, [zero_states, lat_states])
        ctx_states.append(st_ctx)
    outs = tuple(jnp.stack([st[i] for st in ctx_states], 1) for i in range(4))
    return (xp, xs) + outs
```

```python
import functools
import math

import jax
import jax.numpy as jnp
import numpy as np
from jax import lax
from jax.experimental import pallas as pl
from jax.experimental.pallas import tpu as pltpu

D_MODEL = 2048
DEPTH = 4
GRID_W = 64
CHUNK = 64
H_A = 8
DK_A = 128
DV_A = 128
QK_A = H_A * DK_A
V_A = H_A * DV_A
H_B = 4
DQK_B = 128
DV_B = 256
QK_B = H_B * DQK_B
V_B = H_B * DV_B
HY_CH = 1024
HY_ORDER = 2
HY_BANDS = 16
N_EXPERTS = 16
D_EXPERT = 1024
CAP_FACTOR = 2
DN_ALPHA = (2 * DEPTH) ** 0.25
LN_EPS = 1e-6
F32 = jnp.float32
BF16 = jnp.bfloat16

LANES = 128
INTRA_GROUP = 4
ML_GROUP = 4
VMEM_LIMIT_BYTES = 48 * 1024 * 1024

COL_QA, COL_KA, COL_VA, COL_ZA = 0, QK_A, 2 * QK_A, 2 * QK_A + V_A
COL_QB = 2 * QK_A + 2 * V_A
COL_KB = COL_QB + QK_B
COL_VB = COL_KB + QK_B
COL_OB = COL_VB + V_B
COL_HY = COL_OB + V_B
COL_GA = COL_HY + 3 * HY_CH
COL_GB = COL_GA + D_MODEL
COL_GC = COL_GB + D_MODEL
N_BIG = COL_GC + D_MODEL


def _mm_kernel(x_ref, w_ref, o_ref):
    o_ref[...] = jnp.dot(x_ref[...], w_ref[...], preferred_element_type=F32).astype(o_ref.dtype)


def _pick(n, pref):
    for t in pref:
        if n % t == 0:
            return t
    return n


def pmm(x, w, out_dtype=F32):
    m, k = x.shape
    _, n = w.shape
    tm = _pick(m, (1024, 512, 256, 128, 16))
    tn = _pick(n, (512, 256, 128))
    return pl.pallas_call(
        _mm_kernel,
        grid=(m // tm, n // tn),
        in_specs=[pl.BlockSpec((tm, k), lambda i, j: (i, 0)),
                  pl.BlockSpec((k, tn), lambda i, j: (0, j))],
        out_specs=pl.BlockSpec((tm, tn), lambda i, j: (i, j)),
        out_shape=jax.ShapeDtypeStruct((m, n), out_dtype),
        compiler_params=pltpu.CompilerParams(
            dimension_semantics=("parallel", "parallel"), vmem_limit_bytes=VMEM_LIMIT_BYTES),
        name="pmm",
    )(x.astype(BF16), w.astype(BF16))


def pbmm(x, w, out_dtype=F32):
    e, m, k = x.shape
    _, _, n = w.shape
    tm = _pick(m, (1024, 512, 256, 128, 8))
    tn = _pick(n, (512, 256, 128))
    return pl.pallas_call(
        _mm_kernel,
        grid=(e, m // tm, n // tn),
        in_specs=[pl.BlockSpec((None, tm, k), lambda b, i, j: (b, i, 0)),
                  pl.BlockSpec((None, k, tn), lambda b, i, j: (b, 0, j))],
        out_specs=pl.BlockSpec((None, tm, tn), lambda b, i, j: (b, i, j)),
        out_shape=jax.ShapeDtypeStruct((e, m, n), out_dtype),
        compiler_params=pltpu.CompilerParams(
            dimension_semantics=("parallel", "parallel", "parallel"), vmem_limit_bytes=VMEM_LIMIT_BYTES),
        name="pbmm",
    )(x.astype(BF16), w.astype(BF16))


def _merge_kernel(ya_ref, yb_ref, yc_ref, wa_ref, wb_ref, wc_ref, ga_ref, gb_ref, gc_ref, o_ref):
    acc = jax.nn.sigmoid(ga_ref[...]) * jnp.dot(ya_ref[...], wa_ref[...], preferred_element_type=F32)
    acc = acc + jax.nn.sigmoid(gb_ref[...]) * jnp.dot(yb_ref[...], wb_ref[...], preferred_element_type=F32)
    acc = acc + jax.nn.sigmoid(gc_ref[...]) * jnp.dot(yc_ref[...], wc_ref[...], preferred_element_type=F32)
    o_ref[...] = acc.astype(o_ref.dtype)


def branch_merge(y_a, y_b, y_c, w_a, w_b, w_c, proj):
    m = y_a.shape[0]
    tm, tn = _pick(m, (1024, 512, 256)), 512
    y_spec = lambda k: pl.BlockSpec((tm, k), lambda i, j: (i, 0))
    w_spec = lambda k: pl.BlockSpec((k, tn), lambda i, j: (0, j))
    g_spec = lambda col: pl.BlockSpec((tm, tn), lambda i, j, col=col: (i, col // tn + j))
    return pl.pallas_call(
        _merge_kernel,
        grid=(m // tm, D_MODEL // tn),
        in_specs=[y_spec(V_A), y_spec(V_B), y_spec(HY_CH), w_spec(V_A), w_spec(V_B), w_spec(HY_CH),
                  g_spec(COL_GA), g_spec(COL_GB), g_spec(COL_GC)],
        out_specs=pl.BlockSpec((tm, tn), lambda i, j: (i, j)),
        out_shape=jax.ShapeDtypeStruct((m, D_MODEL), BF16),
        compiler_params=pltpu.CompilerParams(
            dimension_semantics=("parallel", "parallel"), vmem_limit_bytes=VMEM_LIMIT_BYTES),
        name="branch_merge",
    )(y_a, y_b, y_c, w_a.astype(BF16), w_b.astype(BF16), w_c.astype(BF16), proj, proj, proj)


def _expert_hidden_kernel(x_ref, wg_ref, wu_ref, o_ref, wg_s, wu_s):
    @pl.when(pl.program_id(2) == 0)
    def _():
        wg_s[...] = wg_ref[...].astype(BF16)
        wu_s[...] = wu_ref[...].astype(BF16)

    x = x_ref[...]
    g = jnp.dot(x, wg_s[...], preferred_element_type=F32)
    u = jnp.dot(x, wu_s[...], preferred_element_type=F32)
    o_ref[...] = (g * jax.nn.sigmoid(g) * u).astype(o_ref.dtype)


def expert_hidden(xe, w_gate, w_up):
    e, m, k = xe.shape
    f = w_gate.shape[2]
    tm, tn = _pick(m, (1280, 1024, 512, 256)), 256
    return pl.pallas_call(
        _expert_hidden_kernel,
        grid=(e, f // tn, m // tm),
        in_specs=[pl.BlockSpec((None, tm, k), lambda b, j, i: (b, i, 0)),
                  pl.BlockSpec((None, k, tn), lambda b, j, i: (b, 0, j)),
                  pl.BlockSpec((None, k, tn), lambda b, j, i: (b, 0, j))],
        out_specs=pl.BlockSpec((None, tm, tn), lambda b, j, i: (b, i, j)),
        out_shape=jax.ShapeDtypeStruct((e, m, f), BF16),
        scratch_shapes=[pltpu.VMEM((k, tn), BF16), pltpu.VMEM((k, tn), BF16)],
        compiler_params=pltpu.CompilerParams(
            dimension_semantics=("parallel", "parallel", "arbitrary"), vmem_limit_bytes=VMEM_LIMIT_BYTES),
        name="expert_hidden",
    )(xe, w_gate, w_up)


def _expert_down_kernel(h_ref, w_ref, g_ref, hi_ref, lo_ref, w_s):
    @pl.when(pl.program_id(2) == 0)
    def _():
        w_s[...] = w_ref[...].astype(BF16)

    y = jnp.dot(h_ref[...], w_s[...], preferred_element_type=F32) * g_ref[...]
    hi_ref[...], lo_ref[...] = _split(y)


def expert_down(hid, w_down, gate):
    e, m, k = hid.shape
    n = w_down.shape[2]
    tm, tn = _pick(m, (1280, 1024, 512, 256)), 512
    out = pl.BlockSpec((None, tm, tn), lambda b, j, i: (b, i, j))
    return pl.pallas_call(
        _expert_down_kernel,
        grid=(e, n // tn, m // tm),
        in_specs=[pl.BlockSpec((None, tm, k), lambda b, j, i: (b, i, 0)),
                  pl.BlockSpec((None, k, tn), lambda b, j, i: (b, 0, j)),
                  pl.BlockSpec((None, tm, 1), lambda b, j, i: (b, i, 0))],
        out_specs=[out, out],
        out_shape=[jax.ShapeDtypeStruct((e, m, n), BF16), jax.ShapeDtypeStruct((e, m, n), BF16)],
        scratch_shapes=[pltpu.VMEM((k, tn), BF16)],
        compiler_params=pltpu.CompilerParams(
            dimension_semantics=("parallel", "parallel", "arbitrary"), vmem_limit_bytes=VMEM_LIMIT_BYTES),
        name="expert_down",
    )(hid, w_down, gate)


ROW_TILE = 256


def _ln(v):
    mu = jnp.mean(v, axis=-1, keepdims=True)
    d = v - mu
    return d * lax.rsqrt(jnp.mean(d * d, axis=-1, keepdims=True) + LN_EPS)


def _mix_out_kernel(m_ref, w_ref, x_ref, g1_ref, sh2_ref, sc2_ref, lng_ref, lnb_ref, wr_ref, br_ref,
                    x1_ref, h2_ref, lg_ref):
    y = jnp.dot(m_ref[...], w_ref[...], preferred_element_type=F32)
    x1 = _ln(DN_ALPHA * x_ref[...] + (1.0 + g1_ref[...]) * y) * lng_ref[...] + lnb_ref[...]
    x1_ref[...] = x1
    h2 = _ln(x1) * (1.0 + sc2_ref[...]) + sh2_ref[...]
    h2_ref[...] = h2.astype(h2_ref.dtype)
    lg_ref[...] = _mm3(_nn, h2, wr_ref[...]) + br_ref[...]


def mix_out(merged, row0, x, mod, L, w_out, ln_g, ln_b, w_router, b_router):
    n = x.shape[0]
    nm = mod.shape[0]
    mod4 = mod.reshape(nm, 6, 1, D_MODEL)
    per_seq = L // ROW_TILE
    mod_spec = lambda part: pl.BlockSpec(
        (None, None, 1, D_MODEL), lambda i, part=part: (0 if nm == 1 else i // per_seq, part, 0, 0))
    vec = pl.BlockSpec((1, D_MODEL), lambda i: (0, 0))
    wr = jnp.pad(w_router, ((0, 0), (0, LANES - N_EXPERTS)))
    br = jnp.pad(b_router, (0, LANES - N_EXPERTS)).reshape(1, LANES)
    return pl.pallas_call(
        _mix_out_kernel,
        grid=(n // ROW_TILE,),
        in_specs=[pl.BlockSpec((ROW_TILE, D_MODEL), lambda i: (row0 // ROW_TILE + i, 0)),
                  pl.BlockSpec((D_MODEL, D_MODEL), lambda i: (0, 0)),
                  pl.BlockSpec((ROW_TILE, D_MODEL), lambda i: (i, 0)),
                  mod_spec(2), mod_spec(3), mod_spec(4), vec, vec,
                  pl.BlockSpec((D_MODEL, LANES), lambda i: (0, 0)),
                  pl.BlockSpec((1, LANES), lambda i: (0, 0))],
        out_specs=[pl.BlockSpec((ROW_TILE, D_MODEL), lambda i: (i, 0)),
                   pl.BlockSpec((ROW_TILE, D_MODEL), lambda i: (i, 0)),
                   pl.BlockSpec((ROW_TILE, LANES), lambda i: (i, 0))],
        out_shape=[jax.ShapeDtypeStruct((n, D_MODEL), F32), jax.ShapeDtypeStruct((n, D_MODEL), BF16),
                   jax.ShapeDtypeStruct((n, LANES), F32)],
        compiler_params=pltpu.CompilerParams(dimension_semantics=("parallel",), vmem_limit_bytes=VMEM_LIMIT_BYTES),
        name="mix_out",
    )(merged, w_out, x, mod4, mod4, mod4, ln_g.reshape(1, -1), ln_b.reshape(1, -1), wr, br)


def _post_norm_kernel(x_ref, y_ref, gate_ref, lng_ref, lnb_ref, o_ref):
    o_ref[...] = _ln(DN_ALPHA * x_ref[...] + (1.0 + gate_ref[...]) * y_ref[...]) * lng_ref[...] + lnb_ref[...]


def post_norm(x, y, mod, part, L, ln_g, ln_b):
    n = x.shape[0]
    nm = mod.shape[0]
    per_seq = L // ROW_TILE
    rows = pl.BlockSpec((ROW_TILE, D_MODEL), lambda i: (i, 0))
    vec = pl.BlockSpec((1, D_MODEL), lambda i: (0, 0))
    return pl.pallas_call(
        _post_norm_kernel,
        grid=(n // ROW_TILE,),
        in_specs=[rows, rows,
                  pl.BlockSpec((None, None, 1, D_MODEL), lambda i: (0 if nm == 1 else i // per_seq, part, 0, 0)),
                  vec, vec],
        out_specs=rows,
        out_shape=jax.ShapeDtypeStruct((n, D_MODEL), F32),
        compiler_params=pltpu.CompilerParams(dimension_semantics=("parallel",), vmem_limit_bytes=VMEM_LIMIT_BYTES),
        name="post_norm",
    )(x, y, mod.reshape(nm, 6, 1, D_MODEL), ln_g.reshape(1, -1), ln_b.reshape(1, -1))


def _nn(a, b):
    return jnp.dot(a, b, preferred_element_type=F32)


def _nt(a, b):
    return lax.dot_general(a, b, (((1,), (1,)), ((), ())), preferred_element_type=F32)


def _tn(a, b):
    return lax.dot_general(a, b, (((0,), (0,)), ((), ())), preferred_element_type=F32)


def _split(a):
    hi = a.astype(BF16)
    return hi, (a - hi.astype(F32)).astype(BF16)


def _mm1(f, a, b):
    return f(a.astype(BF16), b.astype(BF16))


def _mm3(f, a, b):
    ah, al = _split(a)
    bh, bl = _split(b)
    return f(ah, bh) + (f(ah, bl) + f(al, bh))


def _chunk_masks():
    ri = lax.broadcasted_iota(jnp.int32, (CHUNK, CHUNK), 0)
    ci = lax.broadcasted_iota(jnp.int32, (CHUNK, CHUNK), 1)
    eye = ri == ci
    incl = (ri >= ci, ri <= ci)
    strict = (ri > ci, ri < ci)
    return eye, incl, strict


def _to_col(eye, row):
    return jnp.sum(jnp.where(eye, row, 0.0), axis=1, keepdims=True)


GC_BETA, GC_ALPHA, GC_IG, GC_FG, GC_END = 0, 2 * H_A, 4 * H_A, 4 * H_A + 2 * H_B, 4 * H_A + 4 * H_B


def _softplus(x):
    return jnp.maximum(x, 0.0) + jnp.log(1.0 + jnp.exp(-jnp.abs(x)))


def _gate_kernel(x_ref, p_ref, tab_ref, scal_ref, val_s, *, L):
    n = L // CHUNK
    x = x_ref[...]
    p = p_ref[...]
    lane = lax.broadcasted_iota(jnp.int32, (L, LANES), 1)
    beta = jax.nn.sigmoid(x)
    loga = p[0:1] * _softplus(x + p[1:2])
    ig = x + p[2:3]
    lf = -_softplus(-(x + p[3:4]))
    val_s[...] = jnp.where(lane < GC_ALPHA, beta, jnp.where(lane < GC_IG, loga, jnp.where(lane < GC_FG, ig, lf)))

    ri = lax.broadcasted_iota(jnp.int32, (CHUNK, CHUNK), 0)
    ci = lax.broadcasted_iota(jnp.int32, (CHUNK, CHUNK), 1)
    lower = jnp.where(ri >= ci, 1.0, 0.0).astype(BF16)
    upper = jnp.where(ri <= ci, 1.0, 0.0).astype(BF16)
    ident = jnp.where(ri == ci, 1.0, 0.0).astype(BF16)
    cl = lax.broadcasted_iota(jnp.int32, (CHUNK, LANES), 1)
    cumulative = ((cl >= GC_ALPHA) & (cl < GC_IG)) | ((cl >= GC_FG) & (cl < GC_END))
    backward = ((cl >= GC_ALPHA + H_A) & (cl < GC_IG)) | (cl >= GC_FG + H_B)

    def split3(v):
        h1 = v.astype(BF16)
        r1 = v - h1.astype(F32)
        h2 = r1.astype(BF16)
        return h1, h2, (r1 - h2.astype(F32)).astype(BF16)

    def chunk(c, carry):
        v = val_s[pl.ds(pl.multiple_of(c * CHUNK, CHUNK), CHUNK), :]
        parts = split3(v)
        pre = sum(_nn(lower, h) for h in parts)
        suf = sum(_nn(upper, h) for h in parts)
        out = jnp.where(cumulative, jnp.where(backward, suf, pre), v)
        tot = jnp.where(backward[0:1], suf[0:1], pre[CHUNK - 1:CHUNK])
        wlog = tot - out + pltpu.roll(out, GC_FG - GC_IG, 1)
        wmax = jnp.max(wlog, axis=0, keepdims=True)
        tab_ref[c] = sum(_tn(h, ident) for h in split3(out))
        scal_ref[c] = jnp.concatenate([tot, wmax, jnp.zeros((6, LANES), F32)], axis=0)
        return carry

    lax.fori_loop(0, n, chunk, 0)


def gate_tables(proj_small, row_blk0, nb, L, lp):
    n = L // CHUNK
    rows = [jnp.pad(v.reshape(-1), (off, LANES - off - v.size)) for v, off in (
        (-jnp.exp(lp['gdn_a_log']), GC_ALPHA), (lp['gdn_dt_bias'], GC_ALPHA),
        (lp['ml_i_bias'], GC_IG), (lp['ml_f_bias'], GC_FG))]
    params = jnp.stack(rows + [jnp.zeros((LANES,), F32)] * 4)
    return pl.pallas_call(
        functools.partial(_gate_kernel, L=L),
        grid=(nb,),
        in_specs=[pl.BlockSpec((L, LANES), lambda b: (row_blk0 + b, 0)),
                  pl.BlockSpec((8, LANES), lambda b: (0, 0))],
        out_specs=[pl.BlockSpec((None, n, LANES, CHUNK), lambda b: (b, 0, 0, 0)),
                   pl.BlockSpec((None, n, 8, LANES), lambda b: (b, 0, 0, 0))],
        out_shape=[jax.ShapeDtypeStruct((nb, n, LANES, CHUNK), F32),
                   jax.ShapeDtypeStruct((nb, n, 8, LANES), F32)],
        scratch_shapes=[pltpu.VMEM((L, LANES), F32)],
        compiler_params=pltpu.CompilerParams(dimension_semantics=("parallel",), vmem_limit_bytes=VMEM_LIMIT_BYTES),
        name="gate_tables",
    )(proj_small, params)


def _gate_row(tab_ref, c, lane):
    return tab_ref[c, pl.ds(lane, 1), :]


def _gate_scalar(scal_ref, c, row, lane):
    v = scal_ref[c][row:row + 1, :]
    li = lax.broadcasted_iota(jnp.int32, (1, LANES), 1)
    return jnp.sum(jnp.where(li == lane, v, 0.0), axis=1, keepdims=True)


def _gdn_kernel(q_ref, k_ref, v_ref, z_ref, cw_ref, tab_ref, scal_ref, s0_ref, nw_ref, y_ref, sf_ref,
                qs, ks, vs, wq_s, u_s, kd_s, p_s, o_s, st_s, *, L):
    n = L // CHUNK
    head = pl.program_id(1)
    beta_lane = [GC_BETA + d * H_A + head for d in range(2)]
    g_lane = [GC_ALPHA + d * H_A + head for d in range(2)]
    row = lax.broadcasted_iota(jnp.int32, (L, LANES), 0)

    def conv_silu(x_ref, part):
        x = x_ref[...]
        w = cw_ref[part]
        x_prev = jnp.where(row == 0, 0.0, pltpu.roll(x, 1, 0))
        x_next = jnp.where(row == L - 1, 0.0, pltpu.roll(x, L - 1, 0))
        y = x_prev * w[0:1] + x * w[1:2] + x_next * w[2:3]
        return y * jax.nn.sigmoid(y)

    def l2n(x):
        return x * lax.rsqrt(jnp.sum(x * x, axis=-1, keepdims=True) + LN_EPS)

    qs[...] = l2n(conv_silu(q_ref, 0)) * (DK_A ** -0.5)
    ks[...] = l2n(conv_silu(k_ref, 1))
    vs[...] = conv_silu(v_ref, 2)

    eye, incl, strict = _chunk_masks()
    eye_f = jnp.where(eye, 1.0, 0.0)

    def intra(grp, carry):
        chains = []
        for j in range(INTRA_GROUP):
            c = grp * INTRA_GROUP + j
            r0 = pl.multiple_of(c * CHUNK, CHUNK)
            qc = qs[pl.ds(r0, CHUNK), :]
            kc = ks[pl.ds(r0, CHUNK), :]
            vc = vs[pl.ds(r0, CHUNK), :]
            kk = _mm3(_nt, kc, kc)
            qk = _mm1(_nt, qc, kc)
            for d in range(2):
                g_row = _gate_row(tab_ref, c, g_lane[d])
                b_row = _gate_row(tab_ref, c, beta_lane[d])
                gl_row = _gate_scalar(scal_ref, c, 0, g_lane[d])
                g_col = _to_col(eye, g_row)
                b_col = _to_col(eye, b_row)
                dec = jnp.exp(jnp.where(incl[d], g_col - g_row, -jnp.inf))
                lmat = b_col * kk * jnp.where(strict[d], dec, 0.0)
                kd_s[d, pl.ds(r0, CHUNK), :] = jnp.exp(gl_row - g_col) * kc
                wq_s[d, c, CHUNK:, :] = jnp.exp(g_col) * qc
                p_s[d, c] = qk * dec
                chains.append(dict(d=d, c=c, r0=r0, pw=lmat, tinv=eye_f - lmat,
                                   rhs_w=(b_col * jnp.exp(g_col)) * kc, rhs_u=b_col * vc))
        for _ in range(int(math.log2(CHUNK)) - 1):
            for ch in chains:
                ch['pw'] = _mm3(_nn, ch['pw'], ch['pw'])
            for ch in chains:
                ch['tinv'] = ch['tinv'] + _mm3(_nn, ch['tinv'], ch['pw'])
        for ch in chains:
            d, r0 = ch['d'], ch['r0']
            wq_s[d, ch['c'], :CHUNK, :] = _mm3(_nn, ch['tinv'], ch['rhs_w'])
            u_s[d, pl.ds(r0, CHUNK), :] = _mm3(_nn, ch['tinv'], ch['rhs_u'])
        return carry

    lax.fori_loop(0, n // INTRA_GROUP, intra, 0)

    st_s[...] = s0_ref[...]

    def scan(i, carry):
        cs = (i, n - 1 - i)
        r0s = [pl.multiple_of(c * CHUNK, CHUNK) for c in cs]
        s = [st_s[d] for d in range(2)]
        ws = [_mm1(_nn, wq_s[d, cs[d]], s[d]) for d in range(2)]
        uc = [u_s[d, pl.ds(r0s[d], CHUNK), :] - ws[d][:CHUNK] for d in range(2)]
        pu = [_mm1(_nn, p_s[d, cs[d]], uc[d]) for d in range(2)]
        ku = [_mm1(_tn, kd_s[d, pl.ds(r0s[d], CHUNK), :], uc[d]) for d in range(2)]
        for d in range(2):
            o_s[d, pl.ds(r0s[d], CHUNK), :] = ws[d][CHUNK:] + pu[d]
            st_s[d] = jnp.exp(_gate_scalar(scal_ref, cs[d], 0, g_lane[d])) * s[d] + ku[d]
        return carry

    lax.fori_loop(0, n, scan, 0)

    o = o_s[0] + o_s[1]
    z = z_ref[...]
    o = o * lax.rsqrt(jnp.mean(o * o, axis=-1, keepdims=True) + LN_EPS) * nw_ref[...]
    y_ref[...] = (o * (z * jax.nn.sigmoid(z))).astype(y_ref.dtype)
    sf_ref[...] = st_s[...]


def _gate_specs(n):
    return [pl.BlockSpec((None, n, LANES, CHUNK), lambda b, h: (b, 0, 0, 0)),
            pl.BlockSpec((None, n, 8, LANES), lambda b, h: (b, 0, 0, 0))]


def gdn_branch(proj, row_blk0, nb, L, gates, conv_w, s0, norm_w):
    n = L // CHUNK
    seq = lambda col: pl.BlockSpec((L, LANES), lambda b, h, col=col: (row_blk0 + b, col // LANES + h))
    scr = lambda *s: pltpu.VMEM(s, F32)
    return pl.pallas_call(
        functools.partial(_gdn_kernel, L=L),
        grid=(nb, H_A),
        in_specs=[seq(COL_QA), seq(COL_KA), seq(COL_VA), seq(COL_ZA),
                  pl.BlockSpec((3, None, 3, LANES), lambda b, h: (0, h, 0, 0)),
                  *_gate_specs(n),
                  pl.BlockSpec((None, 2, None, DK_A, DV_A), lambda b, h: (b, 0, h, 0, 0)),
                  pl.BlockSpec((1, LANES), lambda b, h: (0, 0))],
        out_specs=[pl.BlockSpec((L, LANES), lambda b, h: (b, h)),
                   pl.BlockSpec((None, 2, None, DK_A, DV_A), lambda b, h: (b, 0, h, 0, 0))],
        out_shape=[jax.ShapeDtypeStruct((nb * L, V_A), BF16),
                   jax.ShapeDtypeStruct((nb, 2, H_A, DK_A, DV_A), F32)],
        scratch_shapes=[scr(L, LANES), scr(L, LANES), scr(L, LANES),
                        scr(2, n, 2 * CHUNK, LANES), scr(2, L, LANES), scr(2, L, LANES),
                        scr(2, n, CHUNK, CHUNK), scr(2, L, LANES), scr(2, DK_A, DV_A)],
        compiler_params=pltpu.CompilerParams(
            dimension_semantics=("parallel", "parallel"), vmem_limit_bytes=VMEM_LIMIT_BYTES),
        name="gdn_branch",
    )(proj, proj, proj, proj, conv_w, *gates, s0, norm_w)


def _mlstm_kernel(q_ref, k_ref, v_ref, ob_ref, tab_ref, scal_ref, c0_ref, n0_ref, m0_ref, nw_ref,
                  y_ref, cf_ref, nf_ref, mf_ref, h_s, c_s, n_s, m_s, *, L):
    n = L // CHUNK
    head = pl.program_id(1)
    ig_lane = [GC_IG + d * H_B + head for d in range(2)]
    fg_lane = [GC_FG + d * H_B + head for d in range(2)]
    eye, incl, _ = _chunk_masks()
    c_s[...] = c0_ref[...]
    n_s[...] = n0_ref[...]
    m_s[...] = m0_ref[...]

    def body(grp, carry):
        ch = []
        for j in range(ML_GROUP):
            i = grp * ML_GROUP + j
            for d in range(2):
                c = i if d == 0 else n - 1 - i
                r0 = pl.multiple_of(c * CHUNK, CHUNK)
                ch.append(dict(d=d, r0=r0, qc=q_ref[pl.ds(r0, CHUNK), :],
                               kc=k_ref[pl.ds(r0, CHUNK), :] * (DQK_B ** -0.5), vc=v_ref[pl.ds(r0, CHUNK), :],
                               b_row=_gate_row(tab_ref, c, fg_lane[d]), i_row=_gate_row(tab_ref, c, ig_lane[d]),
                               bl_row=_gate_scalar(scal_ref, c, 0, fg_lane[d]),
                               wm_row=_gate_scalar(scal_ref, c, 1, fg_lane[d])))
        for x in ch:
            x['qk'] = _mm1(_nt, x['qc'], x['kc'])
        for x in ch:
            x['b_col'] = _to_col(eye, x['b_row'])
            i_col = _to_col(eye, x['i_row'])
            dlog = jnp.where(incl[x['d']], x['b_col'] - x['b_row'] + x['i_row'], -jnp.inf)
            x['dmax'] = jnp.max(dlog, axis=1, keepdims=True)
            x['pw'] = jnp.exp(dlog - x['dmax']) * x['qk']
            x['ewk'] = jnp.exp(x['bl_row'][:, :1] - x['b_col'] + i_col - x['wm_row'][:, :1]) * x['kc']
        for x in ch:
            x['intra_num'] = _mm1(_nn, x['pw'], x['vc'])
            x['dc'] = _mm1(_tn, x['ewk'], x['vc'])
        for x in ch:
            x['intra_den'] = jnp.sum(x['pw'], axis=1, keepdims=True)
            x['dn'] = jnp.sum(x['ewk'], axis=0, keepdims=True)
        for x in ch:
            d, r0 = x['d'], x['r0']
            cm, nm, mm = c_s[d], n_s[d], m_s[d]
            alog = x['b_col'] + mm
            mt = jnp.maximum(alog, x['dmax'])
            wi = jnp.exp(alog - mt)[:, :1]
            wa = jnp.exp(x['dmax'] - mt)[:, :1]
            num = wi * _mm1(_nn, x['qc'], cm) + wa * x['intra_num']
            den = wi * jnp.sum(x['qc'] * nm, axis=1, keepdims=True) + wa * x['intra_den']
            h_s[d, pl.ds(r0, CHUNK), :] = num / jnp.maximum(jnp.abs(den), jnp.exp(-mt[:, :1]))

            m_new = jnp.maximum(x['bl_row'] + mm, x['wm_row'])
            a = jnp.exp(x['bl_row'] + mm - m_new)
            e = jnp.exp(x['wm_row'] - m_new)
            c_s[d] = a[:, :1] * cm + e[:, :1] * x['dc']
            n_s[d] = a * nm + e * x['dn']
            m_s[d] = m_new
        return carry

    lax.fori_loop(0, n // ML_GROUP, body, 0)

    h = h_s[0] + h_s[1]
    h = h * lax.rsqrt(jnp.mean(h * h, axis=-1, keepdims=True) + LN_EPS) * nw_ref[...]
    y_ref[...] = (h * jax.nn.sigmoid(ob_ref[...])).astype(y_ref.dtype)
    cf_ref[...] = c_s[...]
    nf_ref[...] = n_s[...]
    mf_ref[...] = m_s[...]


def mlstm_branch(proj, row_blk0, nb, L, gates, c0, n0, m0, norm_w):
    n = L // CHUNK
    seq = lambda col, w: pl.BlockSpec((L, w), lambda b, h, col=col, w=w: (row_blk0 + b, col // w + h))
    st = lambda *s: pl.BlockSpec((None, 2, None) + s, lambda b, h: (b, 0, h, 0, 0))
    scr = lambda *s: pltpu.VMEM(s, F32)
    return pl.pallas_call(
        functools.partial(_mlstm_kernel, L=L),
        grid=(nb, H_B),
        in_specs=[seq(COL_QB, DQK_B), seq(COL_KB, DQK_B), seq(COL_VB, DV_B), seq(COL_OB, DV_B),
                  *_gate_specs(n),
                  st(DQK_B, DV_B), st(1, DQK_B), st(1, LANES),
                  pl.BlockSpec((1, DV_B), lambda b, h: (0, 0))],
        out_specs=[pl.BlockSpec((L, DV_B), lambda b, h: (b, h)),
                   st(DQK_B, DV_B), st(1, DQK_B), st(1, LANES)],
        out_shape=[jax.ShapeDtypeStruct((nb * L, V_B), BF16),
                   jax.ShapeDtypeStruct((nb, 2, H_B, DQK_B, DV_B), F32),
                   jax.ShapeDtypeStruct((nb, 2, H_B, 1, DQK_B), F32),
                   jax.ShapeDtypeStruct((nb, 2, H_B, 1, LANES), F32)],
        scratch_shapes=[scr(2, L, DV_B), scr(2, DQK_B, DV_B), scr(2, 1, DQK_B), scr(2, 1, LANES)],
        compiler_params=pltpu.CompilerParams(
            dimension_semantics=("parallel", "parallel"), vmem_limit_bytes=VMEM_LIMIT_BYTES),
        name="mlstm_branch",
    )(proj, proj, proj, proj, *gates, c0, n0, m0, norm_w)


def _ln_stats(x):
    mu = jnp.mean(x, -1, keepdims=True)
    var = jnp.mean(jnp.square(x - mu), -1, keepdims=True)
    return (x - mu) * lax.rsqrt(var + LN_EPS)


def centred_conv(x, w):
    k = w.shape[0]
    p = k // 2
    L = x.shape[1]
    xp = jnp.pad(x, ((0, 0), (p, p), (0, 0)))
    y = xp[:, 0:L] * w[0]
    for j in range(1, k):
        y = y + xp[:, j:j + L] * w[j]
    return y


def hyena_filters(L, w1, b1, w2, b2, w3, b3, freq, rate):
    pos = jnp.arange(L, dtype=F32)
    t = pos / L
    ang = (2.0 * math.pi) * t[:, None] * jnp.arange(1, HY_BANDS + 1, dtype=F32)
    feats = jnp.concatenate([t[:, None], jnp.sin(ang), jnp.cos(ang)], -1)
    hp = lax.Precision.HIGHEST
    z = jnp.sin(freq[0] * (jnp.dot(feats, w1, precision=hp) + b1))
    z = jnp.sin(freq[1] * (jnp.dot(z, w2, precision=hp) + b2))
    filt = jnp.dot(z, w3, precision=hp) + b3
    lag = jnp.abs(pos - L // 2) / L
    filt = filt * jnp.exp(-lag[:, None] * rate)
    return filt.reshape(L, HY_ORDER, HY_CH)


def dft_size(L):
    n = 3 * L // 2
    return n if (n // 2) % LANES == 0 else 2 * L


def dft_matrices(L):
    N = dft_size(L)
    F = N // 2
    k = jnp.arange(F, dtype=jnp.int32)
    t = jnp.arange(L, dtype=jnp.int32)
    w = 2.0 * math.pi / N

    def cos_sin(rows, cols):
        hi = jnp.arange(cols.shape[0] // LANES, dtype=jnp.int32) * LANES
        lo = jnp.arange(LANES, dtype=jnp.int32)
        a = w * ((rows[:, None] * hi[None, :]) % N).astype(F32)[:, :, None]
        b = w * ((rows[:, None] * lo[None, :]) % N).astype(F32)[:, None, :]
        ca, sa, cb, sb = jnp.cos(a), jnp.sin(a), jnp.cos(b), jnp.sin(b)
        shape = (rows.shape[0], cols.shape[0])
        return (ca * cb - sa * sb).reshape(shape), (sa * cb + ca * sb).reshape(shape)

    alt = (1 - 2 * (t % 2)).astype(F32)
    cf, sin_f = cos_sin(k, t)
    sf = jnp.where(k[:, None] == 0, alt[None, :], -sin_f)
    tt = t + L // 2
    alt_i = (1 - 2 * (tt % 2)).astype(F32)
    cos_i, sin_i = cos_sin(tt, k)
    ci = jnp.where(k[None, :] == 0, 1.0 / N, (2.0 / N) * cos_i)
    si = jnp.where(k[None, :] == 0, alt_i[:, None] / N, (-2.0 / N) * sin_i)
    return cf, sf, ci, si


def _mm3_kernel(x_ref, w_ref, o_ref):
    o_ref[...] = _mm3(_nn, x_ref[...], w_ref[...])


def pmm3(x, w):
    m, k = x.shape
    _, n = w.shape
    tm = _pick(m, (512, 256, 128))
    tn = _pick(n, (512, 256, 128))
    return pl.pallas_call(
        _mm3_kernel,
        grid=(m // tm, n // tn),
        in_specs=[pl.BlockSpec((tm, k), lambda i, j: (i, 0)),
                  pl.BlockSpec((k, tn), lambda i, j: (0, j))],
        out_specs=pl.BlockSpec((tm, tn), lambda i, j: (i, j)),
        out_shape=jax.ShapeDtypeStruct((m, n), F32),
        compiler_params=pltpu.CompilerParams(
            dimension_semantics=("parallel", "parallel"), vmem_limit_bytes=VMEM_LIMIT_BYTES),
        name="pmm3",
    )(x, w)


def _hyena_kernel(z_ref, un_ref, cwz_ref, cwu_ref, skip_ref, hr_ref, hi_ref, cf_ref, sf_ref, ci_ref, si_ref,
                  o_ref, z_s, zb_s, acc_s, *, L, ft, conv_z):
    f = pl.program_id(2)
    row = lax.broadcasted_iota(jnp.int32, (L, z_ref.shape[1]), 0)

    def conv3(x, w):
        x_prev = jnp.where(row == 0, 0.0, pltpu.roll(x, 1, 0))
        x_next = jnp.where(row == L - 1, 0.0, pltpu.roll(x, L - 1, 0))
        return x_prev * w[0:1] + x * w[1:2] + x_next * w[2:3]

    @pl.when(f == 0)
    def _():
        z = z_ref[...]
        if conv_z:
            z = conv3(z, cwz_ref[...])
        z_s[...] = z
        zb_s[...] = z.astype(BF16)

    zb = zb_s[...]
    zr = _nn(cf_ref[...], zb)
    zi = _nn(sf_ref[...], zb)
    hr = hr_ref[...]
    hi = hi_ref[...]
    packed = (f * ft + lax.broadcasted_iota(jnp.int32, zr.shape, 0)) == 0
    zihi = zi * hi
    yr = zr * hr - jnp.where(packed, 0.0, zihi)
    yi = jnp.where(packed, zihi, zr * hi + zi * hr)
    part = _nn(ci_ref[...], yr.astype(BF16)) + _nn(si_ref[...], yi.astype(BF16))

    @pl.when(f == 0)
    def _():
        acc_s[...] = part

    @pl.when(f > 0)
    def _():
        acc_s[...] += part

    @pl.when(f == pl.num_programs(2) - 1)
    def _():
        un = conv3(un_ref[...], cwu_ref[...])
        o_ref[...] = (un * (acc_s[...] + skip_ref[...] * z_s[...])).astype(o_ref.dtype)


def hyena_order(z_arr, z_blk0, z_col, conv_z, proj, row_blk0, un_col, nb, L, conv_w, cwz_col, skip,
                hr, hi, h_col, mats, out_dtype=F32):
    cf, sf, ci, si = mats
    F = cf.shape[0]
    cb = HY_CH if L * HY_CH * 4 <= (1 << 20) else 256
    ft = _pick(F, (512, 256))
    seq = lambda blk0, col: pl.BlockSpec((L, cb), lambda b, j, f: (blk0 + b, col // cb + j))
    return pl.pallas_call(
        functools.partial(_hyena_kernel, L=L, ft=ft, conv_z=conv_z),
        grid=(nb, HY_CH // cb, F // ft),
        in_specs=[seq(z_blk0, z_col), seq(row_blk0, un_col),
                  pl.BlockSpec((3, cb), lambda b, j, f: (0, cwz_col // cb + j)),
                  pl.BlockSpec((3, cb), lambda b, j, f: (0, (un_col - COL_HY) // cb + j)),
                  pl.BlockSpec((1, cb), lambda b, j, f: (0, j)),
                  pl.BlockSpec((ft, cb), lambda b, j, f: (f, h_col // cb + j)),
                  pl.BlockSpec((ft, cb), lambda b, j, f: (f, h_col // cb + j)),
                  pl.BlockSpec((ft, L), lambda b, j, f: (f, 0)),
                  pl.BlockSpec((ft, L), lambda b, j, f: (f, 0)),
                  pl.BlockSpec((L, ft), lambda b, j, f: (0, f)),
                  pl.BlockSpec((L, ft), lambda b, j, f: (0, f))],
        out_specs=pl.BlockSpec((L, cb), lambda b, j, f: (b, j)),
        out_shape=jax.ShapeDtypeStruct((nb * L, HY_CH), out_dtype),
        scratch_shapes=[pltpu.VMEM((L, cb), F32), pltpu.VMEM((L, cb), BF16), pltpu.VMEM((L, cb), F32)],
        compiler_params=pltpu.CompilerParams(
            dimension_semantics=("parallel", "parallel", "arbitrary"), vmem_limit_bytes=VMEM_LIMIT_BYTES),
        name="hyena_order",
    )(z_arr, proj, conv_w, conv_w, skip, hr, hi, cf, sf, ci, si)


def grid_pos_embed(n_tokens):
    rows = n_tokens // GRID_W
    quarter = D_MODEL // 4
    omega = 1.0 / (10000.0 ** (jnp.arange(quarter, dtype=F32) / quarter))
    r = jnp.arange(rows, dtype=F32)[:, None] * omega
    cl = jnp.arange(GRID_W, dtype=F32)[:, None] * omega
    full = lambda v, axis: jnp.broadcast_to(jnp.expand_dims(v, axis), (rows, GRID_W, quarter))
    pe = jnp.concatenate([full(jnp.sin(r), 1), full(jnp.cos(r), 1), full(jnp.sin(cl), 0), full(jnp.cos(cl), 0)], -1)
    return pe.reshape(rows * GRID_W, D_MODEL)


def seq_mixers(proj_big, proj_small, row0, nb, L, lp, mats, s_gdn, s_c, s_n, s_m):
    blk0 = row0 // L
    gates = gate_tables(proj_small, blk0, nb, L, lp)

    conv_w = lp['gdn_conv'].reshape(3, 3, H_A, DK_A).transpose(1, 2, 0, 3)
    y_a, s_gdn_new = gdn_branch(proj_big, blk0, nb, L, gates, conv_w, s_gdn, lp['gdn_norm'].reshape(1, DV_A))

    m0 = jnp.broadcast_to(s_m[..., None, None], s_m.shape + (1, LANES))
    y_b, c_new, n_new, m_new = mlstm_branch(proj_big, blk0, nb, L, gates, s_c, s_n[..., None, :], m0,
                                            lp['ml_norm'].reshape(1, DV_B))

    filt = hyena_filters(L, lp['hy_w1'], lp['hy_b1'], lp['hy_w2'], lp['hy_b2'], lp['hy_w3'], lp['hy_b3'],
                         lp['hy_freq'], lp['hy_rate']).reshape(L, HY_ORDER * HY_CH)
    mats32, mats16 = mats[L]
    hr, hi = pmm3(mats32[0], filt), pmm3(mats32[1], filt)
    z_arr, z_blk0, z_col = proj_big, blk0, COL_HY
    for order in range(HY_ORDER):
        z_arr = hyena_order(z_arr, z_blk0, z_col, order == 0, proj_big, blk0, COL_HY + (order + 1) * HY_CH, nb, L,
                            lp['hy_conv'], 0, lp['hy_skip'][order:order + 1], hr, hi, order * HY_CH, mats16,
                            out_dtype=BF16 if order == HY_ORDER - 1 else F32)
        z_blk0, z_col = 0, 0
    return (y_a, y_b, z_arr), (s_gdn_new, c_new, n_new[..., 0, :], m_new[..., 0, 0])


TOK_BLK = LANES
COMBINE_ROWS = 48
ROW_ALIGN = 16


def _route_kernel(a_ref, rm_ref, idx_ref, th_s, *, nblk, cap):
    bits = pltpu.bitcast(a_ref[...], jnp.int32)

    def search(i, th):
        cand = th | jnp.left_shift(jnp.int32(1), 30 - i)
        cnt = jnp.sum(jnp.sum(jnp.where(bits >= cand, 1.0, 0.0), axis=2, keepdims=True), axis=1, keepdims=True)
        return jnp.where(cnt >= cap, cand, th)

    th_s[...] = lax.fori_loop(0, 31, search, jnp.zeros((N_EXPERTS, 1, LANES), jnp.int32))

    li = lax.broadcasted_iota(jnp.int32, (LANES, LANES), 0)
    lj = lax.broadcasted_iota(jnp.int32, (LANES, LANES), 1)
    before_lane = jnp.where(li < lj, 1.0, 0.0).astype(BF16)
    bi = lax.broadcasted_iota(jnp.int32, (nblk, nblk), 0)
    bj = lax.broadcasted_iota(jnp.int32, (nblk, nblk), 1)
    before_blk = jnp.where(bj < bi, 1.0, 0.0).astype(BF16)
    diag = lax.broadcasted_iota(jnp.int32, (nblk, LANES), 0) == lax.broadcasted_iota(jnp.int32, (nblk, LANES), 1)
    slot = lax.broadcasted_iota(jnp.int32, (cap, LANES), 0).astype(F32)
    lane = lax.broadcasted_iota(jnp.int32, (cap, LANES), 1)
    lane_f = lane.astype(F32)

    def ranks(mask):
        m = mask.astype(BF16)
        inside = _nn(m, before_lane)
        cnt = jnp.sum(mask, axis=1, keepdims=True)
        earlier = _nn(before_blk, jnp.broadcast_to(cnt, (nblk, LANES)).astype(BF16))
        return inside, earlier, cnt

    def per_expert(e, carry):
        b = pltpu.bitcast(a_ref[e], jnp.int32)
        th = th_s[e]
        gt = jnp.where(b > th, 1.0, 0.0)
        eq = jnp.where(b == th, 1.0, 0.0)
        need = cap - jnp.sum(jnp.sum(gt, axis=1, keepdims=True), axis=0, keepdims=True)
        eq_in, eq_before, _ = ranks(eq)
        sel = jnp.maximum(gt, jnp.where(eq_in + eq_before < need, eq, 0.0))
        inside, earlier, cnt = ranks(sel)
        rm = jnp.where(sel > 0.0, inside, -1.0)
        rm_ref[e] = rm
        end_row = jnp.sum(jnp.where(diag, earlier + cnt, 0.0), axis=0, keepdims=True)
        cnt_row = jnp.sum(jnp.where(diag, jnp.broadcast_to(cnt, (nblk, LANES)), 0.0), axis=0, keepdims=True)
        done = jnp.where((lane < nblk) & (end_row <= slot), 1.0, 0.0)
        blk = jnp.sum(done, axis=1, keepdims=True)
        local = slot[:, :1] - jnp.sum(done * cnt_row, axis=1, keepdims=True)
        onehot = jnp.where(lane_f == blk, 1.0, 0.0).astype(BF16)
        rows = _nn(onehot[:, :nblk], rm.astype(BF16))
        tok_lane = jnp.sum(jnp.where(rows == local, lane_f, 0.0), axis=1, keepdims=True)
        idx_ref[e] = (blk * TOK_BLK + tok_lane).astype(jnp.int32)
        return carry

    lax.fori_loop(0, N_EXPERTS, per_expert, 0)


def route(aff_t):
    e, n = aff_t.shape
    nblk, cap = n // TOK_BLK, CAP_FACTOR * n // N_EXPERTS
    full = lambda *s: pl.BlockSpec(s, lambda i: (0,) * len(s))
    rm, idx = pl.pallas_call(
        functools.partial(_route_kernel, nblk=nblk, cap=cap),
        grid=(1,),
        in_specs=[full(e, nblk, TOK_BLK)],
        out_specs=[full(e, nblk, TOK_BLK), full(e, cap, 1)],
        out_shape=[jax.ShapeDtypeStruct((e, nblk, TOK_BLK), F32), jax.ShapeDtypeStruct((e, cap, 1), jnp.int32)],
        scratch_shapes=[pltpu.VMEM((e, 1, LANES), jnp.int32)],
        compiler_params=pltpu.CompilerParams(dimension_semantics=("arbitrary",), vmem_limit_bytes=VMEM_LIMIT_BYTES),
        name="route",
    )(aff_t.reshape(e, nblk, TOK_BLK))
    return rm, idx[..., 0]


def _combine_kernel(base_ref, cnt_ref, rm_ref, x_ref, gate_ref, lng_ref, lnb_ref, hi_ref, lo_ref, o_ref,
                    buf_hi, buf_lo, sem, xbuf_hi, xbuf_lo, xsem, acc, *, m_rows, nblk):
    b = pl.program_id(0)
    slot = b % 2
    w_rows = COMBINE_ROWS
    r_iota = lax.broadcasted_iota(jnp.int32, (w_rows, LANES), 0).astype(F32)

    def first_row(blk, e):
        return (base_ref[blk * N_EXPERTS + e] // ROW_ALIGN) * ROW_ALIGN

    def window(blk, e, w):
        return pl.multiple_of(jnp.minimum(first_row(blk, e) + w * w_rows, m_rows - w_rows), ROW_ALIGN)

    def copies(blk, e, s):
        start = window(blk, e, 0)
        return (pltpu.make_async_copy(hi_ref.at[e, pl.ds(start, w_rows), :], buf_hi.at[s, e], sem.at[s, 0, e]),
                pltpu.make_async_copy(lo_ref.at[e, pl.ds(start, w_rows), :], buf_lo.at[s, e], sem.at[s, 1, e]))

    def issue(blk, s):
        for e in range(N_EXPERTS):
            for cp in copies(blk, e, s):
                cp.start()

    @pl.when(b == 0)
    def _():
        issue(0, 0)

    @pl.when(b + 1 < nblk)
    def _():
        issue(b + 1, 1 - slot)

    def placement(e, w, start):
        rank = rm_ref[e:e + 1, :]
        rel = rank + (base_ref[b * N_EXPERTS + e] - first_row(b, e)).astype(F32)
        lo_row = w * jnp.float32(w_rows)
        mine = (rank >= 0.0) & (rel >= lo_row) & (rel < lo_row + w_rows)
        target = jnp.where(mine, rel + (first_row(b, e) - start).astype(F32), -1.0)
        return jnp.where(r_iota == target, 1.0, 0.0).astype(BF16)

    for e in range(N_EXPERTS):
        for cp in copies(b, e, slot):
            cp.wait()
    put = jnp.concatenate([placement(e, 0, window(b, e, 0)) for e in range(N_EXPERTS)], axis=0)
    rows_hi = buf_hi[slot].reshape(N_EXPERTS * w_rows, -1)
    rows_lo = buf_lo[slot].reshape(N_EXPERTS * w_rows, -1)
    acc[...] = _tn(put, rows_hi) + _tn(put, rows_lo)

    for e in range(N_EXPERTS):
        used = base_ref[b * N_EXPERTS + e] - first_row(b, e) + cnt_ref[b * N_EXPERTS + e]

        def extra(w, carry, e=e):
            s = window(b, e, w)
            c_hi = pltpu.make_async_copy(hi_ref.at[e, pl.ds(s, w_rows), :], xbuf_hi, xsem.at[0])
            c_lo = pltpu.make_async_copy(lo_ref.at[e, pl.ds(s, w_rows), :], xbuf_lo, xsem.at[1])
            c_hi.start()
            c_lo.start()
            c_hi.wait()
            c_lo.wait()
            p = placement(e, w, s)
            acc[...] += _tn(p, xbuf_hi[...]) + _tn(p, xbuf_lo[...])
            return carry

        lax.fori_loop(1, (used + w_rows - 1) // w_rows, extra, 0)

    o_ref[...] = _ln(DN_ALPHA * x_ref[...] + (1.0 + gate_ref[...]) * acc[...]) * lng_ref[...] + lnb_ref[...]


def combine_post_norm(ye_hi, ye_lo, m0, rm, x1, mod, L, ln_g, ln_b):
    e, m_rows, d = ye_hi.shape
    n = x1.shape[0]
    nblk = n // TOK_BLK
    nm = mod.shape[0]
    per_seq = L // TOK_BLK
    rm_t = rm.transpose(1, 0, 2)
    cnt = jnp.sum(rm_t >= 0.0, axis=-1).astype(jnp.int32)
    base = m0 + jnp.cumsum(cnt, axis=0) - cnt
    rows = pl.BlockSpec((TOK_BLK, d), lambda i, *_: (i, 0))
    vec = pl.BlockSpec((1, d), lambda i, *_: (0, 0))
    return pl.pallas_call(
        functools.partial(_combine_kernel, m_rows=m_rows, nblk=nblk),
        grid_spec=pltpu.PrefetchScalarGridSpec(
            num_scalar_prefetch=2,
            grid=(nblk,),
            in_specs=[pl.BlockSpec((None, e, TOK_BLK), lambda i, *_: (i, 0, 0)),
                      rows,
                      pl.BlockSpec((None, None, 1, d), lambda i, *_: (0 if nm == 1 else i // per_seq, 5, 0, 0)),
                      vec, vec,
                      pl.BlockSpec(memory_space=pl.ANY), pl.BlockSpec(memory_space=pl.ANY)],
            out_specs=rows,
            scratch_shapes=[pltpu.VMEM((2, e, COMBINE_ROWS, d), BF16), pltpu.VMEM((2, e, COMBINE_ROWS, d), BF16),
                            pltpu.SemaphoreType.DMA((2, 2, e)),
                            pltpu.VMEM((COMBINE_ROWS, d), BF16), pltpu.VMEM((COMBINE_ROWS, d), BF16),
                            pltpu.SemaphoreType.DMA((2,)),
                            pltpu.VMEM((TOK_BLK, d), F32)]),
        out_shape=jax.ShapeDtypeStruct((n, d), F32),
        compiler_params=pltpu.CompilerParams(dimension_semantics=("arbitrary",), vmem_limit_bytes=VMEM_LIMIT_BYTES),
        name="combine_post_norm",
    )(base.reshape(-1), cnt.reshape(-1), rm_t, x1, mod.reshape(nm, 6, 1, d), ln_g.reshape(1, -1),
      ln_b.reshape(1, -1), ye_hi, ye_lo)


def expert_choice_ffn(hfs, logits, w_gate, w_up, w_down):
    xes, gates, tables, offs, off = [], [], [], [], 0
    for hf, lg in zip(hfs, logits):
        aff_t = jax.nn.softmax(lg, axis=-1).T
        rm, idx = route(aff_t)
        xes.append(hf[idx])
        gates.append(jnp.take_along_axis(aff_t, idx, axis=1))
        tables.append(rm)
        offs.append(off)
        off += idx.shape[1]
    gate = jnp.concatenate(gates, axis=1)[..., None]
    ye_hi, ye_lo = expert_down(expert_hidden(jnp.concatenate(xes, axis=1), w_gate, w_up), w_down, gate)
    return ye_hi, ye_lo, tables, offs


_IN_SIZES = (QK_A, QK_A, V_A, V_A, 2 * H_A, 2 * H_A, QK_B, QK_B, V_B, V_B, 2 * H_B, 2 * H_B,
             3 * HY_CH, D_MODEL, D_MODEL, D_MODEL)
_IN_OFFS = tuple(int(v) for v in np.concatenate([[0], np.cumsum(_IN_SIZES)]))
_SMALL_PARTS = (4, 5, 10, 11)


def trunk_layer(x_sets, mods, lp, mats, states):
    hs = []
    for x, mod in zip(x_sets, mods):
        shift1, scale1 = mod[..., :D_MODEL], mod[..., D_MODEL:2 * D_MODEL]
        hs.append((_ln_stats(x) * (1.0 + scale1) + shift1).reshape(-1, D_MODEL))
    h_all = jnp.concatenate(hs, 0)

    big_cols = [p for p in range(len(_IN_SIZES)) if p not in _SMALL_PARTS]
    w_in = lp['w_in']
    w_big = jnp.concatenate([w_in[:, _IN_OFFS[p]:_IN_OFFS[p + 1]].astype(BF16) for p in big_cols], -1)
    w_small = jnp.concatenate([w_in[:, _IN_OFFS[p]:_IN_OFFS[p + 1]] for p in _SMALL_PARTS], -1)
    w_small = jnp.pad(w_small, ((0, 0), (0, LANES - w_small.shape[1])))
    proj_big = pmm(h_all, w_big)
    proj_small = pmm(h_all, w_small)

    mixed, new_states, row = [], [], 0
    for x, st in zip(x_sets, states):
        b, L, _ = x.shape
        outs, st_new = seq_mixers(proj_big, proj_small, row, b, L, lp, mats, *st)
        row += b * L
        mixed.append(outs)
        new_states.append(st_new)

    y_a, y_b, y_c = (jnp.concatenate(ts, 0) for ts in zip(*mixed))
    merged = branch_merge(y_a, y_b, y_c, lp['w_br_a'], lp['w_br_b'], lp['w_br_c'], proj_big)
    w_out = lp['w_out'].astype(BF16)

    x1s, h2s, logits, row = [], [], [], 0
    for x, mod in zip(x_sets, mods):
        b, L, _ = x.shape
        x1, h2, lg = mix_out(merged, row, x.reshape(b * L, D_MODEL), mod, L, w_out, lp['ln1_g'], lp['ln1_b'],
                             lp['w_router'], lp['b_router'])
        row += b * L
        x1s.append(x1)
        h2s.append(h2)
        logits.append(lg[:, :N_EXPERTS])
    ye_hi, ye_lo, tables, offs = expert_choice_ffn(h2s, logits, lp['w_gate'], lp['w_up'], lp['w_down'])
    out_sets = []
    for x, x1, rm, m0, mod in zip(x_sets, x1s, tables, offs, mods):
        out = combine_post_norm(ye_hi, ye_lo, m0, rm, x1, mod, x.shape[1], lp['ln2_g'], lp['ln2_b'])
        out_sets.append(out.reshape(x.shape))
    return out_sets, new_states


def kernel(x_prompt, x_sample, state_gdn, state_mlstm_c, state_mlstm_n, state_mlstm_m, c, c_ctx, w_ada, b_ada, w_in, gdn_conv, gdn_a_log, gdn_dt_bias, gdn_norm, ml_i_bias, ml_f_bias, ml_norm, hy_conv, hy_w1, hy_b1, hy_w2, hy_b2, hy_w3, hy_b3, hy_freq, hy_rate, hy_skip, w_br_a, w_br_b, w_br_c, w_out, ln1_g, ln1_b, ln2_g, ln2_b, w_router, b_router, w_gate, w_up, w_down):
    bp = x_prompt.shape[0]
    ls = x_sample.shape[1]
    xp = x_prompt
    xs = x_sample + grid_pos_embed(ls)[None]
    zero_states = (jnp.zeros((bp, 2, H_A, DK_A, DV_A), F32), jnp.zeros((bp, 2, H_B, DQK_B, DV_B), F32),
                   jnp.zeros((bp, 2, H_B, DQK_B), F32), jnp.zeros((bp, 2, H_B), F32))
    cond = jax.nn.silu(jnp.concatenate([c_ctx[None], c], 0))
    cond = jnp.pad(cond, ((0, 16 - cond.shape[0]), (0, 0)))
    ctx_states = []
    mats = {}
    for L in (x_prompt.shape[1], ls):
        m32 = dft_matrices(L)
        mats[L] = (m32, tuple(m.astype(BF16) for m in m32))
    stacked = dict(w_in=w_in, gdn_conv=gdn_conv, gdn_a_log=gdn_a_log, gdn_dt_bias=gdn_dt_bias, gdn_norm=gdn_norm,
                   ml_i_bias=ml_i_bias, ml_f_bias=ml_f_bias, ml_norm=ml_norm, hy_conv=hy_conv, hy_w1=hy_w1,
                   hy_b1=hy_b1, hy_w2=hy_w2, hy_b2=hy_b2, hy_w3=hy_w3, hy_b3=hy_b3, hy_freq=hy_freq,
                   hy_rate=hy_rate, hy_skip=hy_skip, w_br_a=w_br_a, w_br_b=w_br_b, w_br_c=w_br_c, w_out=w_out,
                   ln1_g=ln1_g, ln1_b=ln1_b, ln2_g=ln2_g, ln2_b=ln2_b, w_router=w_router, b_router=b_router,
                   w_gate=w_gate, w_up=w_up, w_down=w_down)
    for l in range(DEPTH):
        lp = {name: t[l] for name, t in stacked.items()}
        mod = pmm(cond, w_ada[l]) + b_ada[l]
        mod_ctx = mod[0].reshape(1, 1, 6 * D_MODEL)
        mod_lat = mod[1:1 + c.shape[0]][:, None, :]
        lat_states = (state_gdn[:, l], state_mlstm_c[:, l], state_mlstm_n[:, l], state_mlstm_m[:, l])
        (xp, xs), (st_ctx, _) = trunk_layer([xp, xs], [mod_ctx, mod_lat], lp, mats, [zero_states, lat_states])
        ctx_states.append(st_ctx)
    outs = tuple(jnp.stack([st[i] for st in ctx_states], 1) for i in range(4))
    return (xp, xs) + outs
```

```python
import functools
import math

import jax
import jax.numpy as jnp
import numpy as np
from jax import lax
from jax.experimental import pallas as pl
from jax.experimental.pallas import tpu as pltpu

D_MODEL = 2048
DEPTH = 4
GRID_W = 64
CHUNK = 64
H_A = 8
DK_A = 128
DV_A = 128
QK_A = H_A * DK_A
V_A = H_A * DV_A
H_B = 4
DQK_B = 128
DV_B = 256
QK_B = H_B * DQK_B
V_B = H_B * DV_B
HY_CH = 1024
HY_ORDER = 2
HY_BANDS = 16
N_EXPERTS = 16
D_EXPERT = 1024
CAP_FACTOR = 2
DN_ALPHA = (2 * DEPTH) ** 0.25
LN_EPS = 1e-6
F32 = jnp.float32
BF16 = jnp.bfloat16

LANES = 128
INTRA_GROUP = 4
ML_GROUP = 4
VMEM_LIMIT_BYTES = 48 * 1024 * 1024
VMEM_LIMIT_LARGE_BYTES = 56 * 1024 * 1024

COL_QA, COL_KA, COL_VA, COL_ZA = 0, QK_A, 2 * QK_A, 2 * QK_A + V_A
COL_QB = 2 * QK_A + 2 * V_A
COL_KB = COL_QB + QK_B
COL_VB = COL_KB + QK_B
COL_OB = COL_VB + V_B
COL_HY = COL_OB + V_B
COL_GA = COL_HY + 3 * HY_CH
COL_GB = COL_GA + D_MODEL
COL_GC = COL_GB + D_MODEL
N_BIG = COL_GC + D_MODEL


def _mm_kernel(x_ref, w_ref, o_ref):
    o_ref[...] = jnp.dot(x_ref[...], w_ref[...], preferred_element_type=F32).astype(o_ref.dtype)


def _pick(n, pref):
    for t in pref:
        if n % t == 0:
            return t
    return n


def pmm(x, w, out_dtype=F32):
    m, k = x.shape
    _, n = w.shape
    tm = _pick(m, (1024, 512, 256, 128, 16))
    tn = _pick(n, (512, 256, 128))
    return pl.pallas_call(
        _mm_kernel,
        grid=(m // tm, n // tn),
        in_specs=[pl.BlockSpec((tm, k), lambda i, j: (i, 0)),
                  pl.BlockSpec((k, tn), lambda i, j: (0, j))],
        out_specs=pl.BlockSpec((tm, tn), lambda i, j: (i, j)),
        out_shape=jax.ShapeDtypeStruct((m, n), out_dtype),
        compiler_params=pltpu.CompilerParams(
            dimension_semantics=("parallel", "parallel"), vmem_limit_bytes=VMEM_LIMIT_BYTES),
        name="pmm",
    )(x.astype(BF16), w.astype(BF16))


def _ada_kernel(x_ref, w_ref, b_ref, o_ref):
    o_ref[...] = jnp.dot(x_ref[...], w_ref[...].astype(BF16), preferred_element_type=F32) + b_ref[...]


def ada_modulation(cond, w_ada, b_ada, layer):
    m, k = cond.shape
    n = w_ada.shape[2]
    tn = 512
    return pl.pallas_call(
        _ada_kernel,
        grid=(n // tn,),
        in_specs=[pl.BlockSpec((m, k), lambda j: (0, 0)),
                  pl.BlockSpec((None, k, tn), lambda j: (layer, 0, j)),
                  pl.BlockSpec((None, 1, tn), lambda j: (layer, 0, j))],
        out_specs=pl.BlockSpec((m, tn), lambda j: (0, j)),
        out_shape=jax.ShapeDtypeStruct((m, n), F32),
        compiler_params=pltpu.CompilerParams(dimension_semantics=("parallel",), vmem_limit_bytes=VMEM_LIMIT_BYTES),
        name="ada_modulation",
    )(cond, w_ada, b_ada.reshape(b_ada.shape[0], 1, n))


def pbmm(x, w, out_dtype=F32):
    e, m, k = x.shape
    _, _, n = w.shape
    tm = _pick(m, (1024, 512, 256, 128, 8))
    tn = _pick(n, (512, 256, 128))
    return pl.pallas_call(
        _mm_kernel,
        grid=(e, m // tm, n // tn),
        in_specs=[pl.BlockSpec((None, tm, k), lambda b, i, j: (b, i, 0)),
                  pl.BlockSpec((None, k, tn), lambda b, i, j: (b, 0, j))],
        out_specs=pl.BlockSpec((None, tm, tn), lambda b, i, j: (b, i, j)),
        out_shape=jax.ShapeDtypeStruct((e, m, n), out_dtype),
        compiler_params=pltpu.CompilerParams(
            dimension_semantics=("parallel", "parallel", "parallel"), vmem_limit_bytes=VMEM_LIMIT_BYTES),
        name="pbmm",
    )(x.astype(BF16), w.astype(BF16))


def _merge_kernel(ya_ref, yb_ref, yc_ref, wa_ref, wb_ref, wc_ref, ga_ref, gb_ref, gc_ref, o_ref):
    acc = jax.nn.sigmoid(ga_ref[...]) * jnp.dot(ya_ref[...], wa_ref[...], preferred_element_type=F32)
    acc = acc + jax.nn.sigmoid(gb_ref[...]) * jnp.dot(yb_ref[...], wb_ref[...], preferred_element_type=F32)
    acc = acc + jax.nn.sigmoid(gc_ref[...]) * jnp.dot(yc_ref[...], wc_ref[...], preferred_element_type=F32)
    o_ref[...] = acc.astype(o_ref.dtype)


def branch_merge(y_a, y_b, y_c, w_a, w_b, w_c, proj):
    m = y_a.shape[0]
    tm, tn = _pick(m, (1024, 512, 256)), 512
    y_spec = lambda k: pl.BlockSpec((tm, k), lambda i, j: (i, 0))
    w_spec = lambda k: pl.BlockSpec((k, tn), lambda i, j: (0, j))
    g_spec = lambda col: pl.BlockSpec((tm, tn), lambda i, j, col=col: (i, col // tn + j))
    return pl.pallas_call(
        _merge_kernel,
        grid=(m // tm, D_MODEL // tn),
        in_specs=[y_spec(V_A), y_spec(V_B), y_spec(HY_CH), w_spec(V_A), w_spec(V_B), w_spec(HY_CH),
                  g_spec(COL_GA), g_spec(COL_GB), g_spec(COL_GC)],
        out_specs=pl.BlockSpec((tm, tn), lambda i, j: (i, j)),
        out_shape=jax.ShapeDtypeStruct((m, D_MODEL), BF16),
        compiler_params=pltpu.CompilerParams(
            dimension_semantics=("parallel", "parallel"), vmem_limit_bytes=VMEM_LIMIT_BYTES),
        name="branch_merge",
    )(y_a, y_b, y_c, w_a.astype(BF16), w_b.astype(BF16), w_c.astype(BF16), proj, proj, proj)


def _expert_hidden_kernel(x_ref, wg_ref, wu_ref, o_ref, wg_s, wu_s):
    @pl.when(pl.program_id(2) == 0)
    def _():
        wg_s[...] = wg_ref[...].astype(BF16)
        wu_s[...] = wu_ref[...].astype(BF16)

    x = x_ref[...]
    g = jnp.dot(x, wg_s[...], preferred_element_type=F32)
    u = jnp.dot(x, wu_s[...], preferred_element_type=F32)
    o_ref[...] = (g * jax.nn.sigmoid(g) * u).astype(o_ref.dtype)


def expert_hidden(xe, w_gate, w_up, layer):
    e, m, k = xe.shape
    f = w_gate.shape[3]
    tm, tn = _pick(m, (1280, 1024, 512, 256)), 512
    return pl.pallas_call(
        _expert_hidden_kernel,
        grid=(e, f // tn, m // tm),
        in_specs=[pl.BlockSpec((None, tm, k), lambda b, j, i: (b, i, 0)),
                  pl.BlockSpec((None, None, k, tn), lambda b, j, i: (layer, b, 0, j)),
                  pl.BlockSpec((None, None, k, tn), lambda b, j, i: (layer, b, 0, j))],
        out_specs=pl.BlockSpec((None, tm, tn), lambda b, j, i: (b, i, j)),
        out_shape=jax.ShapeDtypeStruct((e, m, f), BF16),
        scratch_shapes=[pltpu.VMEM((k, tn), BF16), pltpu.VMEM((k, tn), BF16)],
        compiler_params=pltpu.CompilerParams(
            dimension_semantics=("parallel", "parallel", "arbitrary"), vmem_limit_bytes=VMEM_LIMIT_BYTES),
        name="expert_hidden",
    )(xe, w_gate, w_up)


def _expert_down_kernel(h_ref, w_ref, g_ref, hi_ref, lo_ref, w_s):
    @pl.when(pl.program_id(2) == 0)
    def _():
        w_s[...] = w_ref[...].astype(BF16)

    y = jnp.dot(h_ref[...], w_s[...], preferred_element_type=F32) * g_ref[...]
    hi_ref[...], lo_ref[...] = _split(y)


def expert_down(hid, w_down, gate, layer):
    e, m, k = hid.shape
    n = w_down.shape[3]
    tm, tn = _pick(m, (1280, 1024, 512, 256)), 1024
    out = pl.BlockSpec((None, tm, tn), lambda b, j, i: (b, i, j))
    return pl.pallas_call(
        _expert_down_kernel,
        grid=(e, n // tn, m // tm),
        in_specs=[pl.BlockSpec((None, tm, k), lambda b, j, i: (b, i, 0)),
                  pl.BlockSpec((None, None, k, tn), lambda b, j, i: (layer, b, 0, j)),
                  pl.BlockSpec((None, tm, 1), lambda b, j, i: (b, i, 0))],
        out_specs=[out, out],
        out_shape=[jax.ShapeDtypeStruct((e, m, n), BF16), jax.ShapeDtypeStruct((e, m, n), BF16)],
        scratch_shapes=[pltpu.VMEM((k, tn), BF16)],
        compiler_params=pltpu.CompilerParams(
            dimension_semantics=("parallel", "parallel", "arbitrary"), vmem_limit_bytes=VMEM_LIMIT_BYTES),
        name="expert_down",
    )(hid, w_down, gate)


ROW_TILE = 256


def _ln(v):
    mu = jnp.mean(v, axis=-1, keepdims=True)
    d = v - mu
    return d * lax.rsqrt(jnp.mean(d * d, axis=-1, keepdims=True) + LN_EPS)


def _mix_out_kernel(m_ref, w_ref, x_ref, g1_ref, sh2_ref, sc2_ref, lng_ref, lnb_ref, wr_ref, br_ref,
                    x1_ref, h2_ref, lg_ref):
    y = jnp.dot(m_ref[...], w_ref[...], preferred_element_type=F32)
    x1 = _ln(DN_ALPHA * x_ref[...] + (1.0 + g1_ref[...]) * y) * lng_ref[...] + lnb_ref[...]
    x1_ref[...] = x1
    h2 = _ln(x1) * (1.0 + sc2_ref[...]) + sh2_ref[...]
    h2_ref[...] = h2.astype(h2_ref.dtype)
    lg_ref[...] = _mm3(_nn, h2, wr_ref[...]) + br_ref[...]


def mix_out(merged, row0, x, mod, L, w_out, ln_g, ln_b, w_router, b_router):
    n = x.shape[0]
    nm = mod.shape[0]
    mod4 = mod.reshape(nm, 6, 1, D_MODEL)
    per_seq = L // ROW_TILE
    mod_spec = lambda part: pl.BlockSpec(
        (None, None, 1, D_MODEL), lambda i, part=part: (0 if nm == 1 else i // per_seq, part, 0, 0))
    vec = pl.BlockSpec((1, D_MODEL), lambda i: (0, 0))
    wr = jnp.pad(w_router, ((0, 0), (0, LANES - N_EXPERTS)))
    br = jnp.pad(b_router, (0, LANES - N_EXPERTS)).reshape(1, LANES)
    return pl.pallas_call(
        _mix_out_kernel,
        grid=(n // ROW_TILE,),
        in_specs=[pl.BlockSpec((ROW_TILE, D_MODEL), lambda i: (row0 // ROW_TILE + i, 0)),
                  pl.BlockSpec((D_MODEL, D_MODEL), lambda i: (0, 0)),
                  pl.BlockSpec((ROW_TILE, D_MODEL), lambda i: (i, 0)),
                  mod_spec(2), mod_spec(3), mod_spec(4), vec, vec,
                  pl.BlockSpec((D_MODEL, LANES), lambda i: (0, 0)),
                  pl.BlockSpec((1, LANES), lambda i: (0, 0))],
        out_specs=[pl.BlockSpec((ROW_TILE, D_MODEL), lambda i: (i, 0)),
                   pl.BlockSpec((ROW_TILE, D_MODEL), lambda i: (i, 0)),
                   pl.BlockSpec((ROW_TILE, LANES), lambda i: (i, 0))],
        out_shape=[jax.ShapeDtypeStruct((n, D_MODEL), F32), jax.ShapeDtypeStruct((n, D_MODEL), BF16),
                   jax.ShapeDtypeStruct((n, LANES), F32)],
        compiler_params=pltpu.CompilerParams(dimension_semantics=("parallel",), vmem_limit_bytes=VMEM_LIMIT_BYTES),
        name="mix_out",
    )(merged, w_out, x, mod4, mod4, mod4, ln_g.reshape(1, -1), ln_b.reshape(1, -1), wr, br)


def _post_norm_kernel(x_ref, y_ref, gate_ref, lng_ref, lnb_ref, o_ref):
    o_ref[...] = _ln(DN_ALPHA * x_ref[...] + (1.0 + gate_ref[...]) * y_ref[...]) * lng_ref[...] + lnb_ref[...]


def post_norm(x, y, mod, part, L, ln_g, ln_b):
    n = x.shape[0]
    nm = mod.shape[0]
    per_seq = L // ROW_TILE
    rows = pl.BlockSpec((ROW_TILE, D_MODEL), lambda i: (i, 0))
    vec = pl.BlockSpec((1, D_MODEL), lambda i: (0, 0))
    return pl.pallas_call(
        _post_norm_kernel,
        grid=(n // ROW_TILE,),
        in_specs=[rows, rows,
                  pl.BlockSpec((None, None, 1, D_MODEL), lambda i: (0 if nm == 1 else i // per_seq, part, 0, 0)),
                  vec, vec],
        out_specs=rows,
        out_shape=jax.ShapeDtypeStruct((n, D_MODEL), F32),
        compiler_params=pltpu.CompilerParams(dimension_semantics=("parallel",), vmem_limit_bytes=VMEM_LIMIT_BYTES),
        name="post_norm",
    )(x, y, mod.reshape(nm, 6, 1, D_MODEL), ln_g.reshape(1, -1), ln_b.reshape(1, -1))


def _nn(a, b):
    return jnp.dot(a, b, preferred_element_type=F32)


def _nt(a, b):
    return lax.dot_general(a, b, (((1,), (1,)), ((), ())), preferred_element_type=F32)


def _tn(a, b):
    return lax.dot_general(a, b, (((0,), (0,)), ((), ())), preferred_element_type=F32)


def _split(a):
    hi = a.astype(BF16)
    return hi, (a - hi.astype(F32)).astype(BF16)


def _mm1(f, a, b):
    return f(a.astype(BF16), b.astype(BF16))


def _mm3(f, a, b):
    ah, al = _split(a)
    bh, bl = _split(b)
    return f(ah, bh) + (f(ah, bl) + f(al, bh))


def _chunk_masks():
    ri = lax.broadcasted_iota(jnp.int32, (CHUNK, CHUNK), 0)
    ci = lax.broadcasted_iota(jnp.int32, (CHUNK, CHUNK), 1)
    eye = ri == ci
    incl = (ri >= ci, ri <= ci)
    strict = (ri > ci, ri < ci)
    return eye, incl, strict


def _to_col(eye, row):
    return jnp.sum(jnp.where(eye, row, 0.0), axis=1, keepdims=True)


GC_BETA, GC_ALPHA, GC_IG, GC_FG, GC_END = 0, 2 * H_A, 4 * H_A, 4 * H_A + 2 * H_B, 4 * H_A + 4 * H_B


def _softplus(x):
    return jnp.maximum(x, 0.0) + jnp.log(1.0 + jnp.exp(-jnp.abs(x)))


def _gate_kernel(x_ref, p_ref, tab_ref, scal_ref, val_s, *, L):
    n = L // CHUNK
    x = x_ref[...]
    p = p_ref[...]
    lane = lax.broadcasted_iota(jnp.int32, (L, LANES), 1)
    beta = jax.nn.sigmoid(x)
    loga = p[0:1] * _softplus(x + p[1:2])
    ig = x + p[2:3]
    lf = -_softplus(-(x + p[3:4]))
    val_s[...] = jnp.where(lane < GC_ALPHA, beta, jnp.where(lane < GC_IG, loga, jnp.where(lane < GC_FG, ig, lf)))

    ri = lax.broadcasted_iota(jnp.int32, (CHUNK, CHUNK), 0)
    ci = lax.broadcasted_iota(jnp.int32, (CHUNK, CHUNK), 1)
    lower = jnp.where(ri >= ci, 1.0, 0.0).astype(BF16)
    upper = jnp.where(ri <= ci, 1.0, 0.0).astype(BF16)
    ident = jnp.where(ri == ci, 1.0, 0.0).astype(BF16)
    cl = lax.broadcasted_iota(jnp.int32, (CHUNK, LANES), 1)
    cumulative = ((cl >= GC_ALPHA) & (cl < GC_IG)) | ((cl >= GC_FG) & (cl < GC_END))
    backward = ((cl >= GC_ALPHA + H_A) & (cl < GC_IG)) | (cl >= GC_FG + H_B)

    def split3(v):
        h1 = v.astype(BF16)
        r1 = v - h1.astype(F32)
        h2 = r1.astype(BF16)
        return h1, h2, (r1 - h2.astype(F32)).astype(BF16)

    def chunk(c, carry):
        v = val_s[pl.ds(pl.multiple_of(c * CHUNK, CHUNK), CHUNK), :]
        parts = split3(v)
        pre = sum(_nn(lower, h) for h in parts)
        suf = sum(_nn(upper, h) for h in parts)
        out = jnp.where(cumulative, jnp.where(backward, suf, pre), v)
        tot = jnp.where(backward[0:1], suf[0:1], pre[CHUNK - 1:CHUNK])
        wlog = tot - out + pltpu.roll(out, GC_FG - GC_IG, 1)
        wmax = jnp.max(wlog, axis=0, keepdims=True)
        tab_ref[c] = sum(_tn(h, ident) for h in split3(out))
        scal_ref[c] = jnp.concatenate([tot, wmax, jnp.zeros((6, LANES), F32)], axis=0)
        return carry

    lax.fori_loop(0, n, chunk, 0)


def gate_tables(proj_small, row_blk0, nb, L, lp):
    n = L // CHUNK
    rows = [jnp.pad(v.reshape(-1), (off, LANES - off - v.size)) for v, off in (
        (-jnp.exp(lp['gdn_a_log']), GC_ALPHA), (lp['gdn_dt_bias'], GC_ALPHA),
        (lp['ml_i_bias'], GC_IG), (lp['ml_f_bias'], GC_FG))]
    params = jnp.stack(rows + [jnp.zeros((LANES,), F32)] * 4)
    return pl.pallas_call(
        functools.partial(_gate_kernel, L=L),
        grid=(nb,),
        in_specs=[pl.BlockSpec((L, LANES), lambda b: (row_blk0 + b, 0)),
                  pl.BlockSpec((8, LANES), lambda b: (0, 0))],
        out_specs=[pl.BlockSpec((None, n, LANES, CHUNK), lambda b: (b, 0, 0, 0)),
                   pl.BlockSpec((None, n, 8, LANES), lambda b: (b, 0, 0, 0))],
        out_shape=[jax.ShapeDtypeStruct((nb, n, LANES, CHUNK), F32),
                   jax.ShapeDtypeStruct((nb, n, 8, LANES), F32)],
        scratch_shapes=[pltpu.VMEM((L, LANES), F32)],
        compiler_params=pltpu.CompilerParams(dimension_semantics=("parallel",), vmem_limit_bytes=VMEM_LIMIT_BYTES),
        name="gate_tables",
    )(proj_small, params)


def _gate_row(tab_ref, c, lane):
    return tab_ref[c, pl.ds(lane, 1), :]


def _gate_scalar(scal_ref, c, row, lane):
    v = scal_ref[c][row:row + 1, :]
    li = lax.broadcasted_iota(jnp.int32, (1, LANES), 1)
    return jnp.sum(jnp.where(li == lane, v, 0.0), axis=1, keepdims=True)


def _gdn_kernel(q_ref, k_ref, v_ref, z_ref, cw_ref, tab_ref, scal_ref, s0_ref, nw_ref, y_ref, sf_ref,
                qs, ks, vs, wq_s, u_s, kd_s, p_s, o_s, st_s, *, L):
    n = L // CHUNK
    head = pl.program_id(1)
    beta_lane = [GC_BETA + d * H_A + head for d in range(2)]
    g_lane = [GC_ALPHA + d * H_A + head for d in range(2)]
    row = lax.broadcasted_iota(jnp.int32, (L, LANES), 0)

    def conv_silu(x_ref, part):
        x = x_ref[...]
        w = cw_ref[part]
        x_prev = jnp.where(row == 0, 0.0, pltpu.roll(x, 1, 0))
        x_next = jnp.where(row == L - 1, 0.0, pltpu.roll(x, L - 1, 0))
        y = x_prev * w[0:1] + x * w[1:2] + x_next * w[2:3]
        return y * jax.nn.sigmoid(y)

    def l2n(x):
        return x * lax.rsqrt(jnp.sum(x * x, axis=-1, keepdims=True) + LN_EPS)

    qs[...] = l2n(conv_silu(q_ref, 0)) * (DK_A ** -0.5)
    ks[...] = l2n(conv_silu(k_ref, 1))
    vs[...] = conv_silu(v_ref, 2)

    eye, incl, strict = _chunk_masks()
    eye_f = jnp.where(eye, 1.0, 0.0)

    def intra(grp, carry):
        chains = []
        for j in range(INTRA_GROUP):
            c = grp * INTRA_GROUP + j
            r0 = pl.multiple_of(c * CHUNK, CHUNK)
            qc = qs[pl.ds(r0, CHUNK), :]
            kc = ks[pl.ds(r0, CHUNK), :]
            vc = vs[pl.ds(r0, CHUNK), :]
            kk = _mm3(_nt, kc, kc)
            qk = _mm1(_nt, qc, kc)
            for d in range(2):
                g_row = _gate_row(tab_ref, c, g_lane[d])
                b_row = _gate_row(tab_ref, c, beta_lane[d])
                gl_row = _gate_scalar(scal_ref, c, 0, g_lane[d])
                g_col = _to_col(eye, g_row)
                b_col = _to_col(eye, b_row)
                dec = jnp.exp(jnp.where(incl[d], g_col - g_row, -jnp.inf))
                lmat = b_col * kk * jnp.where(strict[d], dec, 0.0)
                kd_s[d, pl.ds(r0, CHUNK), :] = jnp.exp(gl_row - g_col) * kc
                wq_s[d, c, CHUNK:, :] = jnp.exp(g_col) * qc
                p_s[d, c] = qk * dec
                chains.append(dict(d=d, c=c, r0=r0, pw=lmat, tinv=eye_f - lmat,
                                   rhs_w=(b_col * jnp.exp(g_col)) * kc, rhs_u=b_col * vc))
        for _ in range(int(math.log2(CHUNK)) - 1):
            for ch in chains:
                ch['pw'] = _mm3(_nn, ch['pw'], ch['pw'])
            for ch in chains:
                ch['tinv'] = ch['tinv'] + _mm3(_nn, ch['tinv'], ch['pw'])
        for ch in chains:
            d, r0 = ch['d'], ch['r0']
            wq_s[d, ch['c'], :CHUNK, :] = _mm3(_nn, ch['tinv'], ch['rhs_w'])
            u_s[d, pl.ds(r0, CHUNK), :] = _mm3(_nn, ch['tinv'], ch['rhs_u'])
        return carry

    lax.fori_loop(0, n // INTRA_GROUP, intra, 0)

    st_s[...] = s0_ref[...]

    def scan(i, carry):
        cs = (i, n - 1 - i)
        r0s = [pl.multiple_of(c * CHUNK, CHUNK) for c in cs]
        s = [st_s[d] for d in range(2)]
        ws = [_mm1(_nn, wq_s[d, cs[d]], s[d]) for d in range(2)]
        uc = [u_s[d, pl.ds(r0s[d], CHUNK), :] - ws[d][:CHUNK] for d in range(2)]
        pu = [_mm1(_nn, p_s[d, cs[d]], uc[d]) for d in range(2)]
        ku = [_mm1(_tn, kd_s[d, pl.ds(r0s[d], CHUNK), :], uc[d]) for d in range(2)]
        for d in range(2):
            o_s[d, pl.ds(r0s[d], CHUNK), :] = ws[d][CHUNK:] + pu[d]
            st_s[d] = jnp.exp(_gate_scalar(scal_ref, cs[d], 0, g_lane[d])) * s[d] + ku[d]
        return carry

    lax.fori_loop(0, n, scan, 0)

    o = o_s[0] + o_s[1]
    z = z_ref[...]
    o = o * lax.rsqrt(jnp.mean(o * o, axis=-1, keepdims=True) + LN_EPS) * nw_ref[...]
    y_ref[...] = (o * (z * jax.nn.sigmoid(z))).astype(y_ref.dtype)
    sf_ref[...] = st_s[...]


def _gate_specs(n):
    return [pl.BlockSpec((None, n, LANES, CHUNK), lambda b, h: (b, 0, 0, 0)),
            pl.BlockSpec((None, n, 8, LANES), lambda b, h: (b, 0, 0, 0))]


def gdn_branch(proj, row_blk0, nb, L, gates, conv_w, s0, norm_w):
    n = L // CHUNK
    seq = lambda col: pl.BlockSpec((L, LANES), lambda b, h, col=col: (row_blk0 + b, col // LANES + h))
    scr = lambda *s: pltpu.VMEM(s, F32)
    return pl.pallas_call(
        functools.partial(_gdn_kernel, L=L),
        grid=(nb, H_A),
        in_specs=[seq(COL_QA), seq(COL_KA), seq(COL_VA), seq(COL_ZA),
                  pl.BlockSpec((3, None, 3, LANES), lambda b, h: (0, h, 0, 0)),
                  *_gate_specs(n),
                  pl.BlockSpec((None, 2, None, DK_A, DV_A), lambda b, h: (b, 0, h, 0, 0)),
                  pl.BlockSpec((1, LANES), lambda b, h: (0, 0))],
        out_specs=[pl.BlockSpec((L, LANES), lambda b, h: (b, h)),
                   pl.BlockSpec((None, 2, None, DK_A, DV_A), lambda b, h: (b, 0, h, 0, 0))],
        out_shape=[jax.ShapeDtypeStruct((nb * L, V_A), BF16),
                   jax.ShapeDtypeStruct((nb, 2, H_A, DK_A, DV_A), F32)],
        scratch_shapes=[scr(L, LANES), scr(L, LANES), scr(L, LANES),
                        scr(2, n, 2 * CHUNK, LANES), scr(2, L, LANES), scr(2, L, LANES),
                        scr(2, n, CHUNK, CHUNK), scr(2, L, LANES), scr(2, DK_A, DV_A)],
        compiler_params=pltpu.CompilerParams(
            dimension_semantics=("parallel", "parallel"), vmem_limit_bytes=VMEM_LIMIT_BYTES),
        name="gdn_branch",
    )(proj, proj, proj, proj, conv_w, *gates, s0, norm_w)


def _mlstm_kernel(q_ref, k_ref, v_ref, ob_ref, tab_ref, scal_ref, c0_ref, n0_ref, m0_ref, nw_ref,
                  y_ref, cf_ref, nf_ref, mf_ref, h_s, c_s, n_s, m_s, *, L):
    n = L // CHUNK
    head = pl.program_id(1)
    ig_lane = [GC_IG + d * H_B + head for d in range(2)]
    fg_lane = [GC_FG + d * H_B + head for d in range(2)]
    eye, incl, _ = _chunk_masks()
    c_s[...] = c0_ref[...]
    n_s[...] = n0_ref[...]
    m_s[...] = m0_ref[...]

    def body(grp, carry):
        ch = []
        for j in range(ML_GROUP):
            i = grp * ML_GROUP + j
            for d in range(2):
                c = i if d == 0 else n - 1 - i
                r0 = pl.multiple_of(c * CHUNK, CHUNK)
                ch.append(dict(d=d, r0=r0, qc=q_ref[pl.ds(r0, CHUNK), :],
                               kc=k_ref[pl.ds(r0, CHUNK), :] * (DQK_B ** -0.5), vc=v_ref[pl.ds(r0, CHUNK), :],
                               b_row=_gate_row(tab_ref, c, fg_lane[d]), i_row=_gate_row(tab_ref, c, ig_lane[d]),
                               bl_row=_gate_scalar(scal_ref, c, 0, fg_lane[d]),
                               wm_row=_gate_scalar(scal_ref, c, 1, fg_lane[d])))
        for x in ch:
            x['qk'] = _mm1(_nt, x['qc'], x['kc'])
        for x in ch:
            x['b_col'] = _to_col(eye, x['b_row'])
            i_col = _to_col(eye, x['i_row'])
            dlog = jnp.where(incl[x['d']], x['b_col'] - x['b_row'] + x['i_row'], -jnp.inf)
            x['dmax'] = jnp.max(dlog, axis=1, keepdims=True)
            x['pw'] = jnp.exp(dlog - x['dmax']) * x['qk']
            x['ewk'] = jnp.exp(x['bl_row'][:, :1] - x['b_col'] + i_col - x['wm_row'][:, :1]) * x['kc']
        for x in ch:
            x['intra_num'] = _mm1(_nn, x['pw'], x['vc'])
            x['dc'] = _mm1(_tn, x['ewk'], x['vc'])
        for x in ch:
            x['intra_den'] = jnp.sum(x['pw'], axis=1, keepdims=True)
            x['dn'] = jnp.sum(x['ewk'], axis=0, keepdims=True)
        for x in ch:
            d, r0 = x['d'], x['r0']
            cm, nm, mm = c_s[d], n_s[d], m_s[d]
            alog = x['b_col'] + mm
            mt = jnp.maximum(alog, x['dmax'])
            wi = jnp.exp(alog - mt)[:, :1]
            wa = jnp.exp(x['dmax'] - mt)[:, :1]
            num = wi * _mm1(_nn, x['qc'], cm) + wa * x['intra_num']
            den = wi * jnp.sum(x['qc'] * nm, axis=1, keepdims=True) + wa * x['intra_den']
            h_s[d, pl.ds(r0, CHUNK), :] = num / jnp.maximum(jnp.abs(den), jnp.exp(-mt[:, :1]))

            m_new = jnp.maximum(x['bl_row'] + mm, x['wm_row'])
            a = jnp.exp(x['bl_row'] + mm - m_new)
            e = jnp.exp(x['wm_row'] - m_new)
            c_s[d] = a[:, :1] * cm + e[:, :1] * x['dc']
            n_s[d] = a * nm + e * x['dn']
            m_s[d] = m_new
        return carry

    lax.fori_loop(0, n // ML_GROUP, body, 0)

    h = h_s[0] + h_s[1]
    h = h * lax.rsqrt(jnp.mean(h * h, axis=-1, keepdims=True) + LN_EPS) * nw_ref[...]
    y_ref[...] = (h * jax.nn.sigmoid(ob_ref[...])).astype(y_ref.dtype)
    cf_ref[...] = c_s[...]
    nf_ref[...] = n_s[...]
    mf_ref[...] = m_s[...]


def mlstm_branch(proj, row_blk0, nb, L, gates, c0, n0, m0, norm_w):
    n = L // CHUNK
    seq = lambda col, w: pl.BlockSpec((L, w), lambda b, h, col=col, w=w: (row_blk0 + b, col // w + h))
    st = lambda *s: pl.BlockSpec((None, 2, None) + s, lambda b, h: (b, 0, h, 0, 0))
    scr = lambda *s: pltpu.VMEM(s, F32)
    return pl.pallas_call(
        functools.partial(_mlstm_kernel, L=L),
        grid=(nb, H_B),
        in_specs=[seq(COL_QB, DQK_B), seq(COL_KB, DQK_B), seq(COL_VB, DV_B), seq(COL_OB, DV_B),
                  *_gate_specs(n),
                  st(DQK_B, DV_B), st(1, DQK_B), st(1, LANES),
                  pl.BlockSpec((1, DV_B), lambda b, h: (0, 0))],
        out_specs=[pl.BlockSpec((L, DV_B), lambda b, h: (b, h)),
                   st(DQK_B, DV_B), st(1, DQK_B), st(1, LANES)],
        out_shape=[jax.ShapeDtypeStruct((nb * L, V_B), BF16),
                   jax.ShapeDtypeStruct((nb, 2, H_B, DQK_B, DV_B), F32),
                   jax.ShapeDtypeStruct((nb, 2, H_B, 1, DQK_B), F32),
                   jax.ShapeDtypeStruct((nb, 2, H_B, 1, LANES), F32)],
        scratch_shapes=[scr(2, L, DV_B), scr(2, DQK_B, DV_B), scr(2, 1, DQK_B), scr(2, 1, LANES)],
        compiler_params=pltpu.CompilerParams(
            dimension_semantics=("parallel", "parallel"), vmem_limit_bytes=VMEM_LIMIT_BYTES),
        name="mlstm_branch",
    )(proj, proj, proj, proj, *gates, c0, n0, m0, norm_w)


def _ln_stats(x):
    mu = jnp.mean(x, -1, keepdims=True)
    var = jnp.mean(jnp.square(x - mu), -1, keepdims=True)
    return (x - mu) * lax.rsqrt(var + LN_EPS)


def centred_conv(x, w):
    k = w.shape[0]
    p = k // 2
    L = x.shape[1]
    xp = jnp.pad(x, ((0, 0), (p, p), (0, 0)))
    y = xp[:, 0:L] * w[0]
    for j in range(1, k):
        y = y + xp[:, j:j + L] * w[j]
    return y


def hyena_filters(L, w1, b1, w2, b2, w3, b3, freq, rate):
    pos = jnp.arange(L, dtype=F32)
    t = pos / L
    ang = (2.0 * math.pi) * t[:, None] * jnp.arange(1, HY_BANDS + 1, dtype=F32)
    feats = jnp.concatenate([t[:, None], jnp.sin(ang), jnp.cos(ang)], -1)
    hp = lax.Precision.HIGHEST
    z = jnp.sin(freq[0] * (jnp.dot(feats, w1, precision=hp) + b1))
    z = jnp.sin(freq[1] * (jnp.dot(z, w2, precision=hp) + b2))
    filt = jnp.dot(z, w3, precision=hp) + b3
    lag = jnp.abs(pos - L // 2) / L
    filt = filt * jnp.exp(-lag[:, None] * rate)
    return filt.reshape(L, HY_ORDER, HY_CH)


def dft_size(L):
    n = 3 * L // 2
    return n if (n // 2) % LANES == 0 else 2 * L


def dft_matrices(L):
    N = dft_size(L)
    F = N // 2
    k = jnp.arange(F, dtype=jnp.int32)
    t = jnp.arange(L, dtype=jnp.int32)
    w = 2.0 * math.pi / N

    def cos_sin(rows, cols):
        hi = jnp.arange(cols.shape[0] // LANES, dtype=jnp.int32) * LANES
        lo = jnp.arange(LANES, dtype=jnp.int32)
        a = w * ((rows[:, None] * hi[None, :]) % N).astype(F32)[:, :, None]
        b = w * ((rows[:, None] * lo[None, :]) % N).astype(F32)[:, None, :]
        ca, sa, cb, sb = jnp.cos(a), jnp.sin(a), jnp.cos(b), jnp.sin(b)
        shape = (rows.shape[0], cols.shape[0])
        return (ca * cb - sa * sb).reshape(shape), (sa * cb + ca * sb).reshape(shape)

    alt = (1 - 2 * (t % 2)).astype(F32)
    cf, sin_f = cos_sin(k, t)
    sf = jnp.where(k[:, None] == 0, alt[None, :], -sin_f)
    tt = t + L // 2
    alt_i = (1 - 2 * (tt % 2)).astype(F32)
    cos_i, sin_i = cos_sin(tt, k)
    ci = jnp.where(k[None, :] == 0, 1.0 / N, (2.0 / N) * cos_i)
    si = jnp.where(k[None, :] == 0, alt_i[:, None] / N, (-2.0 / N) * sin_i)
    return cf, sf, ci, si


def _mm3_kernel(x_ref, w_ref, o_ref):
    o_ref[...] = _mm3(_nn, x_ref[...], w_ref[...])


def pmm3(x, w):
    m, k = x.shape
    _, n = w.shape
    tm = _pick(m, (512, 256, 128))
    tn = _pick(n, (512, 256, 128))
    return pl.pallas_call(
        _mm3_kernel,
        grid=(m // tm, n // tn),
        in_specs=[pl.BlockSpec((tm, k), lambda i, j: (i, 0)),
                  pl.BlockSpec((k, tn), lambda i, j: (0, j))],
        out_specs=pl.BlockSpec((tm, tn), lambda i, j: (i, j)),
        out_shape=jax.ShapeDtypeStruct((m, n), F32),
        compiler_params=pltpu.CompilerParams(
            dimension_semantics=("parallel", "parallel"), vmem_limit_bytes=VMEM_LIMIT_BYTES),
        name="pmm3",
    )(x, w)


def _hyena_kernel(z_ref, un_ref, cwz_ref, cwu_ref, skip_ref, hr_ref, hi_ref, cf_ref, sf_ref, ci_ref, si_ref,
                  o_ref, z_s, zb_s, acc_s, *, L, ft, conv_z):
    f = pl.program_id(2)
    row = lax.broadcasted_iota(jnp.int32, (L, z_ref.shape[1]), 0)

    def conv3(x, w):
        x_prev = jnp.where(row == 0, 0.0, pltpu.roll(x, 1, 0))
        x_next = jnp.where(row == L - 1, 0.0, pltpu.roll(x, L - 1, 0))
        return x_prev * w[0:1] + x * w[1:2] + x_next * w[2:3]

    @pl.when(f == 0)
    def _():
        z = z_ref[...]
        if conv_z:
            z = conv3(z, cwz_ref[...])
        z_s[...] = z
        zb_s[...] = z.astype(BF16)

    zb = zb_s[...]
    zr = _nn(cf_ref[...], zb)
    zi = _nn(sf_ref[...], zb)
    hr = hr_ref[...]
    hi = hi_ref[...]
    packed = (f * ft + lax.broadcasted_iota(jnp.int32, zr.shape, 0)) == 0
    zihi = zi * hi
    yr = zr * hr - jnp.where(packed, 0.0, zihi)
    yi = jnp.where(packed, zihi, zr * hi + zi * hr)
    part = _nn(ci_ref[...], yr.astype(BF16)) + _nn(si_ref[...], yi.astype(BF16))

    @pl.when(f == 0)
    def _():
        acc_s[...] = part

    @pl.when(f > 0)
    def _():
        acc_s[...] += part

    @pl.when(f == pl.num_programs(2) - 1)
    def _():
        un = conv3(un_ref[...], cwu_ref[...])
        o_ref[...] = (un * (acc_s[...] + skip_ref[...] * z_s[...])).astype(o_ref.dtype)


def hyena_order(z_arr, z_blk0, z_col, conv_z, proj, row_blk0, un_col, nb, L, conv_w, cwz_col, skip,
                hr, hi, h_col, mats, out_dtype=F32):
    cf, sf, ci, si = mats
    F = cf.shape[0]
    cb, ft = (HY_CH, _pick(F, (512, 256))) if L * HY_CH * 4 <= (1 << 20) else (HY_CH // 2, 256)
    seq = lambda blk0, col: pl.BlockSpec((L, cb), lambda b, j, f: (blk0 + b, col // cb + j))
    return pl.pallas_call(
        functools.partial(_hyena_kernel, L=L, ft=ft, conv_z=conv_z),
        grid=(nb, HY_CH // cb, F // ft),
        in_specs=[seq(z_blk0, z_col), seq(row_blk0, un_col),
                  pl.BlockSpec((3, cb), lambda b, j, f: (0, cwz_col // cb + j)),
                  pl.BlockSpec((3, cb), lambda b, j, f: (0, (un_col - COL_HY) // cb + j)),
                  pl.BlockSpec((1, cb), lambda b, j, f: (0, j)),
                  pl.BlockSpec((ft, cb), lambda b, j, f: (f, h_col // cb + j)),
                  pl.BlockSpec((ft, cb), lambda b, j, f: (f, h_col // cb + j)),
                  pl.BlockSpec((ft, L), lambda b, j, f: (f, 0)),
                  pl.BlockSpec((ft, L), lambda b, j, f: (f, 0)),
                  pl.BlockSpec((L, ft), lambda b, j, f: (0, f)),
                  pl.BlockSpec((L, ft), lambda b, j, f: (0, f))],
        out_specs=pl.BlockSpec((L, cb), lambda b, j, f: (b, j)),
        out_shape=jax.ShapeDtypeStruct((nb * L, HY_CH), out_dtype),
        scratch_shapes=[pltpu.VMEM((L, cb), F32), pltpu.VMEM((L, cb), BF16), pltpu.VMEM((L, cb), F32)],
        compiler_params=pltpu.CompilerParams(
            dimension_semantics=("parallel", "parallel", "arbitrary"), vmem_limit_bytes=VMEM_LIMIT_LARGE_BYTES),
        name="hyena_order",
    )(z_arr, proj, conv_w, conv_w, skip, hr, hi, cf, sf, ci, si)


def grid_pos_embed(n_tokens):
    rows = n_tokens // GRID_W
    quarter = D_MODEL // 4
    omega = 1.0 / (10000.0 ** (jnp.arange(quarter, dtype=F32) / quarter))
    r = jnp.arange(rows, dtype=F32)[:, None] * omega
    cl = jnp.arange(GRID_W, dtype=F32)[:, None] * omega
    full = lambda v, axis: jnp.broadcast_to(jnp.expand_dims(v, axis), (rows, GRID_W, quarter))
    pe = jnp.concatenate([full(jnp.sin(r), 1), full(jnp.cos(r), 1), full(jnp.sin(cl), 0), full(jnp.cos(cl), 0)], -1)
    return pe.reshape(rows * GRID_W, D_MODEL)


def seq_mixers(proj_big, proj_small, row0, nb, L, lp, mats, s_gdn, s_c, s_n, s_m):
    blk0 = row0 // L
    gates = gate_tables(proj_small, blk0, nb, L, lp)

    conv_w = lp['gdn_conv'].reshape(3, 3, H_A, DK_A).transpose(1, 2, 0, 3)
    y_a, s_gdn_new = gdn_branch(proj_big, blk0, nb, L, gates, conv_w, s_gdn, lp['gdn_norm'].reshape(1, DV_A))

    m0 = jnp.broadcast_to(s_m[..., None, None], s_m.shape + (1, LANES))
    y_b, c_new, n_new, m_new = mlstm_branch(proj_big, blk0, nb, L, gates, s_c, s_n[..., None, :], m0,
                                            lp['ml_norm'].reshape(1, DV_B))

    filt = hyena_filters(L, lp['hy_w1'], lp['hy_b1'], lp['hy_w2'], lp['hy_b2'], lp['hy_w3'], lp['hy_b3'],
                         lp['hy_freq'], lp['hy_rate']).reshape(L, HY_ORDER * HY_CH)
    mats32, mats16 = mats[L]
    hr, hi = pmm3(mats32[0], filt), pmm3(mats32[1], filt)
    z_arr, z_blk0, z_col = proj_big, blk0, COL_HY
    for order in range(HY_ORDER):
        z_arr = hyena_order(z_arr, z_blk0, z_col, order == 0, proj_big, blk0, COL_HY + (order + 1) * HY_CH, nb, L,
                            lp['hy_conv'], 0, lp['hy_skip'][order:order + 1], hr, hi, order * HY_CH, mats16,
                            out_dtype=BF16 if order == HY_ORDER - 1 else F32)
        z_blk0, z_col = 0, 0
    return (y_a, y_b, z_arr), (s_gdn_new, c_new, n_new[..., 0, :], m_new[..., 0, 0])


TOK_BLK = LANES
COMBINE_ROWS = 48
ROW_ALIGN = 16


def _route_kernel(a_ref, rm_ref, idx_ref, th_s, *, nblk, cap):
    bits = pltpu.bitcast(a_ref[...], jnp.int32)

    def search(i, th):
        cand = th | jnp.left_shift(jnp.int32(1), 30 - i)
        cnt = jnp.sum(jnp.sum(jnp.where(bits >= cand, 1.0, 0.0), axis=2, keepdims=True), axis=1, keepdims=True)
        return jnp.where(cnt >= cap, cand, th)

    th_s[...] = lax.fori_loop(0, 31, search, jnp.zeros((N_EXPERTS, 1, LANES), jnp.int32))

    li = lax.broadcasted_iota(jnp.int32, (LANES, LANES), 0)
    lj = lax.broadcasted_iota(jnp.int32, (LANES, LANES), 1)
    before_lane = jnp.where(li < lj, 1.0, 0.0).astype(BF16)
    bi = lax.broadcasted_iota(jnp.int32, (nblk, nblk), 0)
    bj = lax.broadcasted_iota(jnp.int32, (nblk, nblk), 1)
    before_blk = jnp.where(bj < bi, 1.0, 0.0).astype(BF16)
    diag = lax.broadcasted_iota(jnp.int32, (nblk, LANES), 0) == lax.broadcasted_iota(jnp.int32, (nblk, LANES), 1)
    slot = lax.broadcasted_iota(jnp.int32, (cap, LANES), 0).astype(F32)
    lane = lax.broadcasted_iota(jnp.int32, (cap, LANES), 1)
    lane_f = lane.astype(F32)

    def ranks(mask):
        m = mask.astype(BF16)
        inside = _nn(m, before_lane)
        cnt = jnp.sum(mask, axis=1, keepdims=True)
        earlier = _nn(before_blk, jnp.broadcast_to(cnt, (nblk, LANES)).astype(BF16))
        return inside, earlier, cnt

    def per_expert(e, carry):
        b = pltpu.bitcast(a_ref[e], jnp.int32)
        th = th_s[e]
        gt = jnp.where(b > th, 1.0, 0.0)
        eq = jnp.where(b == th, 1.0, 0.0)
        need = cap - jnp.sum(jnp.sum(gt, axis=1, keepdims=True), axis=0, keepdims=True)
        eq_in, eq_before, _ = ranks(eq)
        sel = jnp.maximum(gt, jnp.where(eq_in + eq_before < need, eq, 0.0))
        inside, earlier, cnt = ranks(sel)
        rm = jnp.where(sel > 0.0, inside, -1.0)
        rm_ref[e] = rm
        end_row = jnp.sum(jnp.where(diag, earlier + cnt, 0.0), axis=0, keepdims=True)
        cnt_row = jnp.sum(jnp.where(diag, jnp.broadcast_to(cnt, (nblk, LANES)), 0.0), axis=0, keepdims=True)
        done = jnp.where((lane < nblk) & (end_row <= slot), 1.0, 0.0)
        blk = jnp.sum(done, axis=1, keepdims=True)
        local = slot[:, :1] - jnp.sum(done * cnt_row, axis=1, keepdims=True)
        onehot = jnp.where(lane_f == blk, 1.0, 0.0).astype(BF16)
        rows = _nn(onehot[:, :nblk], rm.astype(BF16))
        tok_lane = jnp.sum(jnp.where(rows == local, lane_f, 0.0), axis=1, keepdims=True)
        idx_ref[e] = (blk * TOK_BLK + tok_lane).astype(jnp.int32)
        return carry

    lax.fori_loop(0, N_EXPERTS, per_expert, 0)


def route(aff_t):
    e, n = aff_t.shape
    nblk, cap = n // TOK_BLK, CAP_FACTOR * n // N_EXPERTS
    full = lambda *s: pl.BlockSpec(s, lambda i: (0,) * len(s))
    rm, idx = pl.pallas_call(
        functools.partial(_route_kernel, nblk=nblk, cap=cap),
        grid=(1,),
        in_specs=[full(e, nblk, TOK_BLK)],
        out_specs=[full(e, nblk, TOK_BLK), full(e, cap, 1)],
        out_shape=[jax.ShapeDtypeStruct((e, nblk, TOK_BLK), F32), jax.ShapeDtypeStruct((e, cap, 1), jnp.int32)],
        scratch_shapes=[pltpu.VMEM((e, 1, LANES), jnp.int32)],
        compiler_params=pltpu.CompilerParams(dimension_semantics=("arbitrary",), vmem_limit_bytes=VMEM_LIMIT_BYTES),
        name="route",
    )(aff_t.reshape(e, nblk, TOK_BLK))
    return rm, idx[..., 0]


def _combine_kernel(base_ref, cnt_ref, rm_ref, x_ref, gate_ref, lng_ref, lnb_ref, hi_ref, lo_ref, o_ref,
                    buf_hi, buf_lo, sem, xbuf_hi, xbuf_lo, xsem, acc, *, m_rows, nblk):
    b = pl.program_id(0)
    slot = b % 2
    w_rows = COMBINE_ROWS
    r_iota = lax.broadcasted_iota(jnp.int32, (w_rows, LANES), 0).astype(F32)

    def first_row(blk, e):
        return (base_ref[blk * N_EXPERTS + e] // ROW_ALIGN) * ROW_ALIGN

    def window(blk, e, w):
        return pl.multiple_of(jnp.minimum(first_row(blk, e) + w * w_rows, m_rows - w_rows), ROW_ALIGN)

    def copies(blk, e, s):
        start = window(blk, e, 0)
        return (pltpu.make_async_copy(hi_ref.at[e, pl.ds(start, w_rows), :], buf_hi.at[s, e], sem.at[s, 0, e]),
                pltpu.make_async_copy(lo_ref.at[e, pl.ds(start, w_rows), :], buf_lo.at[s, e], sem.at[s, 1, e]))

    def issue(blk, s):
        for e in range(N_EXPERTS):
            for cp in copies(blk, e, s):
                cp.start()

    @pl.when(b == 0)
    def _():
        issue(0, 0)

    @pl.when(b + 1 < nblk)
    def _():
        issue(b + 1, 1 - slot)

    def placement(e, w, start):
        rank = rm_ref[e:e + 1, :]
        rel = rank + (base_ref[b * N_EXPERTS + e] - first_row(b, e)).astype(F32)
        lo_row = w * jnp.float32(w_rows)
        mine = (rank >= 0.0) & (rel >= lo_row) & (rel < lo_row + w_rows)
        target = jnp.where(mine, rel + (first_row(b, e) - start).astype(F32), -1.0)
        return jnp.where(r_iota == target, 1.0, 0.0).astype(BF16)

    for e in range(N_EXPERTS):
        for cp in copies(b, e, slot):
            cp.wait()
    put = jnp.concatenate([placement(e, 0, window(b, e, 0)) for e in range(N_EXPERTS)], axis=0)
    rows_hi = buf_hi[slot].reshape(N_EXPERTS * w_rows, -1)
    rows_lo = buf_lo[slot].reshape(N_EXPERTS * w_rows, -1)
    acc[...] = _tn(put, rows_hi) + _tn(put, rows_lo)

    for e in range(N_EXPERTS):
        used = base_ref[b * N_EXPERTS + e] - first_row(b, e) + cnt_ref[b * N_EXPERTS + e]

        def extra(w, carry, e=e):
            s = window(b, e, w)
            c_hi = pltpu.make_async_copy(hi_ref.at[e, pl.ds(s, w_rows), :], xbuf_hi, xsem.at[0])
            c_lo = pltpu.make_async_copy(lo_ref.at[e, pl.ds(s, w_rows), :], xbuf_lo, xsem.at[1])
            c_hi.start()
            c_lo.start()
            c_hi.wait()
            c_lo.wait()
            p = placement(e, w, s)
            acc[...] += _tn(p, xbuf_hi[...]) + _tn(p, xbuf_lo[...])
            return carry

        lax.fori_loop(1, (used + w_rows - 1) // w_rows, extra, 0)

    o_ref[...] = _ln(DN_ALPHA * x_ref[...] + (1.0 + gate_ref[...]) * acc[...]) * lng_ref[...] + lnb_ref[...]


def combine_post_norm(ye_hi, ye_lo, m0, rm, x1, mod, L, ln_g, ln_b):
    e, m_rows, d = ye_hi.shape
    n = x1.shape[0]
    nblk = n // TOK_BLK
    nm = mod.shape[0]
    per_seq = L // TOK_BLK
    rm_t = rm.transpose(1, 0, 2)
    cnt = jnp.sum(rm_t >= 0.0, axis=-1).astype(jnp.int32)
    base = m0 + jnp.cumsum(cnt, axis=0) - cnt
    rows = pl.BlockSpec((TOK_BLK, d), lambda i, *_: (i, 0))
    vec = pl.BlockSpec((1, d), lambda i, *_: (0, 0))
    return pl.pallas_call(
        functools.partial(_combine_kernel, m_rows=m_rows, nblk=nblk),
        grid_spec=pltpu.PrefetchScalarGridSpec(
            num_scalar_prefetch=2,
            grid=(nblk,),
            in_specs=[pl.BlockSpec((None, e, TOK_BLK), lambda i, *_: (i, 0, 0)),
                      rows,
                      pl.BlockSpec((None, None, 1, d), lambda i, *_: (0 if nm == 1 else i // per_seq, 5, 0, 0)),
                      vec, vec,
                      pl.BlockSpec(memory_space=pl.ANY), pl.BlockSpec(memory_space=pl.ANY)],
            out_specs=rows,
            scratch_shapes=[pltpu.VMEM((2, e, COMBINE_ROWS, d), BF16), pltpu.VMEM((2, e, COMBINE_ROWS, d), BF16),
                            pltpu.SemaphoreType.DMA((2, 2, e)),
                            pltpu.VMEM((COMBINE_ROWS, d), BF16), pltpu.VMEM((COMBINE_ROWS, d), BF16),
                            pltpu.SemaphoreType.DMA((2,)),
                            pltpu.VMEM((TOK_BLK, d), F32)]),
        out_shape=jax.ShapeDtypeStruct((n, d), F32),
        compiler_params=pltpu.CompilerParams(dimension_semantics=("arbitrary",), vmem_limit_bytes=VMEM_LIMIT_BYTES),
        name="combine_post_norm",
    )(base.reshape(-1), cnt.reshape(-1), rm_t, x1, mod.reshape(nm, 6, 1, d), ln_g.reshape(1, -1),
      ln_b.reshape(1, -1), ye_hi, ye_lo)


def expert_choice_ffn(hfs, logits, w_gate, w_up, w_down, layer):
    xes, gates, tables, offs, off = [], [], [], [], 0
    for hf, lg in zip(hfs, logits):
        aff_t = jax.nn.softmax(lg, axis=-1).T
        rm, idx = route(aff_t)
        xes.append(hf[idx])
        gates.append(jnp.take_along_axis(aff_t, idx, axis=1))
        tables.append(rm)
        offs.append(off)
        off += idx.shape[1]
    gate = jnp.concatenate(gates, axis=1)[..., None]
    hid = expert_hidden(jnp.concatenate(xes, axis=1), w_gate, w_up, layer)
    ye_hi, ye_lo = expert_down(hid, w_down, gate, layer)
    return ye_hi, ye_lo, tables, offs


_IN_SIZES = (QK_A, QK_A, V_A, V_A, 2 * H_A, 2 * H_A, QK_B, QK_B, V_B, V_B, 2 * H_B, 2 * H_B,
             3 * HY_CH, D_MODEL, D_MODEL, D_MODEL)
_IN_OFFS = tuple(int(v) for v in np.concatenate([[0], np.cumsum(_IN_SIZES)]))
_SMALL_PARTS = (4, 5, 10, 11)


def trunk_layer(x_sets, mods, lp, experts, layer, mats, states):
    hs = []
    for x, mod in zip(x_sets, mods):
        shift1, scale1 = mod[..., :D_MODEL], mod[..., D_MODEL:2 * D_MODEL]
        hs.append((_ln_stats(x) * (1.0 + scale1) + shift1).reshape(-1, D_MODEL))
    h_all = jnp.concatenate(hs, 0)

    big_cols = [p for p in range(len(_IN_SIZES)) if p not in _SMALL_PARTS]
    w_in = lp['w_in']
    w_big = jnp.concatenate([w_in[:, _IN_OFFS[p]:_IN_OFFS[p + 1]].astype(BF16) for p in big_cols], -1)
    w_small = jnp.concatenate([w_in[:, _IN_OFFS[p]:_IN_OFFS[p + 1]] for p in _SMALL_PARTS], -1)
    w_small = jnp.pad(w_small, ((0, 0), (0, LANES - w_small.shape[1])))
    proj_big = pmm(h_all, w_big)
    proj_small = pmm(h_all, w_small)

    mixed, new_states, row = [], [], 0
    for x, st in zip(x_sets, states):
        b, L, _ = x.shape
        outs, st_new = seq_mixers(proj_big, proj_small, row, b, L, lp, mats, *st)
        row += b * L
        mixed.append(outs)
        new_states.append(st_new)

    y_a, y_b, y_c = (jnp.concatenate(ts, 0) for ts in zip(*mixed))
    merged = branch_merge(y_a, y_b, y_c, lp['w_br_a'], lp['w_br_b'], lp['w_br_c'], proj_big)
    w_out = lp['w_out'].astype(BF16)

    x1s, h2s, logits, row = [], [], [], 0
    for x, mod in zip(x_sets, mods):
        b, L, _ = x.shape
        x1, h2, lg = mix_out(merged, row, x.reshape(b * L, D_MODEL), mod, L, w_out, lp['ln1_g'], lp['ln1_b'],
                             lp['w_router'], lp['b_router'])
        row += b * L
        x1s.append(x1)
        h2s.append(h2)
        logits.append(lg[:, :N_EXPERTS])
    ye_hi, ye_lo, tables, offs = expert_choice_ffn(h2s, logits, *experts, layer)
    out_sets = []
    for x, x1, rm, m0, mod in zip(x_sets, x1s, tables, offs, mods):
        out = combine_post_norm(ye_hi, ye_lo, m0, rm, x1, mod, x.shape[1], lp['ln2_g'], lp['ln2_b'])
        out_sets.append(out.reshape(x.shape))
    return out_sets, new_states


def kernel(x_prompt, x_sample, state_gdn, state_mlstm_c, state_mlstm_n, state_mlstm_m, c, c_ctx, w_ada, b_ada, w_in, gdn_conv, gdn_a_log, gdn_dt_bias, gdn_norm, ml_i_bias, ml_f_bias, ml_norm, hy_conv, hy_w1, hy_b1, hy_w2, hy_b2, hy_w3, hy_b3, hy_freq, hy_rate, hy_skip, w_br_a, w_br_b, w_br_c, w_out, ln1_g, ln1_b, ln2_g, ln2_b, w_router, b_router, w_gate, w_up, w_down):
    bp = x_prompt.shape[0]
    ls = x_sample.shape[1]
    xp = x_prompt
    xs = x_sample + grid_pos_embed(ls)[None]
    zero_states = (jnp.zeros((bp, 2, H_A, DK_A, DV_A), F32), jnp.zeros((bp, 2, H_B, DQK_B, DV_B), F32),
                   jnp.zeros((bp, 2, H_B, DQK_B), F32), jnp.zeros((bp, 2, H_B), F32))
    cond = jax.nn.silu(jnp.concatenate([c_ctx[None], c], 0))
    cond = jnp.pad(cond, ((0, 16 - cond.shape[0]), (0, 0))).astype(BF16)
    ctx_states = []
    mats = {}
    for L in (x_prompt.shape[1], ls):
        m32 = dft_matrices(L)
        mats[L] = (m32, tuple(m.astype(BF16) for m in m32))
    stacked = dict(w_in=w_in, gdn_conv=gdn_conv, gdn_a_log=gdn_a_log, gdn_dt_bias=gdn_dt_bias, gdn_norm=gdn_norm,
                   ml_i_bias=ml_i_bias, ml_f_bias=ml_f_bias, ml_norm=ml_norm, hy_conv=hy_conv, hy_w1=hy_w1,
                   hy_b1=hy_b1, hy_w2=hy_w2, hy_b2=hy_b2, hy_w3=hy_w3, hy_b3=hy_b3, hy_freq=hy_freq,
                   hy_rate=hy_rate, hy_skip=hy_skip, w_br_a=w_br_a, w_br_b=w_br_b, w_br_c=w_br_c, w_out=w_out,
                   ln1_g=ln1_g, ln1_b=ln1_b, ln2_g=ln2_g, ln2_b=ln2_b, w_router=w_router, b_router=b_router,
                   w_gate=w_gate, w_up=w_up, w_down=w_down)
    for l in range(DEPTH):
        lp = {name: t[l] for name, t in stacked.items()}
        mod = ada_modulation(cond, w_ada, b_ada, l)
        mod_ctx = mod[0].reshape(1, 1, 6 * D_MODEL)
        mod_lat = mod[1:1 + c.shape[0]][:, None, :]
        lat_states = (state_gdn[:, l], state_mlstm_c[:, l], state_mlstm_n[:, l], state_mlstm_m[:, l])
        (xp, xs), (st_ctx, _) = trunk_layer([xp, xs], [mod_ctx, mod_lat], lp, (w_gate, w_up, w_down), l, mats,
                                            [zero_states, lat_states])
        ctx_states.append(st_ctx)
    outs = tuple(jnp.stack([st[i] for st in ctx_states], 1) for i in range(4))
    return (xp, xs) + outs
```

```python
import functools
import math

import jax
import jax.numpy as jnp
import numpy as np
from jax import lax
from jax.experimental import pallas as pl
from jax.experimental.pallas import tpu as pltpu

D_MODEL = 2048
DEPTH = 4
GRID_W = 64
CHUNK = 64
H_A = 8
DK_A = 128
DV_A = 128
QK_A = H_A * DK_A
V_A = H_A * DV_A
H_B = 4
DQK_B = 128
DV_B = 256
QK_B = H_B * DQK_B
V_B = H_B * DV_B
HY_CH = 1024
HY_ORDER = 2
HY_BANDS = 16
N_EXPERTS = 16
D_EXPERT = 1024
CAP_FACTOR = 2
DN_ALPHA = (2 * DEPTH) ** 0.25
LN_EPS = 1e-6
F32 = jnp.float32
BF16 = jnp.bfloat16

LANES = 128
INTRA_GROUP = 4
ML_GROUP = 4
VMEM_LIMIT_BYTES = 48 * 1024 * 1024
VMEM_LIMIT_LARGE_BYTES = 56 * 1024 * 1024

COL_QA, COL_KA, COL_VA, COL_ZA = 0, QK_A, 2 * QK_A, 2 * QK_A + V_A
COL_QB = 2 * QK_A + 2 * V_A
COL_KB = COL_QB + QK_B
COL_VB = COL_KB + QK_B
COL_OB = COL_VB + V_B
COL_HY = COL_OB + V_B
COL_GA = COL_HY + 3 * HY_CH
COL_GB = COL_GA + D_MODEL
COL_GC = COL_GB + D_MODEL
N_BIG = COL_GC + D_MODEL


def _mm_kernel(x_ref, w_ref, o_ref):
    o_ref[...] = jnp.dot(x_ref[...], w_ref[...], preferred_element_type=F32).astype(o_ref.dtype)


def _pick(n, pref):
    for t in pref:
        if n % t == 0:
            return t
    return n


def pmm(x, w, out_dtype=F32):
    m, k = x.shape
    _, n = w.shape
    tm = _pick(m, (1024, 512, 256, 128, 16))
    tn = _pick(n, (512, 256, 128))
    return pl.pallas_call(
        _mm_kernel,
        grid=(m // tm, n // tn),
        in_specs=[pl.BlockSpec((tm, k), lambda i, j: (i, 0)),
                  pl.BlockSpec((k, tn), lambda i, j: (0, j))],
        out_specs=pl.BlockSpec((tm, tn), lambda i, j: (i, j)),
        out_shape=jax.ShapeDtypeStruct((m, n), out_dtype),
        compiler_params=pltpu.CompilerParams(
            dimension_semantics=("parallel", "parallel"), vmem_limit_bytes=VMEM_LIMIT_BYTES),
        name="pmm",
    )(x.astype(BF16), w.astype(BF16))


def _ada_kernel(x_ref, w_ref, b_ref, o_ref):
    o_ref[...] = jnp.dot(x_ref[...], w_ref[...].astype(BF16), preferred_element_type=F32) + b_ref[...]


def ada_modulation(cond, w_ada, b_ada, layer):
    m, k = cond.shape
    n = w_ada.shape[2]
    tn = 512
    return pl.pallas_call(
        _ada_kernel,
        grid=(n // tn,),
        in_specs=[pl.BlockSpec((m, k), lambda j: (0, 0)),
                  pl.BlockSpec((None, k, tn), lambda j: (layer, 0, j)),
                  pl.BlockSpec((None, 1, tn), lambda j: (layer, 0, j))],
        out_specs=pl.BlockSpec((m, tn), lambda j: (0, j)),
        out_shape=jax.ShapeDtypeStruct((m, n), F32),
        compiler_params=pltpu.CompilerParams(dimension_semantics=("parallel",), vmem_limit_bytes=VMEM_LIMIT_BYTES),
        name="ada_modulation",
    )(cond, w_ada, b_ada.reshape(b_ada.shape[0], 1, n))


def pbmm(x, w, out_dtype=F32):
    e, m, k = x.shape
    _, _, n = w.shape
    tm = _pick(m, (1024, 512, 256, 128, 8))
    tn = _pick(n, (512, 256, 128))
    return pl.pallas_call(
        _mm_kernel,
        grid=(e, m // tm, n // tn),
        in_specs=[pl.BlockSpec((None, tm, k), lambda b, i, j: (b, i, 0)),
                  pl.BlockSpec((None, k, tn), lambda b, i, j: (b, 0, j))],
        out_specs=pl.BlockSpec((None, tm, tn), lambda b, i, j: (b, i, j)),
        out_shape=jax.ShapeDtypeStruct((e, m, n), out_dtype),
        compiler_params=pltpu.CompilerParams(
            dimension_semantics=("parallel", "parallel", "parallel"), vmem_limit_bytes=VMEM_LIMIT_BYTES),
        name="pbmm",
    )(x.astype(BF16), w.astype(BF16))


def _merge_kernel(ya_ref, yb_ref, yc_ref, wa_ref, wb_ref, wc_ref, ga_ref, gb_ref, gc_ref, o_ref):
    acc = jax.nn.sigmoid(ga_ref[...]) * jnp.dot(ya_ref[...], wa_ref[...], preferred_element_type=F32)
    acc = acc + jax.nn.sigmoid(gb_ref[...]) * jnp.dot(yb_ref[...], wb_ref[...], preferred_element_type=F32)
    acc = acc + jax.nn.sigmoid(gc_ref[...]) * jnp.dot(yc_ref[...], wc_ref[...], preferred_element_type=F32)
    o_ref[...] = acc.astype(o_ref.dtype)


def branch_merge(y_a, y_b, y_c, w_a, w_b, w_c, proj):
    m = y_a.shape[0]
    tm, tn = _pick(m, (1024, 512, 256)), 512
    y_spec = lambda k: pl.BlockSpec((tm, k), lambda i, j: (i, 0))
    w_spec = lambda k: pl.BlockSpec((k, tn), lambda i, j: (0, j))
    g_spec = lambda col: pl.BlockSpec((tm, tn), lambda i, j, col=col: (i, col // tn + j))
    return pl.pallas_call(
        _merge_kernel,
        grid=(m // tm, D_MODEL // tn),
        in_specs=[y_spec(V_A), y_spec(V_B), y_spec(HY_CH), w_spec(V_A), w_spec(V_B), w_spec(HY_CH),
                  g_spec(COL_GA), g_spec(COL_GB), g_spec(COL_GC)],
        out_specs=pl.BlockSpec((tm, tn), lambda i, j: (i, j)),
        out_shape=jax.ShapeDtypeStruct((m, D_MODEL), BF16),
        compiler_params=pltpu.CompilerParams(
            dimension_semantics=("parallel", "parallel"), vmem_limit_bytes=VMEM_LIMIT_BYTES),
        name="branch_merge",
    )(y_a, y_b, y_c, w_a.astype(BF16), w_b.astype(BF16), w_c.astype(BF16), proj, proj, proj)


def _expert_hidden_kernel(x_ref, wg_ref, wu_ref, o_ref, wg_s, wu_s):
    @pl.when(pl.program_id(2) == 0)
    def _():
        wg_s[...] = wg_ref[...].astype(BF16)
        wu_s[...] = wu_ref[...].astype(BF16)

    x = x_ref[...]
    g = jnp.dot(x, wg_s[...], preferred_element_type=F32)
    u = jnp.dot(x, wu_s[...], preferred_element_type=F32)
    o_ref[...] = (g * jax.nn.sigmoid(g) * u).astype(o_ref.dtype)


def expert_hidden(xe, w_gate, w_up, layer):
    e, m, k = xe.shape
    f = w_gate.shape[3]
    tm, tn = _pick(m, (1280, 1024, 512, 256)), 512
    return pl.pallas_call(
        _expert_hidden_kernel,
        grid=(e, f // tn, m // tm),
        in_specs=[pl.BlockSpec((None, tm, k), lambda b, j, i: (b, i, 0)),
                  pl.BlockSpec((None, None, k, tn), lambda b, j, i: (layer, b, 0, j)),
                  pl.BlockSpec((None, None, k, tn), lambda b, j, i: (layer, b, 0, j))],
        out_specs=pl.BlockSpec((None, tm, tn), lambda b, j, i: (b, i, j)),
        out_shape=jax.ShapeDtypeStruct((e, m, f), BF16),
        scratch_shapes=[pltpu.VMEM((k, tn), BF16), pltpu.VMEM((k, tn), BF16)],
        compiler_params=pltpu.CompilerParams(
            dimension_semantics=("parallel", "parallel", "arbitrary"), vmem_limit_bytes=VMEM_LIMIT_BYTES),
        name="expert_hidden",
    )(xe, w_gate, w_up)


def _expert_down_kernel(h_ref, w_ref, g_ref, hi_ref, lo_ref, w_s):
    @pl.when(pl.program_id(2) == 0)
    def _():
        w_s[...] = w_ref[...].astype(BF16)

    y = jnp.dot(h_ref[...], w_s[...], preferred_element_type=F32) * g_ref[...]
    hi_ref[...], lo_ref[...] = _split(y)


def expert_down(hid, w_down, gate, layer):
    e, m, k = hid.shape
    n = w_down.shape[3]
    tm, tn = _pick(m, (1280, 1024, 512, 256)), 1024
    out = pl.BlockSpec((None, tm, tn), lambda b, j, i: (b, i, j))
    return pl.pallas_call(
        _expert_down_kernel,
        grid=(e, n // tn, m // tm),
        in_specs=[pl.BlockSpec((None, tm, k), lambda b, j, i: (b, i, 0)),
                  pl.BlockSpec((None, None, k, tn), lambda b, j, i: (layer, b, 0, j)),
                  pl.BlockSpec((None, tm, 1), lambda b, j, i: (b, i, 0))],
        out_specs=[out, out],
        out_shape=[jax.ShapeDtypeStruct((e, m, n), BF16), jax.ShapeDtypeStruct((e, m, n), BF16)],
        scratch_shapes=[pltpu.VMEM((k, tn), BF16)],
        compiler_params=pltpu.CompilerParams(
            dimension_semantics=("parallel", "parallel", "arbitrary"), vmem_limit_bytes=VMEM_LIMIT_BYTES),
        name="expert_down",
    )(hid, w_down, gate)


ROW_TILE = 256


def _ln(v):
    mu = jnp.mean(v, axis=-1, keepdims=True)
    d = v - mu
    return d * lax.rsqrt(jnp.mean(d * d, axis=-1, keepdims=True) + LN_EPS)


def _mix_out_kernel(m_ref, w_ref, x_ref, g1_ref, sh2_ref, sc2_ref, lng_ref, lnb_ref, wr_ref, br_ref,
                    x1_ref, h2_ref, lg_ref):
    y = jnp.dot(m_ref[...], w_ref[...], preferred_element_type=F32)
    x1 = _ln(DN_ALPHA * x_ref[...] + (1.0 + g1_ref[...]) * y) * lng_ref[...] + lnb_ref[...]
    x1_ref[...] = x1
    h2 = _ln(x1) * (1.0 + sc2_ref[...]) + sh2_ref[...]
    h2_ref[...] = h2.astype(h2_ref.dtype)
    lg_ref[...] = _mm3(_nn, h2, wr_ref[...]) + br_ref[...]


def mix_out(merged, row0, x, mod, L, w_out, ln_g, ln_b, w_router, b_router):
    n = x.shape[0]
    nm = mod.shape[0]
    mod4 = mod.reshape(nm, 6, 1, D_MODEL)
    per_seq = L // ROW_TILE
    mod_spec = lambda part: pl.BlockSpec(
        (None, None, 1, D_MODEL), lambda i, part=part: (0 if nm == 1 else i // per_seq, part, 0, 0))
    vec = pl.BlockSpec((1, D_MODEL), lambda i: (0, 0))
    wr = jnp.pad(w_router, ((0, 0), (0, LANES - N_EXPERTS)))
    br = jnp.pad(b_router, (0, LANES - N_EXPERTS)).reshape(1, LANES)
    return pl.pallas_call(
        _mix_out_kernel,
        grid=(n // ROW_TILE,),
        in_specs=[pl.BlockSpec((ROW_TILE, D_MODEL), lambda i: (row0 // ROW_TILE + i, 0)),
                  pl.BlockSpec((D_MODEL, D_MODEL), lambda i: (0, 0)),
                  pl.BlockSpec((ROW_TILE, D_MODEL), lambda i: (i, 0)),
                  mod_spec(2), mod_spec(3), mod_spec(4), vec, vec,
                  pl.BlockSpec((D_MODEL, LANES), lambda i: (0, 0)),
                  pl.BlockSpec((1, LANES), lambda i: (0, 0))],
        out_specs=[pl.BlockSpec((ROW_TILE, D_MODEL), lambda i: (i, 0)),
                   pl.BlockSpec((ROW_TILE, D_MODEL), lambda i: (i, 0)),
                   pl.BlockSpec((ROW_TILE, LANES), lambda i: (i, 0))],
        out_shape=[jax.ShapeDtypeStruct((n, D_MODEL), F32), jax.ShapeDtypeStruct((n, D_MODEL), BF16),
                   jax.ShapeDtypeStruct((n, LANES), F32)],
        compiler_params=pltpu.CompilerParams(dimension_semantics=("parallel",), vmem_limit_bytes=VMEM_LIMIT_BYTES),
        name="mix_out",
    )(merged, w_out, x, mod4, mod4, mod4, ln_g.reshape(1, -1), ln_b.reshape(1, -1), wr, br)


def _post_norm_kernel(x_ref, y_ref, gate_ref, lng_ref, lnb_ref, o_ref):
    o_ref[...] = _ln(DN_ALPHA * x_ref[...] + (1.0 + gate_ref[...]) * y_ref[...]) * lng_ref[...] + lnb_ref[...]


def post_norm(x, y, mod, part, L, ln_g, ln_b):
    n = x.shape[0]
    nm = mod.shape[0]
    per_seq = L // ROW_TILE
    rows = pl.BlockSpec((ROW_TILE, D_MODEL), lambda i: (i, 0))
    vec = pl.BlockSpec((1, D_MODEL), lambda i: (0, 0))
    return pl.pallas_call(
        _post_norm_kernel,
        grid=(n // ROW_TILE,),
        in_specs=[rows, rows,
                  pl.BlockSpec((None, None, 1, D_MODEL), lambda i: (0 if nm == 1 else i // per_seq, part, 0, 0)),
                  vec, vec],
        out_specs=rows,
        out_shape=jax.ShapeDtypeStruct((n, D_MODEL), F32),
        compiler_params=pltpu.CompilerParams(dimension_semantics=("parallel",), vmem_limit_bytes=VMEM_LIMIT_BYTES),
        name="post_norm",
    )(x, y, mod.reshape(nm, 6, 1, D_MODEL), ln_g.reshape(1, -1), ln_b.reshape(1, -1))


def _nn(a, b):
    return jnp.dot(a, b, preferred_element_type=F32)


def _nt(a, b):
    return lax.dot_general(a, b, (((1,), (1,)), ((), ())), preferred_element_type=F32)


def _tn(a, b):
    return lax.dot_general(a, b, (((0,), (0,)), ((), ())), preferred_element_type=F32)


def _split(a):
    hi = a.astype(BF16)
    return hi, (a - hi.astype(F32)).astype(BF16)


def _mm1(f, a, b):
    return f(a.astype(BF16), b.astype(BF16))


def _mm3(f, a, b):
    ah, al = _split(a)
    bh, bl = _split(b)
    return f(ah, bh) + (f(ah, bl) + f(al, bh))


def _chunk_masks():
    ri = lax.broadcasted_iota(jnp.int32, (CHUNK, CHUNK), 0)
    ci = lax.broadcasted_iota(jnp.int32, (CHUNK, CHUNK), 1)
    eye = ri == ci
    incl = (ri >= ci, ri <= ci)
    strict = (ri > ci, ri < ci)
    return eye, incl, strict


def _to_col(eye, row):
    return jnp.sum(jnp.where(eye, row, 0.0), axis=1, keepdims=True)


GC_BETA, GC_ALPHA, GC_IG, GC_FG, GC_END = 0, 2 * H_A, 4 * H_A, 4 * H_A + 2 * H_B, 4 * H_A + 4 * H_B


def _softplus(x):
    return jnp.maximum(x, 0.0) + jnp.log(1.0 + jnp.exp(-jnp.abs(x)))


def _gate_kernel(x_ref, p_ref, tab_ref, scal_ref, val_s, *, L):
    n = L // CHUNK
    x = x_ref[...]
    p = p_ref[...]
    lane = lax.broadcasted_iota(jnp.int32, (L, LANES), 1)
    beta = jax.nn.sigmoid(x)
    loga = p[0:1] * _softplus(x + p[1:2])
    ig = x + p[2:3]
    lf = -_softplus(-(x + p[3:4]))
    val_s[...] = jnp.where(lane < GC_ALPHA, beta, jnp.where(lane < GC_IG, loga, jnp.where(lane < GC_FG, ig, lf)))

    ri = lax.broadcasted_iota(jnp.int32, (CHUNK, CHUNK), 0)
    ci = lax.broadcasted_iota(jnp.int32, (CHUNK, CHUNK), 1)
    lower = jnp.where(ri >= ci, 1.0, 0.0).astype(BF16)
    upper = jnp.where(ri <= ci, 1.0, 0.0).astype(BF16)
    ident = jnp.where(ri == ci, 1.0, 0.0).astype(BF16)
    cl = lax.broadcasted_iota(jnp.int32, (CHUNK, LANES), 1)
    cumulative = ((cl >= GC_ALPHA) & (cl < GC_IG)) | ((cl >= GC_FG) & (cl < GC_END))
    backward = ((cl >= GC_ALPHA + H_A) & (cl < GC_IG)) | (cl >= GC_FG + H_B)

    def split3(v):
        h1 = v.astype(BF16)
        r1 = v - h1.astype(F32)
        h2 = r1.astype(BF16)
        return h1, h2, (r1 - h2.astype(F32)).astype(BF16)

    def chunk(c, carry):
        v = val_s[pl.ds(pl.multiple_of(c * CHUNK, CHUNK), CHUNK), :]
        parts = split3(v)
        pre = sum(_nn(lower, h) for h in parts)
        suf = sum(_nn(upper, h) for h in parts)
        out = jnp.where(cumulative, jnp.where(backward, suf, pre), v)
        tot = jnp.where(backward[0:1], suf[0:1], pre[CHUNK - 1:CHUNK])
        wlog = tot - out + pltpu.roll(out, GC_FG - GC_IG, 1)
        wmax = jnp.max(wlog, axis=0, keepdims=True)
        tab_ref[c] = sum(_tn(h, ident) for h in split3(out))
        scal_ref[c] = jnp.concatenate([tot, wmax, jnp.zeros((6, LANES), F32)], axis=0)
        return carry

    lax.fori_loop(0, n, chunk, 0)


def gate_tables(proj_small, row_blk0, nb, L, lp):
    n = L // CHUNK
    rows = [jnp.pad(v.reshape(-1), (off, LANES - off - v.size)) for v, off in (
        (-jnp.exp(lp['gdn_a_log']), GC_ALPHA), (lp['gdn_dt_bias'], GC_ALPHA),
        (lp['ml_i_bias'], GC_IG), (lp['ml_f_bias'], GC_FG))]
    params = jnp.stack(rows + [jnp.zeros((LANES,), F32)] * 4)
    return pl.pallas_call(
        functools.partial(_gate_kernel, L=L),
        grid=(nb,),
        in_specs=[pl.BlockSpec((L, LANES), lambda b: (row_blk0 + b, 0)),
                  pl.BlockSpec((8, LANES), lambda b: (0, 0))],
        out_specs=[pl.BlockSpec((None, n, LANES, CHUNK), lambda b: (b, 0, 0, 0)),
                   pl.BlockSpec((None, n, 8, LANES), lambda b: (b, 0, 0, 0))],
        out_shape=[jax.ShapeDtypeStruct((nb, n, LANES, CHUNK), F32),
                   jax.ShapeDtypeStruct((nb, n, 8, LANES), F32)],
        scratch_shapes=[pltpu.VMEM((L, LANES), F32)],
        compiler_params=pltpu.CompilerParams(dimension_semantics=("parallel",), vmem_limit_bytes=VMEM_LIMIT_BYTES),
        name="gate_tables",
    )(proj_small, params)


def _gate_row(tab_ref, c, lane):
    return tab_ref[c, pl.ds(lane, 1), :]


def _gate_scalar(scal_ref, c, row, lane):
    v = scal_ref[c][row:row + 1, :]
    li = lax.broadcasted_iota(jnp.int32, (1, LANES), 1)
    return jnp.sum(jnp.where(li == lane, v, 0.0), axis=1, keepdims=True)


def _gdn_kernel(q_ref, k_ref, v_ref, z_ref, cw_ref, tab_ref, scal_ref, s0_ref, nw_ref, y_ref, sf_ref,
                qs, ks, vs, wq_s, u_s, kd_s, p_s, o_s, st_s, *, L):
    n = L // CHUNK
    head = pl.program_id(1)
    beta_lane = [GC_BETA + d * H_A + head for d in range(2)]
    g_lane = [GC_ALPHA + d * H_A + head for d in range(2)]
    row = lax.broadcasted_iota(jnp.int32, (L, LANES), 0)

    def conv_silu(x_ref, part):
        x = x_ref[...]
        w = cw_ref[part]
        x_prev = jnp.where(row == 0, 0.0, pltpu.roll(x, 1, 0))
        x_next = jnp.where(row == L - 1, 0.0, pltpu.roll(x, L - 1, 0))
        y = x_prev * w[0:1] + x * w[1:2] + x_next * w[2:3]
        return y * jax.nn.sigmoid(y)

    def l2n(x):
        return x * lax.rsqrt(jnp.sum(x * x, axis=-1, keepdims=True) + LN_EPS)

    qs[...] = l2n(conv_silu(q_ref, 0)) * (DK_A ** -0.5)
    ks[...] = l2n(conv_silu(k_ref, 1))
    vs[...] = conv_silu(v_ref, 2)

    eye, incl, strict = _chunk_masks()
    eye_f = jnp.where(eye, 1.0, 0.0)

    def intra(grp, carry):
        chains = []
        for j in range(INTRA_GROUP):
            c = grp * INTRA_GROUP + j
            r0 = pl.multiple_of(c * CHUNK, CHUNK)
            qc = qs[pl.ds(r0, CHUNK), :]
            kc = ks[pl.ds(r0, CHUNK), :]
            vc = vs[pl.ds(r0, CHUNK), :]
            kk = _mm3(_nt, kc, kc)
            qk = _mm1(_nt, qc, kc)
            for d in range(2):
                g_row = _gate_row(tab_ref, c, g_lane[d])
                b_row = _gate_row(tab_ref, c, beta_lane[d])
                gl_row = _gate_scalar(scal_ref, c, 0, g_lane[d])
                g_col = _to_col(eye, g_row)
                b_col = _to_col(eye, b_row)
                dec = jnp.exp(jnp.where(incl[d], g_col - g_row, -jnp.inf))
                lmat = b_col * kk * jnp.where(strict[d], dec, 0.0)
                kd_s[d, pl.ds(r0, CHUNK), :] = jnp.exp(gl_row - g_col) * kc
                wq_s[d, c, CHUNK:, :] = jnp.exp(g_col) * qc
                p_s[d, c] = qk * dec
                chains.append(dict(d=d, c=c, r0=r0, pw=lmat, tinv=eye_f - lmat,
                                   rhs_w=(b_col * jnp.exp(g_col)) * kc, rhs_u=b_col * vc))
        for _ in range(int(math.log2(CHUNK)) - 1):
            for ch in chains:
                ch['pw'] = _mm3(_nn, ch['pw'], ch['pw'])
            for ch in chains:
                ch['tinv'] = ch['tinv'] + _mm3(_nn, ch['tinv'], ch['pw'])
        for ch in chains:
            d, r0 = ch['d'], ch['r0']
            wq_s[d, ch['c'], :CHUNK, :] = _mm3(_nn, ch['tinv'], ch['rhs_w'])
            u_s[d, pl.ds(r0, CHUNK), :] = _mm3(_nn, ch['tinv'], ch['rhs_u'])
        return carry

    lax.fori_loop(0, n // INTRA_GROUP, intra, 0)

    st_s[...] = s0_ref[...]

    def scan(i, carry):
        cs = (i, n - 1 - i)
        r0s = [pl.multiple_of(c * CHUNK, CHUNK) for c in cs]
        s = [st_s[d] for d in range(2)]
        ws = [_mm1(_nn, wq_s[d, cs[d]], s[d]) for d in range(2)]
        uc = [u_s[d, pl.ds(r0s[d], CHUNK), :] - ws[d][:CHUNK] for d in range(2)]
        pu = [_mm1(_nn, p_s[d, cs[d]], uc[d]) for d in range(2)]
        ku = [_mm1(_tn, kd_s[d, pl.ds(r0s[d], CHUNK), :], uc[d]) for d in range(2)]
        for d in range(2):
            o_s[d, pl.ds(r0s[d], CHUNK), :] = ws[d][CHUNK:] + pu[d]
            st_s[d] = jnp.exp(_gate_scalar(scal_ref, cs[d], 0, g_lane[d])) * s[d] + ku[d]
        return carry

    lax.fori_loop(0, n, scan, 0)

    o = o_s[0] + o_s[1]
    z = z_ref[...]
    o = o * lax.rsqrt(jnp.mean(o * o, axis=-1, keepdims=True) + LN_EPS) * nw_ref[...]
    y_ref[...] = (o * (z * jax.nn.sigmoid(z))).astype(y_ref.dtype)
    sf_ref[...] = st_s[...]


def _gate_specs(n):
    return [pl.BlockSpec((None, n, LANES, CHUNK), lambda b, h: (b, 0, 0, 0)),
            pl.BlockSpec((None, n, 8, LANES), lambda b, h: (b, 0, 0, 0))]


def gdn_branch(proj, row_blk0, nb, L, gates, conv_w, s0, norm_w):
    n = L // CHUNK
    seq = lambda col: pl.BlockSpec((L, LANES), lambda b, h, col=col: (row_blk0 + b, col // LANES + h))
    scr = lambda *s: pltpu.VMEM(s, F32)
    return pl.pallas_call(
        functools.partial(_gdn_kernel, L=L),
        grid=(nb, H_A),
        in_specs=[seq(COL_QA), seq(COL_KA), seq(COL_VA), seq(COL_ZA),
                  pl.BlockSpec((3, None, 3, LANES), lambda b, h: (0, h, 0, 0)),
                  *_gate_specs(n),
                  pl.BlockSpec((None, 2, None, DK_A, DV_A), lambda b, h: (b, 0, h, 0, 0)),
                  pl.BlockSpec((1, LANES), lambda b, h: (0, 0))],
        out_specs=[pl.BlockSpec((L, LANES), lambda b, h: (b, h)),
                   pl.BlockSpec((None, 2, None, DK_A, DV_A), lambda b, h: (b, 0, h, 0, 0))],
        out_shape=[jax.ShapeDtypeStruct((nb * L, V_A), BF16),
                   jax.ShapeDtypeStruct((nb, 2, H_A, DK_A, DV_A), F32)],
        scratch_shapes=[scr(L, LANES), scr(L, LANES), scr(L, LANES),
                        scr(2, n, 2 * CHUNK, LANES), scr(2, L, LANES), scr(2, L, LANES),
                        scr(2, n, CHUNK, CHUNK), scr(2, L, LANES), scr(2, DK_A, DV_A)],
        compiler_params=pltpu.CompilerParams(
            dimension_semantics=("parallel", "parallel"), vmem_limit_bytes=VMEM_LIMIT_BYTES),
        name="gdn_branch",
    )(proj, proj, proj, proj, conv_w, *gates, s0, norm_w)


def _mlstm_kernel(q_ref, k_ref, v_ref, ob_ref, tab_ref, scal_ref, c0_ref, n0_ref, m0_ref, nw_ref,
                  y_ref, cf_ref, nf_ref, mf_ref, h_s, c_s, n_s, m_s, *, L):
    n = L // CHUNK
    head = pl.program_id(1)
    ig_lane = [GC_IG + d * H_B + head for d in range(2)]
    fg_lane = [GC_FG + d * H_B + head for d in range(2)]
    eye, incl, _ = _chunk_masks()
    c_s[...] = c0_ref[...]
    n_s[...] = n0_ref[...]
    m_s[...] = m0_ref[...]

    def body(grp, carry):
        ch = []
        for j in range(ML_GROUP):
            i = grp * ML_GROUP + j
            for d in range(2):
                c = i if d == 0 else n - 1 - i
                r0 = pl.multiple_of(c * CHUNK, CHUNK)
                ch.append(dict(d=d, r0=r0, qc=q_ref[pl.ds(r0, CHUNK), :],
                               kc=k_ref[pl.ds(r0, CHUNK), :] * (DQK_B ** -0.5), vc=v_ref[pl.ds(r0, CHUNK), :],
                               b_row=_gate_row(tab_ref, c, fg_lane[d]), i_row=_gate_row(tab_ref, c, ig_lane[d]),
                               bl_row=_gate_scalar(scal_ref, c, 0, fg_lane[d]),
                               wm_row=_gate_scalar(scal_ref, c, 1, fg_lane[d])))
        for x in ch:
            x['qk'] = _mm1(_nt, x['qc'], x['kc'])
        for x in ch:
            x['b_col'] = _to_col(eye, x['b_row'])
            i_col = _to_col(eye, x['i_row'])
            dlog = jnp.where(incl[x['d']], x['b_col'] - x['b_row'] + x['i_row'], -jnp.inf)
            x['dmax'] = jnp.max(dlog, axis=1, keepdims=True)
            x['pw'] = jnp.exp(dlog - x['dmax']) * x['qk']
            x['ewk'] = jnp.exp(x['bl_row'][:, :1] - x['b_col'] + i_col - x['wm_row'][:, :1]) * x['kc']
        for x in ch:
            x['intra_num'] = _mm1(_nn, x['pw'], x['vc'])
            x['dc'] = _mm1(_tn, x['ewk'], x['vc'])
        for x in ch:
            x['intra_den'] = jnp.sum(x['pw'], axis=1, keepdims=True)
            x['dn'] = jnp.sum(x['ewk'], axis=0, keepdims=True)
        for x in ch:
            d, r0 = x['d'], x['r0']
            cm, nm, mm = c_s[d], n_s[d], m_s[d]
            alog = x['b_col'] + mm
            mt = jnp.maximum(alog, x['dmax'])
            wi = jnp.exp(alog - mt)[:, :1]
            wa = jnp.exp(x['dmax'] - mt)[:, :1]
            num = wi * _mm1(_nn, x['qc'], cm) + wa * x['intra_num']
            den = wi * jnp.sum(x['qc'] * nm, axis=1, keepdims=True) + wa * x['intra_den']
            h_s[d, pl.ds(r0, CHUNK), :] = num / jnp.maximum(jnp.abs(den), jnp.exp(-mt[:, :1]))

            m_new = jnp.maximum(x['bl_row'] + mm, x['wm_row'])
            a = jnp.exp(x['bl_row'] + mm - m_new)
            e = jnp.exp(x['wm_row'] - m_new)
            c_s[d] = a[:, :1] * cm + e[:, :1] * x['dc']
            n_s[d] = a * nm + e * x['dn']
            m_s[d] = m_new
        return carry

    lax.fori_loop(0, n // ML_GROUP, body, 0)

    h = h_s[0] + h_s[1]
    h = h * lax.rsqrt(jnp.mean(h * h, axis=-1, keepdims=True) + LN_EPS) * nw_ref[...]
    y_ref[...] = (h * jax.nn.sigmoid(ob_ref[...])).astype(y_ref.dtype)
    cf_ref[...] = c_s[...]
    nf_ref[...] = n_s[...]
    mf_ref[...] = m_s[...]


def mlstm_branch(proj, row_blk0, nb, L, gates, c0, n0, m0, norm_w):
    n = L // CHUNK
    seq = lambda col, w: pl.BlockSpec((L, w), lambda b, h, col=col, w=w: (row_blk0 + b, col // w + h))
    st = lambda *s: pl.BlockSpec((None, 2, None) + s, lambda b, h: (b, 0, h, 0, 0))
    scr = lambda *s: pltpu.VMEM(s, F32)
    return pl.pallas_call(
        functools.partial(_mlstm_kernel, L=L),
        grid=(nb, H_B),
        in_specs=[seq(COL_QB, DQK_B), seq(COL_KB, DQK_B), seq(COL_VB, DV_B), seq(COL_OB, DV_B),
                  *_gate_specs(n),
                  st(DQK_B, DV_B), st(1, DQK_B), st(1, LANES),
                  pl.BlockSpec((1, DV_B), lambda b, h: (0, 0))],
        out_specs=[pl.BlockSpec((L, DV_B), lambda b, h: (b, h)),
                   st(DQK_B, DV_B), st(1, DQK_B), st(1, LANES)],
        out_shape=[jax.ShapeDtypeStruct((nb * L, V_B), BF16),
                   jax.ShapeDtypeStruct((nb, 2, H_B, DQK_B, DV_B), F32),
                   jax.ShapeDtypeStruct((nb, 2, H_B, 1, DQK_B), F32),
                   jax.ShapeDtypeStruct((nb, 2, H_B, 1, LANES), F32)],
        scratch_shapes=[scr(2, L, DV_B), scr(2, DQK_B, DV_B), scr(2, 1, DQK_B), scr(2, 1, LANES)],
        compiler_params=pltpu.CompilerParams(
            dimension_semantics=("parallel", "parallel"), vmem_limit_bytes=VMEM_LIMIT_BYTES),
        name="mlstm_branch",
    )(proj, proj, proj, proj, *gates, c0, n0, m0, norm_w)


def _ln_stats(x):
    mu = jnp.mean(x, -1, keepdims=True)
    var = jnp.mean(jnp.square(x - mu), -1, keepdims=True)
    return (x - mu) * lax.rsqrt(var + LN_EPS)


def centred_conv(x, w):
    k = w.shape[0]
    p = k // 2
    L = x.shape[1]
    xp = jnp.pad(x, ((0, 0), (p, p), (0, 0)))
    y = xp[:, 0:L] * w[0]
    for j in range(1, k):
        y = y + xp[:, j:j + L] * w[j]
    return y


def hyena_filters(L, w1, b1, w2, b2, w3, b3, freq, rate):
    pos = jnp.arange(L, dtype=F32)
    t = pos / L
    ang = (2.0 * math.pi) * t[:, None] * jnp.arange(1, HY_BANDS + 1, dtype=F32)
    feats = jnp.concatenate([t[:, None], jnp.sin(ang), jnp.cos(ang)], -1)
    hp = lax.Precision.HIGHEST
    z = jnp.sin(freq[0] * (jnp.dot(feats, w1, precision=hp) + b1))
    z = jnp.sin(freq[1] * (jnp.dot(z, w2, precision=hp) + b2))
    filt = jnp.dot(z, w3, precision=hp) + b3
    lag = jnp.abs(pos - L // 2) / L
    filt = filt * jnp.exp(-lag[:, None] * rate)
    return filt.reshape(L, HY_ORDER, HY_CH)


def dft_size(L):
    n = 3 * L // 2
    return n if (n // 2) % LANES == 0 else 2 * L


def dft_matrices(L):
    N = dft_size(L)
    F = N // 2
    k = jnp.arange(F, dtype=jnp.int32)
    t = jnp.arange(L, dtype=jnp.int32)
    w = 2.0 * math.pi / N

    def cos_sin(rows, cols):
        hi = jnp.arange(cols.shape[0] // LANES, dtype=jnp.int32) * LANES
        lo = jnp.arange(LANES, dtype=jnp.int32)
        a = w * ((rows[:, None] * hi[None, :]) % N).astype(F32)[:, :, None]
        b = w * ((rows[:, None] * lo[None, :]) % N).astype(F32)[:, None, :]
        ca, sa, cb, sb = jnp.cos(a), jnp.sin(a), jnp.cos(b), jnp.sin(b)
        shape = (rows.shape[0], cols.shape[0])
        return (ca * cb - sa * sb).reshape(shape), (sa * cb + ca * sb).reshape(shape)

    alt = (1 - 2 * (t % 2)).astype(F32)
    cf, sin_f = cos_sin(k, t)
    sf = jnp.where(k[:, None] == 0, alt[None, :], -sin_f)
    tt = t + L // 2
    alt_i = (1 - 2 * (tt % 2)).astype(F32)
    cos_i, sin_i = cos_sin(tt, k)
    ci = jnp.where(k[None, :] == 0, 1.0 / N, (2.0 / N) * cos_i)
    si = jnp.where(k[None, :] == 0, alt_i[:, None] / N, (-2.0 / N) * sin_i)
    return cf, sf, ci, si


def _mm3_kernel(x_ref, w_ref, o_ref):
    o_ref[...] = _mm3(_nn, x_ref[...], w_ref[...])


def pmm3(x, w):
    m, k = x.shape
    _, n = w.shape
    tm = _pick(m, (512, 256, 128))
    tn = _pick(n, (512, 256, 128))
    return pl.pallas_call(
        _mm3_kernel,
        grid=(m // tm, n // tn),
        in_specs=[pl.BlockSpec((tm, k), lambda i, j: (i, 0)),
                  pl.BlockSpec((k, tn), lambda i, j: (0, j))],
        out_specs=pl.BlockSpec((tm, tn), lambda i, j: (i, j)),
        out_shape=jax.ShapeDtypeStruct((m, n), F32),
        compiler_params=pltpu.CompilerParams(
            dimension_semantics=("parallel", "parallel"), vmem_limit_bytes=VMEM_LIMIT_BYTES),
        name="pmm3",
    )(x, w)


def _hyena_kernel(z_ref, un_ref, cwz_ref, cwu_ref, skip_ref, hr_ref, hi_ref, cf_ref, sf_ref, ci_ref, si_ref,
                  o_ref, z_s, zb_s, acc_s, *, L, ft, conv_z):
    f = pl.program_id(2)
    row = lax.broadcasted_iota(jnp.int32, (L, z_ref.shape[1]), 0)

    def conv3(x, w):
        x_prev = jnp.where(row == 0, 0.0, pltpu.roll(x, 1, 0))
        x_next = jnp.where(row == L - 1, 0.0, pltpu.roll(x, L - 1, 0))
        return x_prev * w[0:1] + x * w[1:2] + x_next * w[2:3]

    @pl.when(f == 0)
    def _():
        z = z_ref[...]
        if conv_z:
            z = conv3(z, cwz_ref[...])
        z_s[...] = z
        zb_s[...] = z.astype(BF16)

    zb = zb_s[...]
    zr = _nn(cf_ref[...], zb)
    zi = _nn(sf_ref[...], zb)
    hr = hr_ref[...]
    hi = hi_ref[...]
    packed = (f * ft + lax.broadcasted_iota(jnp.int32, zr.shape, 0)) == 0
    zihi = zi * hi
    yr = zr * hr - jnp.where(packed, 0.0, zihi)
    yi = jnp.where(packed, zihi, zr * hi + zi * hr)
    part = _nn(ci_ref[...], yr.astype(BF16)) + _nn(si_ref[...], yi.astype(BF16))

    @pl.when(f == 0)
    def _():
        acc_s[...] = part

    @pl.when(f > 0)
    def _():
        acc_s[...] += part

    @pl.when(f == pl.num_programs(2) - 1)
    def _():
        un = conv3(un_ref[...], cwu_ref[...])
        o_ref[...] = (un * (acc_s[...] + skip_ref[...] * z_s[...])).astype(o_ref.dtype)


def hyena_order(z_arr, z_blk0, z_col, conv_z, proj, row_blk0, un_col, nb, L, conv_w, cwz_col, skip,
                hr, hi, h_col, mats, out_dtype=F32):
    cf, sf, ci, si = mats
    F = cf.shape[0]
    cb, ft = (HY_CH, _pick(F, (512, 256))) if L * HY_CH * 4 <= (1 << 20) else (HY_CH // 2, 256)
    seq = lambda blk0, col: pl.BlockSpec((L, cb), lambda b, j, f: (blk0 + b, col // cb + j))
    return pl.pallas_call(
        functools.partial(_hyena_kernel, L=L, ft=ft, conv_z=conv_z),
        grid=(nb, HY_CH // cb, F // ft),
        in_specs=[seq(z_blk0, z_col), seq(row_blk0, un_col),
                  pl.BlockSpec((3, cb), lambda b, j, f: (0, cwz_col // cb + j)),
                  pl.BlockSpec((3, cb), lambda b, j, f: (0, (un_col - COL_HY) // cb + j)),
                  pl.BlockSpec((1, cb), lambda b, j, f: (0, j)),
                  pl.BlockSpec((ft, cb), lambda b, j, f: (f, h_col // cb + j)),
                  pl.BlockSpec((ft, cb), lambda b, j, f: (f, h_col // cb + j)),
                  pl.BlockSpec((ft, L), lambda b, j, f: (f, 0)),
                  pl.BlockSpec((ft, L), lambda b, j, f: (f, 0)),
                  pl.BlockSpec((L, ft), lambda b, j, f: (0, f)),
                  pl.BlockSpec((L, ft), lambda b, j, f: (0, f))],
        out_specs=pl.BlockSpec((L, cb), lambda b, j, f: (b, j)),
        out_shape=jax.ShapeDtypeStruct((nb * L, HY_CH), out_dtype),
        scratch_shapes=[pltpu.VMEM((L, cb), F32), pltpu.VMEM((L, cb), BF16), pltpu.VMEM((L, cb), F32)],
        compiler_params=pltpu.CompilerParams(
            dimension_semantics=("parallel", "parallel", "arbitrary"), vmem_limit_bytes=VMEM_LIMIT_LARGE_BYTES),
        name="hyena_order",
    )(z_arr, proj, conv_w, conv_w, skip, hr, hi, cf, sf, ci, si)


def grid_pos_embed(n_tokens):
    rows = n_tokens // GRID_W
    quarter = D_MODEL // 4
    omega = 1.0 / (10000.0 ** (jnp.arange(quarter, dtype=F32) / quarter))
    r = jnp.arange(rows, dtype=F32)[:, None] * omega
    cl = jnp.arange(GRID_W, dtype=F32)[:, None] * omega
    full = lambda v, axis: jnp.broadcast_to(jnp.expand_dims(v, axis), (rows, GRID_W, quarter))
    pe = jnp.concatenate([full(jnp.sin(r), 1), full(jnp.cos(r), 1), full(jnp.sin(cl), 0), full(jnp.cos(cl), 0)], -1)
    return pe.reshape(rows * GRID_W, D_MODEL)


def seq_mixers(proj_big, proj_small, row0, nb, L, lp, mats, s_gdn, s_c, s_n, s_m):
    blk0 = row0 // L
    gates = gate_tables(proj_small, blk0, nb, L, lp)

    conv_w = lp['gdn_conv'].reshape(3, 3, H_A, DK_A).transpose(1, 2, 0, 3)
    y_a, s_gdn_new = gdn_branch(proj_big, blk0, nb, L, gates, conv_w, s_gdn, lp['gdn_norm'].reshape(1, DV_A))

    m0 = jnp.broadcast_to(s_m[..., None, None], s_m.shape + (1, LANES))
    y_b, c_new, n_new, m_new = mlstm_branch(proj_big, blk0, nb, L, gates, s_c, s_n[..., None, :], m0,
                                            lp['ml_norm'].reshape(1, DV_B))

    filt = hyena_filters(L, lp['hy_w1'], lp['hy_b1'], lp['hy_w2'], lp['hy_b2'], lp['hy_w3'], lp['hy_b3'],
                         lp['hy_freq'], lp['hy_rate']).reshape(L, HY_ORDER * HY_CH)
    mats32, mats16 = mats[L]
    hr, hi = pmm3(mats32[0], filt), pmm3(mats32[1], filt)
    z_arr, z_blk0, z_col = proj_big, blk0, COL_HY
    for order in range(HY_ORDER):
        z_arr = hyena_order(z_arr, z_blk0, z_col, order == 0, proj_big, blk0, COL_HY + (order + 1) * HY_CH, nb, L,
                            lp['hy_conv'], 0, lp['hy_skip'][order:order + 1], hr, hi, order * HY_CH, mats16,
                            out_dtype=BF16 if order == HY_ORDER - 1 else F32)
        z_blk0, z_col = 0, 0
    return (y_a, y_b, z_arr), (s_gdn_new, c_new, n_new[..., 0, :], m_new[..., 0, 0])


TOK_BLK = LANES
COMBINE_ROWS = 48
ROW_ALIGN = 16


def _route_kernel(a_ref, rm_ref, idx_ref, th_s, *, nblk, cap):
    bits = pltpu.bitcast(a_ref[...], jnp.int32)

    def search(i, th):
        cand = th | jnp.left_shift(jnp.int32(1), 30 - i)
        cnt = jnp.sum(jnp.sum(jnp.where(bits >= cand, 1.0, 0.0), axis=2, keepdims=True), axis=1, keepdims=True)
        return jnp.where(cnt >= cap, cand, th)

    th_s[...] = lax.fori_loop(0, 31, search, jnp.zeros((N_EXPERTS, 1, LANES), jnp.int32))

    li = lax.broadcasted_iota(jnp.int32, (LANES, LANES), 0)
    lj = lax.broadcasted_iota(jnp.int32, (LANES, LANES), 1)
    before_lane = jnp.where(li < lj, 1.0, 0.0).astype(BF16)
    bi = lax.broadcasted_iota(jnp.int32, (nblk, nblk), 0)
    bj = lax.broadcasted_iota(jnp.int32, (nblk, nblk), 1)
    before_blk = jnp.where(bj < bi, 1.0, 0.0).astype(BF16)
    diag = lax.broadcasted_iota(jnp.int32, (nblk, LANES), 0) == lax.broadcasted_iota(jnp.int32, (nblk, LANES), 1)
    slot = lax.broadcasted_iota(jnp.int32, (cap, LANES), 0).astype(F32)
    lane = lax.broadcasted_iota(jnp.int32, (cap, LANES), 1)
    lane_f = lane.astype(F32)

    def ranks(mask):
        m = mask.astype(BF16)
        inside = _nn(m, before_lane)
        cnt = jnp.sum(mask, axis=1, keepdims=True)
        earlier = _nn(before_blk, jnp.broadcast_to(cnt, (nblk, LANES)).astype(BF16))
        return inside, earlier, cnt

    def per_expert(e, carry):
        b = pltpu.bitcast(a_ref[e], jnp.int32)
        th = th_s[e]
        gt = jnp.where(b > th, 1.0, 0.0)
        eq = jnp.where(b == th, 1.0, 0.0)
        need = cap - jnp.sum(jnp.sum(gt, axis=1, keepdims=True), axis=0, keepdims=True)
        eq_in, eq_before, _ = ranks(eq)
        sel = jnp.maximum(gt, jnp.where(eq_in + eq_before < need, eq, 0.0))
        inside, earlier, cnt = ranks(sel)
        rm = jnp.where(sel > 0.0, inside, -1.0)
        rm_ref[e] = rm
        end_row = jnp.sum(jnp.where(diag, earlier + cnt, 0.0), axis=0, keepdims=True)
        cnt_row = jnp.sum(jnp.where(diag, jnp.broadcast_to(cnt, (nblk, LANES)), 0.0), axis=0, keepdims=True)
        done = jnp.where((lane < nblk) & (end_row <= slot), 1.0, 0.0)
        blk = jnp.sum(done, axis=1, keepdims=True)
        local = slot[:, :1] - jnp.sum(done * cnt_row, axis=1, keepdims=True)
        onehot = jnp.where(lane_f == blk, 1.0, 0.0).astype(BF16)
        rows = _nn(onehot[:, :nblk], rm.astype(BF16))
        tok_lane = jnp.sum(jnp.where(rows == local, lane_f, 0.0), axis=1, keepdims=True)
        idx_ref[e] = (blk * TOK_BLK + tok_lane).astype(jnp.int32)
        return carry

    lax.fori_loop(0, N_EXPERTS, per_expert, 0)


def route(aff_t):
    e, n = aff_t.shape
    nblk, cap = n // TOK_BLK, CAP_FACTOR * n // N_EXPERTS
    full = lambda *s: pl.BlockSpec(s, lambda i: (0,) * len(s))
    rm, idx = pl.pallas_call(
        functools.partial(_route_kernel, nblk=nblk, cap=cap),
        grid=(1,),
        in_specs=[full(e, nblk, TOK_BLK)],
        out_specs=[full(e, nblk, TOK_BLK), full(e, cap, 1)],
        out_shape=[jax.ShapeDtypeStruct((e, nblk, TOK_BLK), F32), jax.ShapeDtypeStruct((e, cap, 1), jnp.int32)],
        scratch_shapes=[pltpu.VMEM((e, 1, LANES), jnp.int32)],
        compiler_params=pltpu.CompilerParams(dimension_semantics=("arbitrary",), vmem_limit_bytes=VMEM_LIMIT_BYTES),
        name="route",
    )(aff_t.reshape(e, nblk, TOK_BLK))
    return rm, idx[..., 0]


def _combine_kernel(base_ref, cnt_ref, rm_ref, x_ref, gate_ref, lng_ref, lnb_ref, hi_ref, lo_ref, o_ref,
                    buf_hi, buf_lo, sem, xbuf_hi, xbuf_lo, xsem, acc, *, m_rows, nblk):
    b = pl.program_id(0)
    slot = b % 2
    w_rows = COMBINE_ROWS
    r_iota = lax.broadcasted_iota(jnp.int32, (w_rows, LANES), 0).astype(F32)

    def first_row(blk, e):
        return (base_ref[blk * N_EXPERTS + e] // ROW_ALIGN) * ROW_ALIGN

    def window(blk, e, w):
        return pl.multiple_of(jnp.minimum(first_row(blk, e) + w * w_rows, m_rows - w_rows), ROW_ALIGN)

    def copies(blk, e, s):
        start = window(blk, e, 0)
        return (pltpu.make_async_copy(hi_ref.at[e, pl.ds(start, w_rows), :], buf_hi.at[s, e], sem.at[s, 0, e]),
                pltpu.make_async_copy(lo_ref.at[e, pl.ds(start, w_rows), :], buf_lo.at[s, e], sem.at[s, 1, e]))

    def issue(blk, s):
        for e in range(N_EXPERTS):
            for cp in copies(blk, e, s):
                cp.start()

    @pl.when(b == 0)
    def _():
        issue(0, 0)

    @pl.when(b + 1 < nblk)
    def _():
        issue(b + 1, 1 - slot)

    def placement(e, w, start):
        rank = rm_ref[e:e + 1, :]
        rel = rank + (base_ref[b * N_EXPERTS + e] - first_row(b, e)).astype(F32)
        lo_row = w * jnp.float32(w_rows)
        mine = (rank >= 0.0) & (rel >= lo_row) & (rel < lo_row + w_rows)
        target = jnp.where(mine, rel + (first_row(b, e) - start).astype(F32), -1.0)
        return jnp.where(r_iota == target, 1.0, 0.0).astype(BF16)

    for e in range(N_EXPERTS):
        for cp in copies(b, e, slot):
            cp.wait()
    put = jnp.concatenate([placement(e, 0, window(b, e, 0)) for e in range(N_EXPERTS)], axis=0)
    rows_hi = buf_hi[slot].reshape(N_EXPERTS * w_rows, -1)
    rows_lo = buf_lo[slot].reshape(N_EXPERTS * w_rows, -1)
    acc[...] = _tn(put, rows_hi) + _tn(put, rows_lo)

    for e in range(N_EXPERTS):
        used = base_ref[b * N_EXPERTS + e] - first_row(b, e) + cnt_ref[b * N_EXPERTS + e]

        def extra(w, carry, e=e):
            s = window(b, e, w)
            c_hi = pltpu.make_async_copy(hi_ref.at[e, pl.ds(s, w_rows), :], xbuf_hi, xsem.at[0])
            c_lo = pltpu.make_async_copy(lo_ref.at[e, pl.ds(s, w_rows), :], xbuf_lo, xsem.at[1])
            c_hi.start()
            c_lo.start()
            c_hi.wait()
            c_lo.wait()
            p = placement(e, w, s)
            acc[...] += _tn(p, xbuf_hi[...]) + _tn(p, xbuf_lo[...])
            return carry

        lax.fori_loop(1, (used + w_rows - 1) // w_rows, extra, 0)

    o_ref[...] = _ln(DN_ALPHA * x_ref[...] + (1.0 + gate_ref[...]) * acc[...]) * lng_ref[...] + lnb_ref[...]


def combine_post_norm(ye_hi, ye_lo, m0, rm, x1, mod, L, ln_g, ln_b):
    e, m_rows, d = ye_hi.shape
    n = x1.shape[0]
    nblk = n // TOK_BLK
    nm = mod.shape[0]
    per_seq = L // TOK_BLK
    rm_t = rm.transpose(1, 0, 2)
    cnt = jnp.sum(rm_t >= 0.0, axis=-1).astype(jnp.int32)
    base = m0 + jnp.cumsum(cnt, axis=0) - cnt
    rows = pl.BlockSpec((TOK_BLK, d), lambda i, *_: (i, 0))
    vec = pl.BlockSpec((1, d), lambda i, *_: (0, 0))
    return pl.pallas_call(
        functools.partial(_combine_kernel, m_rows=m_rows, nblk=nblk),
        grid_spec=pltpu.PrefetchScalarGridSpec(
            num_scalar_prefetch=2,
            grid=(nblk,),
            in_specs=[pl.BlockSpec((None, e, TOK_BLK), lambda i, *_: (i, 0, 0)),
                      rows,
                      pl.BlockSpec((None, None, 1, d), lambda i, *_: (0 if nm == 1 else i // per_seq, 5, 0, 0)),
                      vec, vec,
                      pl.BlockSpec(memory_space=pl.ANY), pl.BlockSpec(memory_space=pl.ANY)],
            out_specs=rows,
            scratch_shapes=[pltpu.VMEM((2, e, COMBINE_ROWS, d), BF16), pltpu.VMEM((2, e, COMBINE_ROWS, d), BF16),
                            pltpu.SemaphoreType.DMA((2, 2, e)),
                            pltpu.VMEM((COMBINE_ROWS, d), BF16), pltpu.VMEM((COMBINE_ROWS, d), BF16),
                            pltpu.SemaphoreType.DMA((2,)),
                            pltpu.VMEM((TOK_BLK, d), F32)]),
        out_shape=jax.ShapeDtypeStruct((n, d), F32),
        compiler_params=pltpu.CompilerParams(dimension_semantics=("arbitrary",), vmem_limit_bytes=VMEM_LIMIT_BYTES),
        name="combine_post_norm",
    )(base.reshape(-1), cnt.reshape(-1), rm_t, x1, mod.reshape(nm, 6, 1, d), ln_g.reshape(1, -1),
      ln_b.reshape(1, -1), ye_hi, ye_lo)


def expert_choice_ffn(hfs, logits, w_gate, w_up, w_down, layer):
    xes, gates, tables, offs, off = [], [], [], [], 0
    for hf, lg in zip(hfs, logits):
        aff_t = jax.nn.softmax(lg, axis=-1).T
        rm, idx = route(aff_t)
        xes.append(hf[idx])
        gates.append(jnp.take_along_axis(aff_t, idx, axis=1))
        tables.append(rm)
        offs.append(off)
        off += idx.shape[1]
    gate = jnp.concatenate(gates, axis=1)[..., None]
    hid = expert_hidden(jnp.concatenate(xes, axis=1), w_gate, w_up, layer)
    ye_hi, ye_lo = expert_down(hid, w_down, gate, layer)
    return ye_hi, ye_lo, tables, offs


_IN_SIZES = (QK_A, QK_A, V_A, V_A, 2 * H_A, 2 * H_A, QK_B, QK_B, V_B, V_B, 2 * H_B, 2 * H_B,
             3 * HY_CH, D_MODEL, D_MODEL, D_MODEL)
_IN_OFFS = tuple(int(v) for v in np.concatenate([[0], np.cumsum(_IN_SIZES)]))
_SMALL_PARTS = (4, 5, 10, 11)


def trunk_layer(x_sets, mods, lp, experts, layer, mats, states):
    big_cols = [p for p in range(len(_IN_SIZES)) if p not in _SMALL_PARTS]
    w_in = lp['w_in']
    w_big = jnp.concatenate([w_in[:, _IN_OFFS[p]:_IN_OFFS[p + 1]].astype(BF16) for p in big_cols], -1)
    w_small = jnp.concatenate([w_in[:, _IN_OFFS[p]:_IN_OFFS[p + 1]] for p in _SMALL_PARTS], -1)
    w_small = jnp.pad(w_small, ((0, 0), (0, LANES - w_small.shape[1]))).astype(BF16)
    w_out = lp['w_out'].astype(BF16)

    x1s, h2s, logits, new_states = [], [], [], []
    for x, mod, st in zip(x_sets, mods, states):
        b, L, _ = x.shape
        shift1, scale1 = mod[..., :D_MODEL], mod[..., D_MODEL:2 * D_MODEL]
        h = (_ln_stats(x) * (1.0 + scale1) + shift1).reshape(b * L, D_MODEL).astype(BF16)
        proj_big = pmm(h, w_big)
        proj_small = pmm(h, w_small)
        (y_a, y_b, y_c), st_new = seq_mixers(proj_big, proj_small, 0, b, L, lp, mats, *st)
        new_states.append(st_new)
        merged = branch_merge(y_a, y_b, y_c, lp['w_br_a'], lp['w_br_b'], lp['w_br_c'], proj_big)
        x1, h2, lg = mix_out(merged, 0, x.reshape(b * L, D_MODEL), mod, L, w_out, lp['ln1_g'], lp['ln1_b'],
                             lp['w_router'], lp['b_router'])
        x1s.append(x1)
        h2s.append(h2)
        logits.append(lg[:, :N_EXPERTS])
    ye_hi, ye_lo, tables, offs = expert_choice_ffn(h2s, logits, *experts, layer)
    out_sets = []
    for x, x1, rm, m0, mod in zip(x_sets, x1s, tables, offs, mods):
        out = combine_post_norm(ye_hi, ye_lo, m0, rm, x1, mod, x.shape[1], lp['ln2_g'], lp['ln2_b'])
        out_sets.append(out.reshape(x.shape))
    return out_sets, new_states


def kernel(x_prompt, x_sample, state_gdn, state_mlstm_c, state_mlstm_n, state_mlstm_m, c, c_ctx, w_ada, b_ada, w_in, gdn_conv, gdn_a_log, gdn_dt_bias, gdn_norm, ml_i_bias, ml_f_bias, ml_norm, hy_conv, hy_w1, hy_b1, hy_w2, hy_b2, hy_w3, hy_b3, hy_freq, hy_rate, hy_skip, w_br_a, w_br_b, w_br_c, w_out, ln1_g, ln1_b, ln2_g, ln2_b, w_router, b_router, w_gate, w_up, w_down):
    bp = x_prompt.shape[0]
    ls = x_sample.shape[1]
    xp = x_prompt
    xs = x_sample + grid_pos_embed(ls)[None]
    zero_states = (jnp.zeros((bp, 2, H_A, DK_A, DV_A), F32), jnp.zeros((bp, 2, H_B, DQK_B, DV_B), F32),
                   jnp.zeros((bp, 2, H_B, DQK_B), F32), jnp.zeros((bp, 2, H_B), F32))
    cond = jax.nn.silu(jnp.concatenate([c_ctx[None], c], 0))
    cond = jnp.pad(cond, ((0, 16 - cond.shape[0]), (0, 0))).astype(BF16)
    ctx_states = []
    mats = {}
    for L in (x_prompt.shape[1], ls):
        m32 = dft_matrices(L)
        mats[L] = (m32, tuple(m.astype(BF16) for m in m32))
    stacked = dict(w_in=w_in, gdn_conv=gdn_conv, gdn_a_log=gdn_a_log, gdn_dt_bias=gdn_dt_bias, gdn_norm=gdn_norm,
                   ml_i_bias=ml_i_bias, ml_f_bias=ml_f_bias, ml_norm=ml_norm, hy_conv=hy_conv, hy_w1=hy_w1,
                   hy_b1=hy_b1, hy_w2=hy_w2, hy_b2=hy_b2, hy_w3=hy_w3, hy_b3=hy_b3, hy_freq=hy_freq,
                   hy_rate=hy_rate, hy_skip=hy_skip, w_br_a=w_br_a, w_br_b=w_br_b, w_br_c=w_br_c, w_out=w_out,
                   ln1_g=ln1_g, ln1_b=ln1_b, ln2_g=ln2_g, ln2_b=ln2_b, w_router=w_router, b_router=b_router,
                   w_gate=w_gate, w_up=w_up, w_down=w_down)
    for l in range(DEPTH):
        lp = {name: t[l] for name, t in stacked.items()}
        mod = ada_modulation(cond, w_ada, b_ada, l)
        mod_ctx = mod[0].reshape(1, 1, 6 * D_MODEL)
        mod_lat = mod[1:1 + c.shape[0]][:, None, :]
        lat_states = (state_gdn[:, l], state_mlstm_c[:, l], state_mlstm_n[:, l], state_mlstm_m[:, l])
        (xp, xs), (st_ctx, _) = trunk_layer([xp, xs], [mod_ctx, mod_lat], lp, (w_gate, w_up, w_down), l, mats,
                                            [zero_states, lat_states])
        ctx_states.append(st_ctx)
    outs = tuple(jnp.stack([st[i] for st in ctx_states], 1) for i in range(4))
    return (xp, xs) + outs
```

```python
import functools
import math

import jax
import jax.numpy as jnp
import numpy as np
from jax import lax
from jax.experimental import pallas as pl
from jax.experimental.pallas import tpu as pltpu

D_MODEL = 2048
DEPTH = 4
GRID_W = 64
CHUNK = 64
H_A = 8
DK_A = 128
DV_A = 128
QK_A = H_A * DK_A
V_A = H_A * DV_A
H_B = 4
DQK_B = 128
DV_B = 256
QK_B = H_B * DQK_B
V_B = H_B * DV_B
HY_CH = 1024
HY_ORDER = 2
HY_BANDS = 16
N_EXPERTS = 16
D_EXPERT = 1024
CAP_FACTOR = 2
DN_ALPHA = (2 * DEPTH) ** 0.25
LN_EPS = 1e-6
F32 = jnp.float32
BF16 = jnp.bfloat16

LANES = 128
INTRA_GROUP = 8
ML_GROUP = 4
VMEM_LIMIT_BYTES = 48 * 1024 * 1024
VMEM_LIMIT_LARGE_BYTES = 56 * 1024 * 1024

COL_QA, COL_KA, COL_VA, COL_ZA = 0, QK_A, 2 * QK_A, 2 * QK_A + V_A
COL_QB = 2 * QK_A + 2 * V_A
COL_KB = COL_QB + QK_B
COL_VB = COL_KB + QK_B
COL_OB = COL_VB + V_B
COL_HY = COL_OB + V_B
COL_GA = COL_HY + 3 * HY_CH
COL_GB = COL_GA + D_MODEL
COL_GC = COL_GB + D_MODEL
N_BIG = COL_GC + D_MODEL


def _mm_kernel(x_ref, w_ref, o_ref):
    o_ref[...] = jnp.dot(x_ref[...], w_ref[...], preferred_element_type=F32).astype(o_ref.dtype)


def _pick(n, pref):
    for t in pref:
        if n % t == 0:
            return t
    return n


def pmm(x, w, out_dtype=F32):
    m, k = x.shape
    _, n = w.shape
    tm = _pick(m, (1024, 512, 256, 128, 16))
    tn = _pick(n, (512, 256, 128))
    return pl.pallas_call(
        _mm_kernel,
        grid=(m // tm, n // tn),
        in_specs=[pl.BlockSpec((tm, k), lambda i, j: (i, 0)),
                  pl.BlockSpec((k, tn), lambda i, j: (0, j))],
        out_specs=pl.BlockSpec((tm, tn), lambda i, j: (i, j)),
        out_shape=jax.ShapeDtypeStruct((m, n), out_dtype),
        compiler_params=pltpu.CompilerParams(
            dimension_semantics=("parallel", "parallel"), vmem_limit_bytes=VMEM_LIMIT_BYTES),
        name="pmm",
    )(x.astype(BF16), w.astype(BF16))


def _ada_kernel(x_ref, w_ref, b_ref, o_ref):
    o_ref[...] = jnp.dot(x_ref[...], w_ref[...].astype(BF16), preferred_element_type=F32) + b_ref[...]


def ada_modulation(cond, w_ada, b_ada, layer):
    m, k = cond.shape
    n = w_ada.shape[2]
    tn = 512
    return pl.pallas_call(
        _ada_kernel,
        grid=(n // tn,),
        in_specs=[pl.BlockSpec((m, k), lambda j: (0, 0)),
                  pl.BlockSpec((None, k, tn), lambda j: (layer, 0, j)),
                  pl.BlockSpec((None, 1, tn), lambda j: (layer, 0, j))],
        out_specs=pl.BlockSpec((m, tn), lambda j: (0, j)),
        out_shape=jax.ShapeDtypeStruct((m, n), F32),
        compiler_params=pltpu.CompilerParams(dimension_semantics=("parallel",), vmem_limit_bytes=VMEM_LIMIT_BYTES),
        name="ada_modulation",
    )(cond, w_ada, b_ada.reshape(b_ada.shape[0], 1, n))


def pbmm(x, w, out_dtype=F32):
    e, m, k = x.shape
    _, _, n = w.shape
    tm = _pick(m, (1024, 512, 256, 128, 8))
    tn = _pick(n, (512, 256, 128))
    return pl.pallas_call(
        _mm_kernel,
        grid=(e, m // tm, n // tn),
        in_specs=[pl.BlockSpec((None, tm, k), lambda b, i, j: (b, i, 0)),
                  pl.BlockSpec((None, k, tn), lambda b, i, j: (b, 0, j))],
        out_specs=pl.BlockSpec((None, tm, tn), lambda b, i, j: (b, i, j)),
        out_shape=jax.ShapeDtypeStruct((e, m, n), out_dtype),
        compiler_params=pltpu.CompilerParams(
            dimension_semantics=("parallel", "parallel", "parallel"), vmem_limit_bytes=VMEM_LIMIT_BYTES),
        name="pbmm",
    )(x.astype(BF16), w.astype(BF16))


def _merge_kernel(ya_ref, yb_ref, yc_ref, wa_ref, wb_ref, wc_ref, ga_ref, gb_ref, gc_ref, o_ref):
    acc = jax.nn.sigmoid(ga_ref[...]) * jnp.dot(ya_ref[...], wa_ref[...], preferred_element_type=F32)
    acc = acc + jax.nn.sigmoid(gb_ref[...]) * jnp.dot(yb_ref[...], wb_ref[...], preferred_element_type=F32)
    acc = acc + jax.nn.sigmoid(gc_ref[...]) * jnp.dot(yc_ref[...], wc_ref[...], preferred_element_type=F32)
    o_ref[...] = acc.astype(o_ref.dtype)


def branch_merge(y_a, y_b, y_c, w_a, w_b, w_c, proj):
    m = y_a.shape[0]
    tm, tn = _pick(m, (1024, 512, 256)), 512
    y_spec = lambda k: pl.BlockSpec((tm, k), lambda i, j: (i, 0))
    w_spec = lambda k: pl.BlockSpec((k, tn), lambda i, j: (0, j))
    g_spec = lambda col: pl.BlockSpec((tm, tn), lambda i, j, col=col: (i, col // tn + j))
    return pl.pallas_call(
        _merge_kernel,
        grid=(m // tm, D_MODEL // tn),
        in_specs=[y_spec(V_A), y_spec(V_B), y_spec(HY_CH), w_spec(V_A), w_spec(V_B), w_spec(HY_CH),
                  g_spec(COL_GA), g_spec(COL_GB), g_spec(COL_GC)],
        out_specs=pl.BlockSpec((tm, tn), lambda i, j: (i, j)),
        out_shape=jax.ShapeDtypeStruct((m, D_MODEL), BF16),
        compiler_params=pltpu.CompilerParams(
            dimension_semantics=("parallel", "parallel"), vmem_limit_bytes=VMEM_LIMIT_BYTES),
        name="branch_merge",
    )(y_a, y_b, y_c, w_a.astype(BF16), w_b.astype(BF16), w_c.astype(BF16), proj, proj, proj)


def _expert_hidden_kernel(x_ref, wg_ref, wu_ref, o_ref, wg_s, wu_s):
    @pl.when(pl.program_id(2) == 0)
    def _():
        wg_s[...] = wg_ref[...].astype(BF16)
        wu_s[...] = wu_ref[...].astype(BF16)

    x = x_ref[...]
    g = jnp.dot(x, wg_s[...], preferred_element_type=F32)
    u = jnp.dot(x, wu_s[...], preferred_element_type=F32)
    o_ref[...] = (g * jax.nn.sigmoid(g) * u).astype(o_ref.dtype)


def expert_hidden(xe, w_gate, w_up, layer):
    e, m, k = xe.shape
    f = w_gate.shape[3]
    tm, tn = _pick(m, (1280, 1024, 512, 256)), 512
    return pl.pallas_call(
        _expert_hidden_kernel,
        grid=(e, f // tn, m // tm),
        in_specs=[pl.BlockSpec((None, tm, k), lambda b, j, i: (b, i, 0)),
                  pl.BlockSpec((None, None, k, tn), lambda b, j, i: (layer, b, 0, j)),
                  pl.BlockSpec((None, None, k, tn), lambda b, j, i: (layer, b, 0, j))],
        out_specs=pl.BlockSpec((None, tm, tn), lambda b, j, i: (b, i, j)),
        out_shape=jax.ShapeDtypeStruct((e, m, f), BF16),
        scratch_shapes=[pltpu.VMEM((k, tn), BF16), pltpu.VMEM((k, tn), BF16)],
        compiler_params=pltpu.CompilerParams(
            dimension_semantics=("parallel", "parallel", "arbitrary"), vmem_limit_bytes=VMEM_LIMIT_BYTES),
        name="expert_hidden",
    )(xe, w_gate, w_up)


def _expert_down_kernel(h_ref, w_ref, g_ref, hi_ref, lo_ref, w_s):
    @pl.when(pl.program_id(2) == 0)
    def _():
        w_s[...] = w_ref[...].astype(BF16)

    y = jnp.dot(h_ref[...], w_s[...], preferred_element_type=F32) * g_ref[...]
    hi_ref[...], lo_ref[...] = _split(y)


def expert_down(hid, w_down, gate, layer):
    e, m, k = hid.shape
    n = w_down.shape[3]
    tm, tn = _pick(m, (1280, 1024, 512, 256)), 1024
    out = pl.BlockSpec((None, tm, tn), lambda b, j, i: (b, i, j))
    return pl.pallas_call(
        _expert_down_kernel,
        grid=(e, n // tn, m // tm),
        in_specs=[pl.BlockSpec((None, tm, k), lambda b, j, i: (b, i, 0)),
                  pl.BlockSpec((None, None, k, tn), lambda b, j, i: (layer, b, 0, j)),
                  pl.BlockSpec((None, tm, 1), lambda b, j, i: (b, i, 0))],
        out_specs=[out, out],
        out_shape=[jax.ShapeDtypeStruct((e, m, n), BF16), jax.ShapeDtypeStruct((e, m, n), BF16)],
        scratch_shapes=[pltpu.VMEM((k, tn), BF16)],
        compiler_params=pltpu.CompilerParams(
            dimension_semantics=("parallel", "parallel", "arbitrary"), vmem_limit_bytes=VMEM_LIMIT_BYTES),
        name="expert_down",
    )(hid, w_down, gate)


ROW_TILE = 256


def _ln(v):
    mu = jnp.mean(v, axis=-1, keepdims=True)
    d = v - mu
    return d * lax.rsqrt(jnp.mean(d * d, axis=-1, keepdims=True) + LN_EPS)


def _mix_out_kernel(m_ref, w_ref, x_ref, g1_ref, sh2_ref, sc2_ref, lng_ref, lnb_ref, wr_ref, br_ref,
                    x1_ref, h2_ref, lg_ref):
    y = jnp.dot(m_ref[...], w_ref[...], preferred_element_type=F32)
    x1 = _ln(DN_ALPHA * x_ref[...] + (1.0 + g1_ref[...]) * y) * lng_ref[...] + lnb_ref[...]
    x1_ref[...] = x1
    h2 = _ln(x1) * (1.0 + sc2_ref[...]) + sh2_ref[...]
    h2_ref[...] = h2.astype(h2_ref.dtype)
    lg_ref[...] = _mm3(_nn, h2, wr_ref[...]) + br_ref[...]


def mix_out(merged, row0, x, mod, L, w_out, ln_g, ln_b, w_router, b_router):
    n = x.shape[0]
    nm = mod.shape[0]
    mod4 = mod.reshape(nm, 6, 1, D_MODEL)
    per_seq = L // ROW_TILE
    mod_spec = lambda part: pl.BlockSpec(
        (None, None, 1, D_MODEL), lambda i, part=part: (0 if nm == 1 else i // per_seq, part, 0, 0))
    vec = pl.BlockSpec((1, D_MODEL), lambda i: (0, 0))
    wr = jnp.pad(w_router, ((0, 0), (0, LANES - N_EXPERTS)))
    br = jnp.pad(b_router, (0, LANES - N_EXPERTS)).reshape(1, LANES)
    return pl.pallas_call(
        _mix_out_kernel,
        grid=(n // ROW_TILE,),
        in_specs=[pl.BlockSpec((ROW_TILE, D_MODEL), lambda i: (row0 // ROW_TILE + i, 0)),
                  pl.BlockSpec((D_MODEL, D_MODEL), lambda i: (0, 0)),
                  pl.BlockSpec((ROW_TILE, D_MODEL), lambda i: (i, 0)),
                  mod_spec(2), mod_spec(3), mod_spec(4), vec, vec,
                  pl.BlockSpec((D_MODEL, LANES), lambda i: (0, 0)),
                  pl.BlockSpec((1, LANES), lambda i: (0, 0))],
        out_specs=[pl.BlockSpec((ROW_TILE, D_MODEL), lambda i: (i, 0)),
                   pl.BlockSpec((ROW_TILE, D_MODEL), lambda i: (i, 0)),
                   pl.BlockSpec((ROW_TILE, LANES), lambda i: (i, 0))],
        out_shape=[jax.ShapeDtypeStruct((n, D_MODEL), F32), jax.ShapeDtypeStruct((n, D_MODEL), BF16),
                   jax.ShapeDtypeStruct((n, LANES), F32)],
        compiler_params=pltpu.CompilerParams(dimension_semantics=("parallel",), vmem_limit_bytes=VMEM_LIMIT_BYTES),
        name="mix_out",
    )(merged, w_out, x, mod4, mod4, mod4, ln_g.reshape(1, -1), ln_b.reshape(1, -1), wr, br)


def _post_norm_kernel(x_ref, y_ref, gate_ref, lng_ref, lnb_ref, o_ref):
    o_ref[...] = _ln(DN_ALPHA * x_ref[...] + (1.0 + gate_ref[...]) * y_ref[...]) * lng_ref[...] + lnb_ref[...]


def post_norm(x, y, mod, part, L, ln_g, ln_b):
    n = x.shape[0]
    nm = mod.shape[0]
    per_seq = L // ROW_TILE
    rows = pl.BlockSpec((ROW_TILE, D_MODEL), lambda i: (i, 0))
    vec = pl.BlockSpec((1, D_MODEL), lambda i: (0, 0))
    return pl.pallas_call(
        _post_norm_kernel,
        grid=(n // ROW_TILE,),
        in_specs=[rows, rows,
                  pl.BlockSpec((None, None, 1, D_MODEL), lambda i: (0 if nm == 1 else i // per_seq, part, 0, 0)),
                  vec, vec],
        out_specs=rows,
        out_shape=jax.ShapeDtypeStruct((n, D_MODEL), F32),
        compiler_params=pltpu.CompilerParams(dimension_semantics=("parallel",), vmem_limit_bytes=VMEM_LIMIT_BYTES),
        name="post_norm",
    )(x, y, mod.reshape(nm, 6, 1, D_MODEL), ln_g.reshape(1, -1), ln_b.reshape(1, -1))


def _nn(a, b):
    return jnp.dot(a, b, preferred_element_type=F32)


def _nt(a, b):
    return lax.dot_general(a, b, (((1,), (1,)), ((), ())), preferred_element_type=F32)


def _tn(a, b):
    return lax.dot_general(a, b, (((0,), (0,)), ((), ())), preferred_element_type=F32)


def _split(a):
    hi = a.astype(BF16)
    return hi, (a - hi.astype(F32)).astype(BF16)


def _mm1(f, a, b):
    return f(a.astype(BF16), b.astype(BF16))


def _mm3(f, a, b):
    ah, al = _split(a)
    bh, bl = _split(b)
    return f(ah, bh) + (f(ah, bl) + f(al, bh))


def _chunk_masks():
    ri = lax.broadcasted_iota(jnp.int32, (CHUNK, CHUNK), 0)
    ci = lax.broadcasted_iota(jnp.int32, (CHUNK, CHUNK), 1)
    eye = ri == ci
    incl = (ri >= ci, ri <= ci)
    strict = (ri > ci, ri < ci)
    return eye, incl, strict


def _to_col(eye, row):
    return jnp.sum(jnp.where(eye, row, 0.0), axis=1, keepdims=True)


GC_BETA, GC_ALPHA, GC_IG, GC_FG, GC_END = 0, 2 * H_A, 4 * H_A, 4 * H_A + 2 * H_B, 4 * H_A + 4 * H_B


def _softplus(x):
    return jnp.maximum(x, 0.0) + jnp.log(1.0 + jnp.exp(-jnp.abs(x)))


def _gate_kernel(x_ref, p_ref, tab_ref, scal_ref, val_s, *, L):
    n = L // CHUNK
    x = x_ref[...]
    p = p_ref[...]
    lane = lax.broadcasted_iota(jnp.int32, (L, LANES), 1)
    beta = jax.nn.sigmoid(x)
    loga = p[0:1] * _softplus(x + p[1:2])
    ig = x + p[2:3]
    lf = -_softplus(-(x + p[3:4]))
    val_s[...] = jnp.where(lane < GC_ALPHA, beta, jnp.where(lane < GC_IG, loga, jnp.where(lane < GC_FG, ig, lf)))

    ri = lax.broadcasted_iota(jnp.int32, (CHUNK, CHUNK), 0)
    ci = lax.broadcasted_iota(jnp.int32, (CHUNK, CHUNK), 1)
    lower = jnp.where(ri >= ci, 1.0, 0.0).astype(BF16)
    upper = jnp.where(ri <= ci, 1.0, 0.0).astype(BF16)
    ident = jnp.where(ri == ci, 1.0, 0.0).astype(BF16)
    cl = lax.broadcasted_iota(jnp.int32, (CHUNK, LANES), 1)
    cumulative = ((cl >= GC_ALPHA) & (cl < GC_IG)) | ((cl >= GC_FG) & (cl < GC_END))
    backward = ((cl >= GC_ALPHA + H_A) & (cl < GC_IG)) | (cl >= GC_FG + H_B)

    def split3(v):
        h1 = v.astype(BF16)
        r1 = v - h1.astype(F32)
        h2 = r1.astype(BF16)
        return h1, h2, (r1 - h2.astype(F32)).astype(BF16)

    def chunk(c, carry):
        v = val_s[pl.ds(pl.multiple_of(c * CHUNK, CHUNK), CHUNK), :]
        parts = split3(v)
        pre = sum(_nn(lower, h) for h in parts)
        suf = sum(_nn(upper, h) for h in parts)
        out = jnp.where(cumulative, jnp.where(backward, suf, pre), v)
        tot = jnp.where(backward[0:1], suf[0:1], pre[CHUNK - 1:CHUNK])
        wlog = tot - out + pltpu.roll(out, GC_FG - GC_IG, 1)
        wmax = jnp.max(wlog, axis=0, keepdims=True)
        tab_ref[c] = sum(_tn(h, ident) for h in split3(out))
        scal_ref[c] = jnp.concatenate([tot, wmax, jnp.zeros((6, LANES), F32)], axis=0)
        return carry

    lax.fori_loop(0, n, chunk, 0)


def gate_tables(proj_small, row_blk0, nb, L, lp):
    n = L // CHUNK
    rows = [jnp.pad(v.reshape(-1), (off, LANES - off - v.size)) for v, off in (
        (-jnp.exp(lp['gdn_a_log']), GC_ALPHA), (lp['gdn_dt_bias'], GC_ALPHA),
        (lp['ml_i_bias'], GC_IG), (lp['ml_f_bias'], GC_FG))]
    params = jnp.stack(rows + [jnp.zeros((LANES,), F32)] * 4)
    return pl.pallas_call(
        functools.partial(_gate_kernel, L=L),
        grid=(nb,),
        in_specs=[pl.BlockSpec((L, LANES), lambda b: (row_blk0 + b, 0)),
                  pl.BlockSpec((8, LANES), lambda b: (0, 0))],
        out_specs=[pl.BlockSpec((None, n, LANES, CHUNK), lambda b: (b, 0, 0, 0)),
                   pl.BlockSpec((None, n, 8, LANES), lambda b: (b, 0, 0, 0))],
        out_shape=[jax.ShapeDtypeStruct((nb, n, LANES, CHUNK), F32),
                   jax.ShapeDtypeStruct((nb, n, 8, LANES), F32)],
        scratch_shapes=[pltpu.VMEM((L, LANES), F32)],
        compiler_params=pltpu.CompilerParams(dimension_semantics=("parallel",), vmem_limit_bytes=VMEM_LIMIT_BYTES),
        name="gate_tables",
    )(proj_small, params)


def _gate_row(tab_ref, c, lane):
    return tab_ref[c, pl.ds(lane, 1), :]


def _gate_scalar(scal_ref, c, row, lane):
    v = scal_ref[c][row:row + 1, :]
    li = lax.broadcasted_iota(jnp.int32, (1, LANES), 1)
    return jnp.sum(jnp.where(li == lane, v, 0.0), axis=1, keepdims=True)


def _gdn_kernel(q_ref, k_ref, v_ref, z_ref, cw_ref, tab_ref, scal_ref, s0_ref, nw_ref, y_ref, sf_ref,
                qs, ks, vs, wq_s, u_s, kd_s, p_s, o_s, st_s, *, L):
    n = L // CHUNK
    head = pl.program_id(1)
    beta_lane = [GC_BETA + d * H_A + head for d in range(2)]
    g_lane = [GC_ALPHA + d * H_A + head for d in range(2)]
    row = lax.broadcasted_iota(jnp.int32, (L, LANES), 0)

    def conv_silu(x_ref, part):
        x = x_ref[...]
        w = cw_ref[part]
        x_prev = jnp.where(row == 0, 0.0, pltpu.roll(x, 1, 0))
        x_next = jnp.where(row == L - 1, 0.0, pltpu.roll(x, L - 1, 0))
        y = x_prev * w[0:1] + x * w[1:2] + x_next * w[2:3]
        return y * jax.nn.sigmoid(y)

    def l2n(x):
        return x * lax.rsqrt(jnp.sum(x * x, axis=-1, keepdims=True) + LN_EPS)

    qs[...] = l2n(conv_silu(q_ref, 0)) * (DK_A ** -0.5)
    ks[...] = l2n(conv_silu(k_ref, 1))
    vs[...] = conv_silu(v_ref, 2)

    eye, incl, strict = _chunk_masks()
    eye_f = jnp.where(eye, 1.0, 0.0)

    group = min(INTRA_GROUP, n)

    def intra(grp, carry):
        chains = []
        for j in range(group):
            c = grp * group + j
            r0 = pl.multiple_of(c * CHUNK, CHUNK)
            qc = qs[pl.ds(r0, CHUNK), :]
            kc = ks[pl.ds(r0, CHUNK), :]
            vc = vs[pl.ds(r0, CHUNK), :]
            kk = _mm3(_nt, kc, kc)
            qk = _mm1(_nt, qc, kc)
            for d in range(2):
                g_row = _gate_row(tab_ref, c, g_lane[d])
                b_row = _gate_row(tab_ref, c, beta_lane[d])
                gl_row = _gate_scalar(scal_ref, c, 0, g_lane[d])
                g_col = _to_col(eye, g_row)
                b_col = _to_col(eye, b_row)
                dec = jnp.exp(jnp.where(incl[d], g_col - g_row, -jnp.inf))
                lmat = b_col * kk * jnp.where(strict[d], dec, 0.0)
                kd_s[d, pl.ds(r0, CHUNK), :] = jnp.exp(gl_row - g_col) * kc
                wq_s[d, c, CHUNK:, :] = jnp.exp(g_col) * qc
                p_s[d, c] = qk * dec
                chains.append(dict(d=d, c=c, r0=r0, pw=lmat, tinv=eye_f - lmat,
                                   rhs_w=(b_col * jnp.exp(g_col)) * kc, rhs_u=b_col * vc))
        for _ in range(int(math.log2(CHUNK)) - 1):
            for ch in chains:
                ch['pw'] = _mm3(_nn, ch['pw'], ch['pw'])
            for ch in chains:
                ch['tinv'] = ch['tinv'] + _mm3(_nn, ch['tinv'], ch['pw'])
        for ch in chains:
            d, r0 = ch['d'], ch['r0']
            wq_s[d, ch['c'], :CHUNK, :] = _mm3(_nn, ch['tinv'], ch['rhs_w'])
            u_s[d, pl.ds(r0, CHUNK), :] = _mm3(_nn, ch['tinv'], ch['rhs_u'])
        return carry

    lax.fori_loop(0, n // group, intra, 0)

    st_s[...] = s0_ref[...]

    def scan(i, carry):
        cs = (i, n - 1 - i)
        r0s = [pl.multiple_of(c * CHUNK, CHUNK) for c in cs]
        s = [st_s[d] for d in range(2)]
        ws = [_mm1(_nn, wq_s[d, cs[d]], s[d]) for d in range(2)]
        uc = [u_s[d, pl.ds(r0s[d], CHUNK), :] - ws[d][:CHUNK] for d in range(2)]
        pu = [_mm1(_nn, p_s[d, cs[d]], uc[d]) for d in range(2)]
        ku = [_mm1(_tn, kd_s[d, pl.ds(r0s[d], CHUNK), :], uc[d]) for d in range(2)]
        for d in range(2):
            o_s[d, pl.ds(r0s[d], CHUNK), :] = ws[d][CHUNK:] + pu[d]
            st_s[d] = jnp.exp(_gate_scalar(scal_ref, cs[d], 0, g_lane[d])) * s[d] + ku[d]
        return carry

    lax.fori_loop(0, n, scan, 0)

    o = o_s[0] + o_s[1]
    z = z_ref[...]
    o = o * lax.rsqrt(jnp.mean(o * o, axis=-1, keepdims=True) + LN_EPS) * nw_ref[...]
    y_ref[...] = (o * (z * jax.nn.sigmoid(z))).astype(y_ref.dtype)
    sf_ref[...] = st_s[...]


def _gate_specs(n):
    return [pl.BlockSpec((None, n, LANES, CHUNK), lambda b, h: (b, 0, 0, 0)),
            pl.BlockSpec((None, n, 8, LANES), lambda b, h: (b, 0, 0, 0))]


def gdn_branch(proj, row_blk0, nb, L, gates, conv_w, s0, norm_w):
    n = L // CHUNK
    seq = lambda col: pl.BlockSpec((L, LANES), lambda b, h, col=col: (row_blk0 + b, col // LANES + h))
    scr = lambda *s: pltpu.VMEM(s, F32)
    return pl.pallas_call(
        functools.partial(_gdn_kernel, L=L),
        grid=(nb, H_A),
        in_specs=[seq(COL_QA), seq(COL_KA), seq(COL_VA), seq(COL_ZA),
                  pl.BlockSpec((3, None, 3, LANES), lambda b, h: (0, h, 0, 0)),
                  *_gate_specs(n),
                  pl.BlockSpec((None, 2, None, DK_A, DV_A), lambda b, h: (b, 0, h, 0, 0)),
                  pl.BlockSpec((1, LANES), lambda b, h: (0, 0))],
        out_specs=[pl.BlockSpec((L, LANES), lambda b, h: (b, h)),
                   pl.BlockSpec((None, 2, None, DK_A, DV_A), lambda b, h: (b, 0, h, 0, 0))],
        out_shape=[jax.ShapeDtypeStruct((nb * L, V_A), BF16),
                   jax.ShapeDtypeStruct((nb, 2, H_A, DK_A, DV_A), F32)],
        scratch_shapes=[scr(L, LANES), scr(L, LANES), scr(L, LANES),
                        scr(2, n, 2 * CHUNK, LANES), scr(2, L, LANES), scr(2, L, LANES),
                        scr(2, n, CHUNK, CHUNK), scr(2, L, LANES), scr(2, DK_A, DV_A)],
        compiler_params=pltpu.CompilerParams(
            dimension_semantics=("parallel", "parallel"), vmem_limit_bytes=VMEM_LIMIT_BYTES),
        name="gdn_branch",
    )(proj, proj, proj, proj, conv_w, *gates, s0, norm_w)


def _mlstm_kernel(q_ref, k_ref, v_ref, ob_ref, tab_ref, scal_ref, c0_ref, n0_ref, m0_ref, nw_ref,
                  y_ref, cf_ref, nf_ref, mf_ref, h_s, c_s, n_s, m_s, *, L):
    n = L // CHUNK
    head = pl.program_id(1)
    ig_lane = [GC_IG + d * H_B + head for d in range(2)]
    fg_lane = [GC_FG + d * H_B + head for d in range(2)]
    eye, incl, _ = _chunk_masks()
    c_s[...] = c0_ref[...]
    n_s[...] = n0_ref[...]
    m_s[...] = m0_ref[...]

    def body(grp, carry):
        ch = []
        for j in range(ML_GROUP):
            i = grp * ML_GROUP + j
            for d in range(2):
                c = i if d == 0 else n - 1 - i
                r0 = pl.multiple_of(c * CHUNK, CHUNK)
                ch.append(dict(d=d, r0=r0, qc=q_ref[pl.ds(r0, CHUNK), :],
                               kc=k_ref[pl.ds(r0, CHUNK), :] * (DQK_B ** -0.5), vc=v_ref[pl.ds(r0, CHUNK), :],
                               b_row=_gate_row(tab_ref, c, fg_lane[d]), i_row=_gate_row(tab_ref, c, ig_lane[d]),
                               bl_row=_gate_scalar(scal_ref, c, 0, fg_lane[d]),
                               wm_row=_gate_scalar(scal_ref, c, 1, fg_lane[d])))
        for x in ch:
            x['qk'] = _mm1(_nt, x['qc'], x['kc'])
        for x in ch:
            x['b_col'] = _to_col(eye, x['b_row'])
            i_col = _to_col(eye, x['i_row'])
            dlog = jnp.where(incl[x['d']], x['b_col'] - x['b_row'] + x['i_row'], -jnp.inf)
            x['dmax'] = jnp.max(dlog, axis=1, keepdims=True)
            x['pw'] = jnp.exp(dlog - x['dmax']) * x['qk']
            x['ewk'] = jnp.exp(x['bl_row'][:, :1] - x['b_col'] + i_col - x['wm_row'][:, :1]) * x['kc']
        for x in ch:
            x['intra_num'] = _mm1(_nn, x['pw'], x['vc'])
            x['dc'] = _mm1(_tn, x['ewk'], x['vc'])
        for x in ch:
            x['intra_den'] = jnp.sum(x['pw'], axis=1, keepdims=True)
            x['dn'] = jnp.sum(x['ewk'], axis=0, keepdims=True)
        for x in ch:
            d, r0 = x['d'], x['r0']
            cm, nm, mm = c_s[d], n_s[d], m_s[d]
            alog = x['b_col'] + mm
            mt = jnp.maximum(alog, x['dmax'])
            wi = jnp.exp(alog - mt)[:, :1]
            wa = jnp.exp(x['dmax'] - mt)[:, :1]
            num = wi * _mm1(_nn, x['qc'], cm) + wa * x['intra_num']
            den = wi * jnp.sum(x['qc'] * nm, axis=1, keepdims=True) + wa * x['intra_den']
            h_s[d, pl.ds(r0, CHUNK), :] = num / jnp.maximum(jnp.abs(den), jnp.exp(-mt[:, :1]))

            m_new = jnp.maximum(x['bl_row'] + mm, x['wm_row'])
            a = jnp.exp(x['bl_row'] + mm - m_new)
            e = jnp.exp(x['wm_row'] - m_new)
            c_s[d] = a[:, :1] * cm + e[:, :1] * x['dc']
            n_s[d] = a * nm + e * x['dn']
            m_s[d] = m_new
        return carry

    lax.fori_loop(0, n // ML_GROUP, body, 0)

    h = h_s[0] + h_s[1]
    h = h * lax.rsqrt(jnp.mean(h * h, axis=-1, keepdims=True) + LN_EPS) * nw_ref[...]
    y_ref[...] = (h * jax.nn.sigmoid(ob_ref[...])).astype(y_ref.dtype)
    cf_ref[...] = c_s[...]
    nf_ref[...] = n_s[...]
    mf_ref[...] = m_s[...]


def mlstm_branch(proj, row_blk0, nb, L, gates, c0, n0, m0, norm_w):
    n = L // CHUNK
    seq = lambda col, w: pl.BlockSpec((L, w), lambda b, h, col=col, w=w: (row_blk0 + b, col // w + h))
    st = lambda *s: pl.BlockSpec((None, 2, None) + s, lambda b, h: (b, 0, h, 0, 0))
    scr = lambda *s: pltpu.VMEM(s, F32)
    return pl.pallas_call(
        functools.partial(_mlstm_kernel, L=L),
        grid=(nb, H_B),
        in_specs=[seq(COL_QB, DQK_B), seq(COL_KB, DQK_B), seq(COL_VB, DV_B), seq(COL_OB, DV_B),
                  *_gate_specs(n),
                  st(DQK_B, DV_B), st(1, DQK_B), st(1, LANES),
                  pl.BlockSpec((1, DV_B), lambda b, h: (0, 0))],
        out_specs=[pl.BlockSpec((L, DV_B), lambda b, h: (b, h)),
                   st(DQK_B, DV_B), st(1, DQK_B), st(1, LANES)],
        out_shape=[jax.ShapeDtypeStruct((nb * L, V_B), BF16),
                   jax.ShapeDtypeStruct((nb, 2, H_B, DQK_B, DV_B), F32),
                   jax.ShapeDtypeStruct((nb, 2, H_B, 1, DQK_B), F32),
                   jax.ShapeDtypeStruct((nb, 2, H_B, 1, LANES), F32)],
        scratch_shapes=[scr(2, L, DV_B), scr(2, DQK_B, DV_B), scr(2, 1, DQK_B), scr(2, 1, LANES)],
        compiler_params=pltpu.CompilerParams(
            dimension_semantics=("parallel", "parallel"), vmem_limit_bytes=VMEM_LIMIT_BYTES),
        name="mlstm_branch",
    )(proj, proj, proj, proj, *gates, c0, n0, m0, norm_w)


def _ln_stats(x):
    mu = jnp.mean(x, -1, keepdims=True)
    var = jnp.mean(jnp.square(x - mu), -1, keepdims=True)
    return (x - mu) * lax.rsqrt(var + LN_EPS)


def centred_conv(x, w):
    k = w.shape[0]
    p = k // 2
    L = x.shape[1]
    xp = jnp.pad(x, ((0, 0), (p, p), (0, 0)))
    y = xp[:, 0:L] * w[0]
    for j in range(1, k):
        y = y + xp[:, j:j + L] * w[j]
    return y


def hyena_filters(L, w1, b1, w2, b2, w3, b3, freq, rate):
    pos = jnp.arange(L, dtype=F32)
    t = pos / L
    ang = (2.0 * math.pi) * t[:, None] * jnp.arange(1, HY_BANDS + 1, dtype=F32)
    feats = jnp.concatenate([t[:, None], jnp.sin(ang), jnp.cos(ang)], -1)
    hp = lax.Precision.HIGHEST
    z = jnp.sin(freq[0] * (jnp.dot(feats, w1, precision=hp) + b1))
    z = jnp.sin(freq[1] * (jnp.dot(z, w2, precision=hp) + b2))
    filt = jnp.dot(z, w3, precision=hp) + b3
    lag = jnp.abs(pos - L // 2) / L
    filt = filt * jnp.exp(-lag[:, None] * rate)
    return filt.reshape(L, HY_ORDER, HY_CH)


def dft_size(L):
    n = 3 * L // 2
    return n if (n // 2) % LANES == 0 else 2 * L


def dft_matrices(L):
    N = dft_size(L)
    F = N // 2
    k = jnp.arange(F, dtype=jnp.int32)
    t = jnp.arange(L, dtype=jnp.int32)
    w = 2.0 * math.pi / N

    def cos_sin(rows, cols):
        hi = jnp.arange(cols.shape[0] // LANES, dtype=jnp.int32) * LANES
        lo = jnp.arange(LANES, dtype=jnp.int32)
        a = w * ((rows[:, None] * hi[None, :]) % N).astype(F32)[:, :, None]
        b = w * ((rows[:, None] * lo[None, :]) % N).astype(F32)[:, None, :]
        ca, sa, cb, sb = jnp.cos(a), jnp.sin(a), jnp.cos(b), jnp.sin(b)
        shape = (rows.shape[0], cols.shape[0])
        return (ca * cb - sa * sb).reshape(shape), (sa * cb + ca * sb).reshape(shape)

    alt = (1 - 2 * (t % 2)).astype(F32)
    cf, sin_f = cos_sin(k, t)
    sf = jnp.where(k[:, None] == 0, alt[None, :], -sin_f)
    tt = t + L // 2
    alt_i = (1 - 2 * (tt % 2)).astype(F32)
    cos_i, sin_i = cos_sin(tt, k)
    ci = jnp.where(k[None, :] == 0, 1.0 / N, (2.0 / N) * cos_i)
    si = jnp.where(k[None, :] == 0, alt_i[:, None] / N, (-2.0 / N) * sin_i)
    return cf, sf, ci, si


def _mm3_kernel(x_ref, w_ref, o_ref):
    o_ref[...] = _mm3(_nn, x_ref[...], w_ref[...])


def pmm3(x, w):
    m, k = x.shape
    _, n = w.shape
    tm = _pick(m, (512, 256, 128))
    tn = _pick(n, (512, 256, 128))
    return pl.pallas_call(
        _mm3_kernel,
        grid=(m // tm, n // tn),
        in_specs=[pl.BlockSpec((tm, k), lambda i, j: (i, 0)),
                  pl.BlockSpec((k, tn), lambda i, j: (0, j))],
        out_specs=pl.BlockSpec((tm, tn), lambda i, j: (i, j)),
        out_shape=jax.ShapeDtypeStruct((m, n), F32),
        compiler_params=pltpu.CompilerParams(
            dimension_semantics=("parallel", "parallel"), vmem_limit_bytes=VMEM_LIMIT_BYTES),
        name="pmm3",
    )(x, w)


def _hyena_kernel(z_ref, un_ref, cwz_ref, cwu_ref, skip_ref, hr_ref, hi_ref, cf_ref, sf_ref, ci_ref, si_ref,
                  o_ref, z_s, zb_s, acc_s, *, L, ft, conv_z):
    f = pl.program_id(2)
    row = lax.broadcasted_iota(jnp.int32, (L, z_ref.shape[1]), 0)

    def conv3(x, w):
        x_prev = jnp.where(row == 0, 0.0, pltpu.roll(x, 1, 0))
        x_next = jnp.where(row == L - 1, 0.0, pltpu.roll(x, L - 1, 0))
        return x_prev * w[0:1] + x * w[1:2] + x_next * w[2:3]

    @pl.when(f == 0)
    def _():
        z = z_ref[...]
        if conv_z:
            z = conv3(z, cwz_ref[...])
        z_s[...] = z
        zb_s[...] = z.astype(BF16)

    zb = zb_s[...]
    zr = _nn(cf_ref[...], zb)
    zi = _nn(sf_ref[...], zb)
    hr = hr_ref[...]
    hi = hi_ref[...]
    packed = (f * ft + lax.broadcasted_iota(jnp.int32, zr.shape, 0)) == 0
    zihi = zi * hi
    yr = zr * hr - jnp.where(packed, 0.0, zihi)
    yi = jnp.where(packed, zihi, zr * hi + zi * hr)
    part = _nn(ci_ref[...], yr.astype(BF16)) + _nn(si_ref[...], yi.astype(BF16))

    @pl.when(f == 0)
    def _():
        acc_s[...] = part

    @pl.when(f > 0)
    def _():
        acc_s[...] += part

    @pl.when(f == pl.num_programs(2) - 1)
    def _():
        un = conv3(un_ref[...], cwu_ref[...])
        o_ref[...] = (un * (acc_s[...] + skip_ref[...] * z_s[...])).astype(o_ref.dtype)


def hyena_order(z_arr, z_blk0, z_col, conv_z, proj, row_blk0, un_col, nb, L, conv_w, cwz_col, skip,
                hr, hi, h_col, mats, out_dtype=F32):
    cf, sf, ci, si = mats
    F = cf.shape[0]
    cb, ft = (HY_CH, _pick(F, (512, 256))) if L * HY_CH * 4 <= (1 << 20) else (HY_CH // 2, 256)
    seq = lambda blk0, col: pl.BlockSpec((L, cb), lambda b, j, f: (blk0 + b, col // cb + j))
    return pl.pallas_call(
        functools.partial(_hyena_kernel, L=L, ft=ft, conv_z=conv_z),
        grid=(nb, HY_CH // cb, F // ft),
        in_specs=[seq(z_blk0, z_col), seq(row_blk0, un_col),
                  pl.BlockSpec((3, cb), lambda b, j, f: (0, cwz_col // cb + j)),
                  pl.BlockSpec((3, cb), lambda b, j, f: (0, (un_col - COL_HY) // cb + j)),
                  pl.BlockSpec((1, cb), lambda b, j, f: (0, j)),
                  pl.BlockSpec((ft, cb), lambda b, j, f: (f, h_col // cb + j)),
                  pl.BlockSpec((ft, cb), lambda b, j, f: (f, h_col // cb + j)),
                  pl.BlockSpec((ft, L), lambda b, j, f: (f, 0)),
                  pl.BlockSpec((ft, L), lambda b, j, f: (f, 0)),
                  pl.BlockSpec((L, ft), lambda b, j, f: (0, f)),
                  pl.BlockSpec((L, ft), lambda b, j, f: (0, f))],
        out_specs=pl.BlockSpec((L, cb), lambda b, j, f: (b, j)),
        out_shape=jax.ShapeDtypeStruct((nb * L, HY_CH), out_dtype),
        scratch_shapes=[pltpu.VMEM((L, cb), F32), pltpu.VMEM((L, cb), BF16), pltpu.VMEM((L, cb), F32)],
        compiler_params=pltpu.CompilerParams(
            dimension_semantics=("parallel", "parallel", "arbitrary"), vmem_limit_bytes=VMEM_LIMIT_LARGE_BYTES),
        name="hyena_order",
    )(z_arr, proj, conv_w, conv_w, skip, hr, hi, cf, sf, ci, si)


def grid_pos_embed(n_tokens):
    rows = n_tokens // GRID_W
    quarter = D_MODEL // 4
    omega = 1.0 / (10000.0 ** (jnp.arange(quarter, dtype=F32) / quarter))
    r = jnp.arange(rows, dtype=F32)[:, None] * omega
    cl = jnp.arange(GRID_W, dtype=F32)[:, None] * omega
    full = lambda v, axis: jnp.broadcast_to(jnp.expand_dims(v, axis), (rows, GRID_W, quarter))
    pe = jnp.concatenate([full(jnp.sin(r), 1), full(jnp.cos(r), 1), full(jnp.sin(cl), 0), full(jnp.cos(cl), 0)], -1)
    return pe.reshape(rows * GRID_W, D_MODEL)


def seq_mixers(proj_big, proj_small, row0, nb, L, lp, mats, s_gdn, s_c, s_n, s_m):
    blk0 = row0 // L
    gates = gate_tables(proj_small, blk0, nb, L, lp)

    conv_w = lp['gdn_conv'].reshape(3, 3, H_A, DK_A).transpose(1, 2, 0, 3)
    y_a, s_gdn_new = gdn_branch(proj_big, blk0, nb, L, gates, conv_w, s_gdn, lp['gdn_norm'].reshape(1, DV_A))

    m0 = jnp.broadcast_to(s_m[..., None, None], s_m.shape + (1, LANES))
    y_b, c_new, n_new, m_new = mlstm_branch(proj_big, blk0, nb, L, gates, s_c, s_n[..., None, :], m0,
                                            lp['ml_norm'].reshape(1, DV_B))

    filt = hyena_filters(L, lp['hy_w1'], lp['hy_b1'], lp['hy_w2'], lp['hy_b2'], lp['hy_w3'], lp['hy_b3'],
                         lp['hy_freq'], lp['hy_rate']).reshape(L, HY_ORDER * HY_CH)
    mats32, mats16 = mats[L]
    hr, hi = pmm3(mats32[0], filt), pmm3(mats32[1], filt)
    z_arr, z_blk0, z_col = proj_big, blk0, COL_HY
    for order in range(HY_ORDER):
        z_arr = hyena_order(z_arr, z_blk0, z_col, order == 0, proj_big, blk0, COL_HY + (order + 1) * HY_CH, nb, L,
                            lp['hy_conv'], 0, lp['hy_skip'][order:order + 1], hr, hi, order * HY_CH, mats16,
                            out_dtype=BF16 if order == HY_ORDER - 1 else F32)
        z_blk0, z_col = 0, 0
    return (y_a, y_b, z_arr), (s_gdn_new, c_new, n_new[..., 0, :], m_new[..., 0, 0])


TOK_BLK = LANES
COMBINE_ROWS = 48
ROW_ALIGN = 16


def _route_kernel(a_ref, rm_ref, idx_ref, th_s, *, nblk, cap):
    bits = pltpu.bitcast(a_ref[...], jnp.int32)

    def search(i, th):
        cand = th | jnp.left_shift(jnp.int32(1), 30 - i)
        cnt = jnp.sum(jnp.sum(jnp.where(bits >= cand, 1.0, 0.0), axis=2, keepdims=True), axis=1, keepdims=True)
        return jnp.where(cnt >= cap, cand, th)

    th_s[...] = lax.fori_loop(0, 31, search, jnp.zeros((N_EXPERTS, 1, LANES), jnp.int32))

    li = lax.broadcasted_iota(jnp.int32, (LANES, LANES), 0)
    lj = lax.broadcasted_iota(jnp.int32, (LANES, LANES), 1)
    before_lane = jnp.where(li < lj, 1.0, 0.0).astype(BF16)
    bi = lax.broadcasted_iota(jnp.int32, (nblk, nblk), 0)
    bj = lax.broadcasted_iota(jnp.int32, (nblk, nblk), 1)
    before_blk = jnp.where(bj < bi, 1.0, 0.0).astype(BF16)
    diag = lax.broadcasted_iota(jnp.int32, (nblk, LANES), 0) == lax.broadcasted_iota(jnp.int32, (nblk, LANES), 1)
    slot = lax.broadcasted_iota(jnp.int32, (cap, LANES), 0).astype(F32)
    lane = lax.broadcasted_iota(jnp.int32, (cap, LANES), 1)
    lane_f = lane.astype(F32)

    def ranks(mask):
        m = mask.astype(BF16)
        inside = _nn(m, before_lane)
        cnt = jnp.sum(mask, axis=1, keepdims=True)
        earlier = _nn(before_blk, jnp.broadcast_to(cnt, (nblk, LANES)).astype(BF16))
        return inside, earlier, cnt

    def per_expert(e, carry):
        b = pltpu.bitcast(a_ref[e], jnp.int32)
        th = th_s[e]
        gt = jnp.where(b > th, 1.0, 0.0)
        eq = jnp.where(b == th, 1.0, 0.0)
        need = cap - jnp.sum(jnp.sum(gt, axis=1, keepdims=True), axis=0, keepdims=True)
        eq_in, eq_before, _ = ranks(eq)
        sel = jnp.maximum(gt, jnp.where(eq_in + eq_before < need, eq, 0.0))
        inside, earlier, cnt = ranks(sel)
        rm = jnp.where(sel > 0.0, inside, -1.0)
        rm_ref[e] = rm
        end_row = jnp.sum(jnp.where(diag, earlier + cnt, 0.0), axis=0, keepdims=True)
        cnt_row = jnp.sum(jnp.where(diag, jnp.broadcast_to(cnt, (nblk, LANES)), 0.0), axis=0, keepdims=True)
        done = jnp.where((lane < nblk) & (end_row <= slot), 1.0, 0.0)
        blk = jnp.sum(done, axis=1, keepdims=True)
        local = slot[:, :1] - jnp.sum(done * cnt_row, axis=1, keepdims=True)
        onehot = jnp.where(lane_f == blk, 1.0, 0.0).astype(BF16)
        rows = _nn(onehot[:, :nblk], rm.astype(BF16))
        tok_lane = jnp.sum(jnp.where(rows == local, lane_f, 0.0), axis=1, keepdims=True)
        idx_ref[e] = (blk * TOK_BLK + tok_lane).astype(jnp.int32)
        return carry

    lax.fori_loop(0, N_EXPERTS, per_expert, 0)


def route(aff_t):
    e, n = aff_t.shape
    nblk, cap = n // TOK_BLK, CAP_FACTOR * n // N_EXPERTS
    full = lambda *s: pl.BlockSpec(s, lambda i: (0,) * len(s))
    rm, idx = pl.pallas_call(
        functools.partial(_route_kernel, nblk=nblk, cap=cap),
        grid=(1,),
        in_specs=[full(e, nblk, TOK_BLK)],
        out_specs=[full(e, nblk, TOK_BLK), full(e, cap, 1)],
        out_shape=[jax.ShapeDtypeStruct((e, nblk, TOK_BLK), F32), jax.ShapeDtypeStruct((e, cap, 1), jnp.int32)],
        scratch_shapes=[pltpu.VMEM((e, 1, LANES), jnp.int32)],
        compiler_params=pltpu.CompilerParams(dimension_semantics=("arbitrary",), vmem_limit_bytes=VMEM_LIMIT_BYTES),
        name="route",
    )(aff_t.reshape(e, nblk, TOK_BLK))
    return rm, idx[..., 0]


def _combine_kernel(base_ref, cnt_ref, rm_ref, x_ref, gate_ref, lng_ref, lnb_ref, hi_ref, lo_ref, o_ref,
                    buf_hi, buf_lo, sem, xbuf_hi, xbuf_lo, xsem, acc, *, m_rows, nblk):
    b = pl.program_id(0)
    slot = b % 2
    w_rows = COMBINE_ROWS
    r_iota = lax.broadcasted_iota(jnp.int32, (w_rows, LANES), 0).astype(F32)

    def first_row(blk, e):
        return (base_ref[blk * N_EXPERTS + e] // ROW_ALIGN) * ROW_ALIGN

    def window(blk, e, w):
        return pl.multiple_of(jnp.minimum(first_row(blk, e) + w * w_rows, m_rows - w_rows), ROW_ALIGN)

    def copies(blk, e, s):
        start = window(blk, e, 0)
        return (pltpu.make_async_copy(hi_ref.at[e, pl.ds(start, w_rows), :], buf_hi.at[s, e], sem.at[s, 0, e]),
                pltpu.make_async_copy(lo_ref.at[e, pl.ds(start, w_rows), :], buf_lo.at[s, e], sem.at[s, 1, e]))

    def issue(blk, s):
        for e in range(N_EXPERTS):
            for cp in copies(blk, e, s):
                cp.start()

    @pl.when(b == 0)
    def _():
        issue(0, 0)

    @pl.when(b + 1 < nblk)
    def _():
        issue(b + 1, 1 - slot)

    def placement(e, w, start):
        rank = rm_ref[e:e + 1, :]
        rel = rank + (base_ref[b * N_EXPERTS + e] - first_row(b, e)).astype(F32)
        lo_row = w * jnp.float32(w_rows)
        mine = (rank >= 0.0) & (rel >= lo_row) & (rel < lo_row + w_rows)
        target = jnp.where(mine, rel + (first_row(b, e) - start).astype(F32), -1.0)
        return jnp.where(r_iota == target, 1.0, 0.0).astype(BF16)

    for e in range(N_EXPERTS):
        for cp in copies(b, e, slot):
            cp.wait()
    put = jnp.concatenate([placement(e, 0, window(b, e, 0)) for e in range(N_EXPERTS)], axis=0)
    rows_hi = buf_hi[slot].reshape(N_EXPERTS * w_rows, -1)
    rows_lo = buf_lo[slot].reshape(N_EXPERTS * w_rows, -1)
    acc[...] = _tn(put, rows_hi) + _tn(put, rows_lo)

    for e in range(N_EXPERTS):
        used = base_ref[b * N_EXPERTS + e] - first_row(b, e) + cnt_ref[b * N_EXPERTS + e]

        def extra(w, carry, e=e):
            s = window(b, e, w)
            c_hi = pltpu.make_async_copy(hi_ref.at[e, pl.ds(s, w_rows), :], xbuf_hi, xsem.at[0])
            c_lo = pltpu.make_async_copy(lo_ref.at[e, pl.ds(s, w_rows), :], xbuf_lo, xsem.at[1])
            c_hi.start()
            c_lo.start()
            c_hi.wait()
            c_lo.wait()
            p = placement(e, w, s)
            acc[...] += _tn(p, xbuf_hi[...]) + _tn(p, xbuf_lo[...])
            return carry

        lax.fori_loop(1, (used + w_rows - 1) // w_rows, extra, 0)

    o_ref[...] = _ln(DN_ALPHA * x_ref[...] + (1.0 + gate_ref[...]) * acc[...]) * lng_ref[...] + lnb_ref[...]


def combine_post_norm(ye_hi, ye_lo, m0, rm, x1, mod, L, ln_g, ln_b):
    e, m_rows, d = ye_hi.shape
    n = x1.shape[0]
    nblk = n // TOK_BLK
    nm = mod.shape[0]
    per_seq = L // TOK_BLK
    rm_t = rm.transpose(1, 0, 2)
    cnt = jnp.sum(rm_t >= 0.0, axis=-1).astype(jnp.int32)
    base = m0 + jnp.cumsum(cnt, axis=0) - cnt
    rows = pl.BlockSpec((TOK_BLK, d), lambda i, *_: (i, 0))
    vec = pl.BlockSpec((1, d), lambda i, *_: (0, 0))
    return pl.pallas_call(
        functools.partial(_combine_kernel, m_rows=m_rows, nblk=nblk),
        grid_spec=pltpu.PrefetchScalarGridSpec(
            num_scalar_prefetch=2,
            grid=(nblk,),
            in_specs=[pl.BlockSpec((None, e, TOK_BLK), lambda i, *_: (i, 0, 0)),
                      rows,
                      pl.BlockSpec((None, None, 1, d), lambda i, *_: (0 if nm == 1 else i // per_seq, 5, 0, 0)),
                      vec, vec,
                      pl.BlockSpec(memory_space=pl.ANY), pl.BlockSpec(memory_space=pl.ANY)],
            out_specs=rows,
            scratch_shapes=[pltpu.VMEM((2, e, COMBINE_ROWS, d), BF16), pltpu.VMEM((2, e, COMBINE_ROWS, d), BF16),
                            pltpu.SemaphoreType.DMA((2, 2, e)),
                            pltpu.VMEM((COMBINE_ROWS, d), BF16), pltpu.VMEM((COMBINE_ROWS, d), BF16),
                            pltpu.SemaphoreType.DMA((2,)),
                            pltpu.VMEM((TOK_BLK, d), F32)]),
        out_shape=jax.ShapeDtypeStruct((n, d), F32),
        compiler_params=pltpu.CompilerParams(dimension_semantics=("arbitrary",), vmem_limit_bytes=VMEM_LIMIT_BYTES),
        name="combine_post_norm",
    )(base.reshape(-1), cnt.reshape(-1), rm_t, x1, mod.reshape(nm, 6, 1, d), ln_g.reshape(1, -1),
      ln_b.reshape(1, -1), ye_hi, ye_lo)


def expert_choice_ffn(hfs, logits, w_gate, w_up, w_down, layer):
    xes, gates, tables, offs, off = [], [], [], [], 0
    for hf, lg in zip(hfs, logits):
        aff_t = jax.nn.softmax(lg, axis=-1).T
        rm, idx = route(aff_t)
        xes.append(hf[idx])
        gates.append(jnp.take_along_axis(aff_t, idx, axis=1))
        tables.append(rm)
        offs.append(off)
        off += idx.shape[1]
    gate = jnp.concatenate(gates, axis=1)[..., None]
    hid = expert_hidden(jnp.concatenate(xes, axis=1), w_gate, w_up, layer)
    ye_hi, ye_lo = expert_down(hid, w_down, gate, layer)
    return ye_hi, ye_lo, tables, offs


_IN_SIZES = (QK_A, QK_A, V_A, V_A, 2 * H_A, 2 * H_A, QK_B, QK_B, V_B, V_B, 2 * H_B, 2 * H_B,
             3 * HY_CH, D_MODEL, D_MODEL, D_MODEL)
_IN_OFFS = tuple(int(v) for v in np.concatenate([[0], np.cumsum(_IN_SIZES)]))
_SMALL_PARTS = (4, 5, 10, 11)


def trunk_layer(x_sets, mods, lp, experts, layer, mats, states):
    big_cols = [p for p in range(len(_IN_SIZES)) if p not in _SMALL_PARTS]
    w_in = lp['w_in']
    w_big = jnp.concatenate([w_in[:, _IN_OFFS[p]:_IN_OFFS[p + 1]].astype(BF16) for p in big_cols], -1)
    w_small = jnp.concatenate([w_in[:, _IN_OFFS[p]:_IN_OFFS[p + 1]] for p in _SMALL_PARTS], -1)
    w_small = jnp.pad(w_small, ((0, 0), (0, LANES - w_small.shape[1]))).astype(BF16)
    w_out = lp['w_out'].astype(BF16)

    x1s, h2s, logits, new_states = [], [], [], []
    for x, mod, st in zip(x_sets, mods, states):
        b, L, _ = x.shape
        shift1, scale1 = mod[..., :D_MODEL], mod[..., D_MODEL:2 * D_MODEL]
        h = (_ln_stats(x) * (1.0 + scale1) + shift1).reshape(b * L, D_MODEL).astype(BF16)
        proj_big = pmm(h, w_big)
        proj_small = pmm(h, w_small)
        (y_a, y_b, y_c), st_new = seq_mixers(proj_big, proj_small, 0, b, L, lp, mats, *st)
        new_states.append(st_new)
        merged = branch_merge(y_a, y_b, y_c, lp['w_br_a'], lp['w_br_b'], lp['w_br_c'], proj_big)
        x1, h2, lg = mix_out(merged, 0, x.reshape(b * L, D_MODEL), mod, L, w_out, lp['ln1_g'], lp['ln1_b'],
                             lp['w_router'], lp['b_router'])
        x1s.append(x1)
        h2s.append(h2)
        logits.append(lg[:, :N_EXPERTS])
    ye_hi, ye_lo, tables, offs = expert_choice_ffn(h2s, logits, *experts, layer)
    out_sets = []
    for x, x1, rm, m0, mod in zip(x_sets, x1s, tables, offs, mods):
        out = combine_post_norm(ye_hi, ye_lo, m0, rm, x1, mod, x.shape[1], lp['ln2_g'], lp['ln2_b'])
        out_sets.append(out.reshape(x.shape))
    return out_sets, new_states


def kernel(x_prompt, x_sample, state_gdn, state_mlstm_c, state_mlstm_n, state_mlstm_m, c, c_ctx, w_ada, b_ada, w_in, gdn_conv, gdn_a_log, gdn_dt_bias, gdn_norm, ml_i_bias, ml_f_bias, ml_norm, hy_conv, hy_w1, hy_b1, hy_w2, hy_b2, hy_w3, hy_b3, hy_freq, hy_rate, hy_skip, w_br_a, w_br_b, w_br_c, w_out, ln1_g, ln1_b, ln2_g, ln2_b, w_router, b_router, w_gate, w_up, w_down):
    bp = x_prompt.shape[0]
    ls = x_sample.shape[1]
    xp = x_prompt
    xs = x_sample + grid_pos_embed(ls)[None]
    zero_states = (jnp.zeros((bp, 2, H_A, DK_A, DV_A), F32), jnp.zeros((bp, 2, H_B, DQK_B, DV_B), F32),
                   jnp.zeros((bp, 2, H_B, DQK_B), F32), jnp.zeros((bp, 2, H_B), F32))
    cond = jax.nn.silu(jnp.concatenate([c_ctx[None], c], 0))
    cond = jnp.pad(cond, ((0, 16 - cond.shape[0]), (0, 0))).astype(BF16)
    ctx_states = []
    mats = {}
    for L in (x_prompt.shape[1], ls):
        m32 = dft_matrices(L)
        mats[L] = (m32, tuple(m.astype(BF16) for m in m32))
    stacked = dict(w_in=w_in, gdn_conv=gdn_conv, gdn_a_log=gdn_a_log, gdn_dt_bias=gdn_dt_bias, gdn_norm=gdn_norm,
                   ml_i_bias=ml_i_bias, ml_f_bias=ml_f_bias, ml_norm=ml_norm, hy_conv=hy_conv, hy_w1=hy_w1,
                   hy_b1=hy_b1, hy_w2=hy_w2, hy_b2=hy_b2, hy_w3=hy_w3, hy_b3=hy_b3, hy_freq=hy_freq,
                   hy_rate=hy_rate, hy_skip=hy_skip, w_br_a=w_br_a, w_br_b=w_br_b, w_br_c=w_br_c, w_out=w_out,
                   ln1_g=ln1_g, ln1_b=ln1_b, ln2_g=ln2_g, ln2_b=ln2_b, w_router=w_router, b_router=b_router,
                   w_gate=w_gate, w_up=w_up, w_down=w_down)
    for l in range(DEPTH):
        lp = {name: t[l] for name, t in stacked.items()}
        mod = ada_modulation(cond, w_ada, b_ada, l)
        mod_ctx = mod[0].reshape(1, 1, 6 * D_MODEL)
        mod_lat = mod[1:1 + c.shape[0]][:, None, :]
        lat_states = (state_gdn[:, l], state_mlstm_c[:, l], state_mlstm_n[:, l], state_mlstm_m[:, l])
        (xp, xs), (st_ctx, _) = trunk_layer([xp, xs], [mod_ctx, mod_lat], lp, (w_gate, w_up, w_down), l, mats,
                                            [zero_states, lat_states])
        ctx_states.append(st_ctx)
    outs = tuple(jnp.stack([st[i] for st in ctx_states], 1) for i in range(4))
    return (xp, xs) + outs
```
